```python
import math
import jax, jax.numpy as jnp
from jax import lax
import numpy as np

D_MODEL = 2048
BATCH = 4
SEQ = 2048
DEPTH = 2

ATT_HEAD_DIM = D_MODEL // 16
ATT_HEADS = 8
ATT_KV_HEADS = 2
ATT_WIDTH = ATT_HEADS * ATT_HEAD_DIM
ATT_KV_WIDTH = ATT_KV_HEADS * ATT_HEAD_DIM
WINDOW = 128
ATT_BLOCK = 128
ROPE_THETA = 10000.0
M_HEADS = 4
M_V_DIM = D_MODEL // 8
M_QK_DIM = M_V_DIM // 2
M_WIDTH = M_HEADS * M_V_DIM
M_QK_WIDTH = M_HEADS * M_QK_DIM
M_CHUNK = 64
MIX_WIDTH = ATT_WIDTH + M_WIDTH
IN_SIZES = (ATT_WIDTH, ATT_KV_WIDTH, ATT_KV_WIDTH, M_QK_WIDTH, M_QK_WIDTH, M_WIDTH, M_WIDTH, 4 * M_HEADS)
IN_WIDTH = ATT_WIDTH + 2 * ATT_KV_WIDTH + 2 * M_QK_WIDTH + 2 * M_WIDTH + 4 * M_HEADS
N_EXPERTS = 16
EXPERT_FF = D_MODEL // 2
CAPACITY_FACTOR = 2
EPS = 1e-6

kernel_name = "hymba_style_swa_mlstm_ec_moe_encoder"


def rms_norm(x):
    xf = x.astype(jnp.float32)
    return (xf * lax.rsqrt(jnp.mean(xf * xf, axis=-1, keepdims=True) + EPS)).astype(x.dtype)


def split_columns(t, sizes):
    offs = np.cumsum(np.array(sizes))[:-1].tolist()
    return jnp.split(t, offs, axis=-1)


def rope_tables(seq, dim, dtype):
    inv = 1.0 / (ROPE_THETA ** (jnp.arange(0, dim, 2, dtype=jnp.float32) / dim))
    ang = jnp.arange(seq, dtype=jnp.float32)[:, None] * inv[None, :]
    ang = jnp.concatenate([ang, ang], axis=-1)
    return jnp.cos(ang).astype(dtype), jnp.sin(ang).astype(dtype)


def apply_rope(x, cos, sin):
    x1, x2 = jnp.split(x, 2, axis=-1)
    rot = jnp.concatenate([-x2, x1], axis=-1)
    return x * cos[None, :, None, :] + rot * sin[None, :, None, :]


def windowed_gqa(q, k, v, sink):
    B, S, H, Dh = q.shape
    KV = k.shape[2]
    G = H // KV
    L = ATT_BLOCK
    nb = S // L
    qb = q.reshape(B, nb, L, KV, G, Dh)

    def band(t):
        tp = jnp.pad(t, ((0, 0), (L, L), (0, 0), (0, 0)))
        tb = tp.reshape(B, nb + 2, L, KV, Dh)
        return jnp.concatenate([tb[:, :-2], tb[:, 1:-1], tb[:, 2:]], axis=2)

    kb, vb = band(k), band(v)
    scores = jnp.einsum('bnqkgd,bnjkd->bnkgqj', qb, kb).astype(jnp.float32) * (Dh ** -0.5)
    blk = jnp.arange(nb)[:, None, None]
    qpos = blk * L + jnp.arange(L)[None, :, None]
    kpos = (blk - 1) * L + jnp.arange(3 * L)[None, None, :]
    mask = (jnp.abs(kpos - qpos) <= WINDOW) & (kpos >= 0) & (kpos < S)
    scores = jnp.where(mask[None, :, None, None], scores, jnp.finfo(jnp.float32).min)
    sink_col = jnp.broadcast_to(sink.astype(jnp.float32).reshape(1, 1, KV, G, 1, 1), scores.shape[:-1] + (1,))
    probs = jax.nn.softmax(jnp.concatenate([scores, sink_col], axis=-1), axis=-1)[..., :-1]
    out = jnp.einsum('bnkgqj,bnjkd->bnqkgd', probs.astype(v.dtype), vb)
    return out.reshape(B, S, H * Dh)


def mlstm_chunkwise(q, k, v, i_pre, f_pre):
    B, H, S, Dk = q.shape
    Dv = v.shape[-1]
    L = M_CHUNK
    nc = S // L
    q = q.astype(jnp.float32).reshape(B, H, nc, L, Dk) * (Dk ** -0.5)
    k = k.astype(jnp.float32).reshape(B, H, nc, L, Dk)
    v = v.astype(jnp.float32).reshape(B, H, nc, L, Dv)
    logf = jax.nn.log_sigmoid(f_pre.astype(jnp.float32)).reshape(B, H, nc, L)
    ig = i_pre.astype(jnp.float32).reshape(B, H, nc, L)
    b = jnp.cumsum(logf, axis=-1)
    b_tot = b[..., -1]
    lower = jnp.tril(jnp.ones((L, L), dtype=bool))
    dmat = jnp.where(lower, b[..., :, None] - b[..., None, :] + ig[..., None, :], -jnp.inf)
    w_state = b_tot[..., None] - b + ig
    m_loc = jnp.max(w_state, axis=-1)
    ek = jnp.exp(w_state - m_loc[..., None])[..., None] * k
    c_loc = jnp.einsum('bhclk,bhclv->bhckv', ek, v)
    n_loc = jnp.sum(ek, axis=3)

    def step(carry, inp):
        c_st, n_st, m_st = carry
        cl, nl, ml, bt = inp
        m_new = jnp.maximum(bt + m_st, ml)
        a = jnp.exp(bt + m_st - m_new)
        g = jnp.exp(ml - m_new)
        c_new = a[..., None, None] * c_st + g[..., None, None] * cl
        n_new = a[..., None] * n_st + g[..., None] * nl
        return (c_new, n_new, m_new), (c_st, n_st, m_st)

    init = (jnp.zeros((B, H, Dk, Dv), jnp.float32), jnp.zeros((B, H, Dk), jnp.float32),
            jnp.zeros((B, H), jnp.float32))
    xs = (jnp.moveaxis(c_loc, 2, 0), jnp.moveaxis(n_loc, 2, 0), jnp.moveaxis(m_loc, 2, 0), jnp.moveaxis(b_tot, 2, 0))
    _, (c_prev, n_prev, m_prev) = lax.scan(step, init, xs)
    c_prev = jnp.moveaxis(c_prev, 0, 2)
    n_prev = jnp.moveaxis(n_prev, 0, 2)
    m_prev = jnp.moveaxis(m_prev, 0, 2)
    g_inter = b + m_prev[..., None]
    m_t = jnp.maximum(jnp.max(dmat, axis=-1), g_inter)
    e_inter = jnp.exp(g_inter - m_t)
    s = jnp.einsum('bhctk,bhcsk->bhcts', q, k) * jnp.exp(dmat - m_t[..., None])
    num = jnp.einsum('bhcts,bhcsv->bhctv', s, v) + e_inter[..., None] * jnp.einsum('bhctk,bhckv->bhctv', q, c_prev)
    den = jnp.sum(s, axis=-1) + e_inter * jnp.einsum('bhctk,bhck->bhct', q, n_prev)
    h = num / jnp.maximum(jnp.abs(den), jnp.exp(-m_t))[..., None]
    return h.reshape(B, H, S, Dv)


def mlstm_bidirectional(q, k, v, i_fwd, f_fwd, i_bwd, f_bwd):
    flip = lambda t: jnp.flip(t, axis=2)
    h_fwd = mlstm_chunkwise(q, k, v, i_fwd, f_fwd)
    h_bwd = flip(mlstm_chunkwise(flip(q), flip(k), flip(v), flip(i_bwd), flip(f_bwd)))
    return h_fwd + h_bwd


def expert_choice_moe(h, w_router, w_gate, w_up, w_down):
    B, S, D = h.shape
    cap = CAPACITY_FACTOR * S // N_EXPERTS
    aff = jax.nn.softmax((h @ w_router).astype(jnp.float32), axis=-1)
    gates, idx = lax.top_k(jnp.swapaxes(aff, 1, 2), cap)
    xe = jax.vmap(lambda hb, ib: hb[ib])(h, idx)
    hid = jax.nn.silu(jnp.einsum('becd,edf->becf', xe, w_gate)) * jnp.einsum('becd,edf->becf', xe, w_up)
    ye = jnp.einsum('becf,efd->becd', hid, w_down) * gates[..., None].astype(h.dtype)
    flat_idx = (jnp.arange(B)[:, None, None] * S + idx).reshape(-1)
    out = jax.ops.segment_sum(ye.reshape(-1, D), flat_idx, num_segments=B * S)
    return out.reshape(B, S, D)


def hybrid_layer(x, c_act, w_ada, b_ada, w_in, b_gates, q_gain, k_gain, sink, m_gain, w_out,
                 w_router, w_gate, w_up, w_down, cos, sin):
    B, S, _ = x.shape
    mod = (c_act @ w_ada + b_ada)[:, None, :]
    sh1, sc1, g1, sh2, sc2, g2 = jnp.split(mod, 6, axis=-1)
    h = rms_norm(x) * (1 + sc1) + sh1
    aq, ak, av, mq, mk, mv, mo, mg = split_columns(h @ w_in, IN_SIZES)
    aq = rms_norm(aq.reshape(B, S, ATT_HEADS, ATT_HEAD_DIM)) * q_gain
    ak = rms_norm(ak.reshape(B, S, ATT_KV_HEADS, ATT_HEAD_DIM)) * k_gain
    av = av.reshape(B, S, ATT_KV_HEADS, ATT_HEAD_DIM)
    att_out = windowed_gqa(apply_rope(aq, cos, sin), apply_rope(ak, cos, sin), av, sink)
    to_heads = lambda t, d: jnp.transpose(t.reshape(B, S, M_HEADS, d), (0, 2, 1, 3))
    gate_pre = mg.astype(jnp.float32).reshape(B, S, 4, M_HEADS) + b_gates.astype(jnp.float32)
    gate_pre = jnp.transpose(gate_pre, (2, 0, 3, 1))
    hm = mlstm_bidirectional(to_heads(mq, M_QK_DIM), to_heads(mk, M_QK_DIM), to_heads(mv, M_V_DIM),
                             gate_pre[0], gate_pre[2], gate_pre[1], gate_pre[3])
    hm = rms_norm(jnp.transpose(hm, (0, 2, 1, 3))).astype(x.dtype) * m_gain.reshape(M_HEADS, M_V_DIM)
    m_out = hm.reshape(B, S, M_WIDTH) * jax.nn.sigmoid(mo)
    mix = jnp.concatenate([att_out, m_out], axis=-1) @ w_out
    x = x + g1 * mix
    h2 = rms_norm(x) * (1 + sc2) + sh2
    x = x + g2 * expert_choice_moe(h2, w_router, w_gate, w_up, w_down)
    return x


def setup_inputs(seed: int = 0) -> dict:
    key = jax.random.key(seed)
    ks = jax.random.split(key, 16)
    f32 = jnp.float32
    nrm = lambda k, shape, scale: jax.random.normal(k, shape, f32) * scale
    f_bias = jnp.linspace(3.0, 6.0, M_HEADS, dtype=f32)
    gate_noise = nrm(ks[5], (DEPTH, 4, M_HEADS), 0.1)
    b_gates = gate_noise + jnp.stack([jnp.zeros((M_HEADS,), f32), jnp.zeros((M_HEADS,), f32), f_bias, f_bias])[None]
    return {
        "x": nrm(ks[0], (BATCH, SEQ, D_MODEL), 1.0),
        "c": nrm(ks[1], (BATCH, D_MODEL), 1.0),
        "w_ada": nrm(ks[2], (DEPTH, D_MODEL, 6 * D_MODEL), 0.5 * D_MODEL ** -0.5),
        "b_ada": nrm(ks[3], (DEPTH, 6 * D_MODEL), 0.01),
        "w_in": nrm(ks[4], (DEPTH, D_MODEL, IN_WIDTH), D_MODEL ** -0.5),
        "b_gates": b_gates,
        "q_gain": 1.0 + nrm(ks[6], (DEPTH, ATT_HEAD_DIM), 0.1),
        "k_gain": 1.0 + nrm(ks[7], (DEPTH, ATT_HEAD_DIM), 0.1),
        "sink": nrm(ks[8], (DEPTH, ATT_HEADS), 1.0),
        "m_gain": 1.0 + nrm(ks[9], (DEPTH, M_WIDTH), 0.1),
        "w_out": nrm(ks[10], (DEPTH, MIX_WIDTH, D_MODEL), MIX_WIDTH ** -0.5),
        "w_router": nrm(ks[11], (DEPTH, D_MODEL, N_EXPERTS), D_MODEL ** -0.5),
        "w_gate": nrm(ks[12], (DEPTH, N_EXPERTS, D_MODEL, EXPERT_FF), D_MODEL ** -0.5),
        "w_up": nrm(ks[13], (DEPTH, N_EXPERTS, D_MODEL, EXPERT_FF), D_MODEL ** -0.5),
        "w_down": nrm(ks[14], (DEPTH, N_EXPERTS, EXPERT_FF, D_MODEL), EXPERT_FF ** -0.5),
    }


def reference(x, c, w_ada, b_ada, w_in, b_gates, q_gain, k_gain, sink, m_gain, w_out,
              w_router, w_gate, w_up, w_down):
    cos, sin = rope_tables(x.shape[1], ATT_HEAD_DIM, x.dtype)
    c_act = jax.nn.silu(c)
    for l in range(DEPTH):
        x = hybrid_layer(x, c_act, w_ada[l], b_ada[l], w_in[l], b_gates[l], q_gain[l], k_gain[l], sink[l],
                         m_gain[l], w_out[l], w_router[l], w_gate[l], w_up[l], w_down[l], cos, sin)
    return x
```

```python
import functools

import jax
import jax.numpy as jnp
from jax import lax
from jax.experimental import pallas as pl
from jax.experimental.pallas import tpu as pltpu

F32 = jnp.float32
BF16 = jnp.bfloat16
HIGHEST = lax.Precision.HIGHEST

D_MODEL = 2048
ATT_HEAD_DIM = 128
ATT_HEADS = 8
ATT_KV_HEADS = 2
ATT_GROUP = ATT_HEADS // ATT_KV_HEADS
ATT_WIDTH = ATT_HEADS * ATT_HEAD_DIM
ATT_KV_WIDTH = ATT_KV_HEADS * ATT_HEAD_DIM
WINDOW = 128
ATT_BLOCK = 128
ROPE_THETA = 10000.0
M_HEADS = 4
M_V_DIM = 256
M_QK_DIM = 128
M_WIDTH = M_HEADS * M_V_DIM
M_QK_WIDTH = M_HEADS * M_QK_DIM
M_CHUNK = 64
N_GATES = 4 * M_HEADS
N_EXPERTS = 16
EXPERT_FF = D_MODEL // 2
CAPACITY_FACTOR = 2
EPS = 1e-6
NEG = -1e30

COL_AQ = 0
COL_MV = ATT_WIDTH
COL_MO = COL_MV + M_WIDTH
COL_AK = COL_MO + M_WIDTH
COL_AV = COL_AK + ATT_KV_WIDTH
COL_MQ = COL_AV + ATT_KV_WIDTH
COL_MK = COL_MQ + M_QK_WIDTH
PROJ_WIDTH = COL_MK + M_QK_WIDTH

VMEM_LIMIT_BYTES = 56 * 1024 * 1024

ADA_TN = 1024
INPROJ_TM = 1024
INPROJ_TN = 512
OUTPROJ_TM = 512
OUTPROJ_TN = 512
FFN_TF = 256
COMBINE_TQ = 1024


def _params(*sem):
    return pltpu.CompilerParams(dimension_semantics=sem, vmem_limit_bytes=VMEM_LIMIT_BYTES)


def _ada_kernel(c_ref, w_ref, b_ref, o_ref):
    c = c_ref[...]
    c_act = c * jax.nn.sigmoid(c)
    o_ref[0] = jnp.dot(c_act, w_ref[0], precision=HIGHEST, preferred_element_type=F32) + b_ref[0]


def _ada(c_pad, w_ada, b_ada):
    depth, d, n = w_ada.shape
    rows = c_pad.shape[0]
    return pl.pallas_call(
        _ada_kernel,
        grid=(depth, n // ADA_TN),
        in_specs=[
            pl.BlockSpec((rows, d), lambda l, j: (0, 0)),
            pl.BlockSpec((1, d, ADA_TN), lambda l, j: (l, 0, j)),
            pl.BlockSpec((1, 1, ADA_TN), lambda l, j: (l, 0, j)),
        ],
        out_specs=pl.BlockSpec((1, rows, ADA_TN), lambda l, j: (l, 0, j)),
        out_shape=jax.ShapeDtypeStruct((depth, rows, n), F32),
        compiler_params=_params("arbitrary", "arbitrary"),
        name="ada_mod",
    )(c_pad, w_ada, b_ada.reshape(depth, 1, n))


def _inproj_kernel(x_ref, sc_ref, sh_ref, w_ref, wg_ref, cos_ref, sin_ref, qg_ref, kg_ref,
                   o_ref, g_ref, h_scr):
    j = pl.program_id(2)

    @pl.when(j == 0)
    def _():
        x = x_ref[0]
        r = lax.rsqrt(jnp.mean(x * x, axis=-1, keepdims=True) + EPS)
        h = (x * r) * (1.0 + sc_ref[0]) + sh_ref[0]
        hb = h.astype(BF16)
        h_scr[...] = hb
        g_ref[0] = jnp.dot(hb, wg_ref[...], preferred_element_type=F32)

    y = jnp.dot(h_scr[...], w_ref[...], preferred_element_type=F32)

    def norm_rope(t, gain):
        tn = t * lax.rsqrt(jnp.mean(t * t, axis=-1, keepdims=True) + EPS) * gain
        return tn * cos_ref[...] + pltpu.roll(tn, ATT_HEAD_DIM // 2, 1) * sin_ref[...]

    heads_per_tile = INPROJ_TN // ATT_HEAD_DIM
    q_tiles = ATT_WIDTH // INPROJ_TN
    kv_tile = COL_AK // INPROJ_TN
    scale = ATT_HEAD_DIM ** -0.5

    @pl.when(j < q_tiles)
    def _():
        for u in range(heads_per_tile):
            sl = slice(u * ATT_HEAD_DIM, (u + 1) * ATT_HEAD_DIM)
            o_ref[0, :, sl] = (norm_rope(y[:, sl], qg_ref[...]) * scale).astype(BF16)

    @pl.when(j == kv_tile)
    def _():
        for u in range(ATT_KV_HEADS):
            sl = slice(u * ATT_HEAD_DIM, (u + 1) * ATT_HEAD_DIM)
            o_ref[0, :, sl] = norm_rope(y[:, sl], kg_ref[...]).astype(BF16)
        o_ref[0, :, ATT_KV_WIDTH:] = y[:, ATT_KV_WIDTH:].astype(BF16)

    @pl.when(jnp.logical_and(j >= q_tiles, j != kv_tile))
    def _():
        o_ref[0] = y.astype(BF16)


def _inproj(x, mod6, w_main, w_gate, cos, sin_signed, q_gain, k_gain):
    b, s, d = x.shape
    tm, tn = INPROJ_TM, INPROJ_TN
    assert COL_AK % tn == 0 and ATT_WIDTH % tn == 0 and 2 * ATT_KV_WIDTH == tn
    return pl.pallas_call(
        _inproj_kernel,
        grid=(b, s // tm, PROJ_WIDTH // tn),
        in_specs=[
            pl.BlockSpec((1, tm, d), lambda bi, i, j: (bi, i, 0)),
            pl.BlockSpec((1, 1, d), lambda bi, i, j: (bi * 6 + 1, 0, 0)),
            pl.BlockSpec((1, 1, d), lambda bi, i, j: (bi * 6 + 0, 0, 0)),
            pl.BlockSpec((d, tn), lambda bi, i, j: (0, j)),
            pl.BlockSpec((d, N_GATES), lambda bi, i, j: (0, 0)),
            pl.BlockSpec((tm, ATT_HEAD_DIM), lambda bi, i, j: (i, 0)),
            pl.BlockSpec((tm, ATT_HEAD_DIM), lambda bi, i, j: (i, 0)),
            pl.BlockSpec((1, ATT_HEAD_DIM), lambda bi, i, j: (0, 0)),
            pl.BlockSpec((1, ATT_HEAD_DIM), lambda bi, i, j: (0, 0)),
        ],
        out_specs=[
            pl.BlockSpec((1, tm, tn), lambda bi, i, j: (bi, i, j)),
            pl.BlockSpec((1, tm, N_GATES), lambda bi, i, j: (bi, i, 0)),
        ],
        out_shape=[
            jax.ShapeDtypeStruct((b, s, PROJ_WIDTH), BF16),
            jax.ShapeDtypeStruct((b, s, N_GATES), F32),
        ],
        scratch_shapes=[pltpu.VMEM((tm, d), BF16)],
        compiler_params=_params("arbitrary", "arbitrary", "arbitrary"),
        name="in_proj",
    )(x, mod6, mod6, w_main, w_gate, cos, sin_signed, q_gain, k_gain)


def _attn_kernel(sink_ref, q_ref, kl_ref, kc_ref, kr_ref, vl_ref, vc_ref, vr_ref, o_ref, *, n_blocks):
    n = pl.program_id(1)
    L = ATT_BLOCK
    rows = ATT_GROUP * L
    i = lax.broadcasted_iota(jnp.int32, (rows, 3 * L), 0) & (L - 1)
    jj = lax.broadcasted_iota(jnp.int32, (rows, 3 * L), 1)
    dist = jj - L - i
    valid = (jnp.abs(dist) <= WINDOW) & ((jj >= L) | (n > 0)) & ((jj < 2 * L) | (n < n_blocks - 1))
    rgrp = lax.broadcasted_iota(jnp.int32, (rows, 1), 0) // L
    for kv in range(ATT_KV_HEADS):
        hs = slice(kv * ATT_HEAD_DIM, (kv + 1) * ATT_HEAD_DIM)
        q = jnp.concatenate(
            [q_ref[0, :, (kv * ATT_GROUP + g) * ATT_HEAD_DIM:(kv * ATT_GROUP + g + 1) * ATT_HEAD_DIM]
             for g in range(ATT_GROUP)], axis=0)
        kb = jnp.concatenate([kl_ref[0, :, hs], kc_ref[0, :, hs], kr_ref[0, :, hs]], axis=0)
        vb = jnp.concatenate([vl_ref[0, :, hs], vc_ref[0, :, hs], vr_ref[0, :, hs]], axis=0)
        s = lax.dot_general(q, kb, (((1,), (1,)), ((), ())), preferred_element_type=F32)
        s = jnp.where(valid, s, NEG)
        sink = jnp.zeros((rows, 1), F32)
        for g in range(ATT_GROUP):
            sink = jnp.where(rgrp == g, sink_ref[kv * ATT_GROUP + g], sink)
        m = jnp.maximum(jnp.max(s, axis=-1, keepdims=True), sink)
        p = jnp.exp(s - m)
        denom = jnp.sum(p, axis=-1, keepdims=True) + jnp.exp(sink - m)
        o = jnp.dot(p.astype(BF16), vb, preferred_element_type=F32) / denom
        for g in range(ATT_GROUP):
            h = kv * ATT_GROUP + g
            o_ref[0, :, h * ATT_HEAD_DIM:(h + 1) * ATT_HEAD_DIM] = o[g * L:(g + 1) * L].astype(BF16)


def _attention(proj, sink):
    b, s, _ = proj.shape
    L = ATT_BLOCK
    nb = s // L
    kblk = COL_AK // ATT_KV_WIDTH
    vblk = COL_AV // ATT_KV_WIDTH
    left = lambda n: jnp.maximum(n - 1, 0)
    right = lambda n: jnp.minimum(n + 1, nb - 1)
    kv_spec = lambda col, f: pl.BlockSpec((1, L, ATT_KV_WIDTH), lambda bi, n: (bi, f(n), col))
    same = lambda n: n
    return pl.pallas_call(
        functools.partial(_attn_kernel, n_blocks=nb),
        grid=(b, nb),
        in_specs=[
            pl.BlockSpec(memory_space=pltpu.SMEM),
            pl.BlockSpec((1, L, ATT_WIDTH), lambda bi, n: (bi, n, COL_AQ // ATT_WIDTH)),
            kv_spec(kblk, left), kv_spec(kblk, same), kv_spec(kblk, right),
            kv_spec(vblk, left), kv_spec(vblk, same), kv_spec(vblk, right),
        ],
        out_specs=pl.BlockSpec((1, L, ATT_WIDTH), lambda bi, n: (bi, n, 0)),
        out_shape=jax.ShapeDtypeStruct((b, s, ATT_WIDTH), BF16),
        compiler_params=_params("arbitrary", "arbitrary"),
        name="window_attn",
    )(sink, proj, proj, proj, proj, proj, proj, proj)


def _mlstm_kernel(q_ref, k_ref, v_ref, mo_ref, gcol_ref, grow_ref, bcol_ref, brow_ref, gain_ref, o_ref,
                  bc_scr, wc_scr, bt_scr, ar_scr, c_scr, n_scr, h_scr, *, seq):
    L = M_CHUNK
    nc = seq // L
    ns = 2 * M_HEADS
    scale = M_QK_DIM ** -0.5

    g = gcol_ref[0] + bcol_ref[...]
    ig = g[:, :ns]
    logf = jax.nn.log_sigmoid(g[:, ns:])
    t = lax.broadcasted_iota(jnp.int32, (seq, 1), 0) & (L - 1)
    pre = logf
    suf = logf
    sh = 1
    while sh < L:
        pre = pre + jnp.where(t >= sh, pltpu.roll(pre, sh, 0), 0.0)
        suf = suf + jnp.where(t + sh < L, pltpu.roll(suf, seq - sh, 0), 0.0)
        sh *= 2
    btot = pre + suf - logf
    is_fwd = lax.broadcasted_iota(jnp.int32, (1, ns), 1) < M_HEADS
    bcol = jnp.where(is_fwd, pre, suf)
    bc_scr[...] = bcol
    bt_scr[...] = btot
    wc_scr[...] = btot - bcol + ig

    gr = grow_ref[0] + brow_ref[...]
    ig_r = gr[:ns].reshape(ns * nc, L)
    logf_r = jax.nn.log_sigmoid(gr[ns:]).reshape(ns * nc, L)
    u = lax.broadcasted_iota(jnp.int32, (L, L), 0)
    w = lax.broadcasted_iota(jnp.int32, (L, L), 1)
    pre_r = jnp.dot(logf_r, (u <= w).astype(F32), precision=HIGHEST, preferred_element_type=F32)
    suf_r = jnp.dot(logf_r, (u >= w).astype(F32), precision=HIGHEST, preferred_element_type=F32)
    fwd_rows = lax.broadcasted_iota(jnp.int32, (ns * nc, 1), 0) < M_HEADS * nc
    ar_scr[...] = (ig_r - jnp.where(fwd_rows, pre_r, suf_r)).reshape(ns, nc, L)

    c_scr[...] = jnp.zeros_like(c_scr)
    n_scr[...] = jnp.zeros_like(n_scr)
    h_scr[...] = jnp.zeros_like(h_scr)

    row = lax.broadcasted_iota(jnp.int32, (L, L), 0)
    col = lax.broadcasted_iota(jnp.int32, (L, L), 1)
    masks = (col <= row, col >= row)

    def chunk_step(k, c, m_st):
        d, h = divmod(k, M_HEADS)
        r0 = pl.multiple_of(c * L, L)
        q = q_ref[0, pl.ds(r0, L), h * M_QK_DIM:(h + 1) * M_QK_DIM]
        kk = k_ref[0, pl.ds(r0, L), h * M_QK_DIM:(h + 1) * M_QK_DIM]
        v = v_ref[0, pl.ds(r0, L), h * M_V_DIM:(h + 1) * M_V_DIM]
        bc = bc_scr[pl.ds(r0, L), k:k + 1]
        wc = wc_scr[pl.ds(r0, L), k:k + 1]
        bt = bt_scr[pl.ds(r0, 1), k:k + 1]
        ar = ar_scr[k, pl.ds(c, 1), :]
        dm = jnp.where(masks[d], bc + ar, NEG)
        g_inter = bc + m_st
        m_t = jnp.maximum(jnp.max(dm, axis=-1, keepdims=True), g_inter)
        e_inter = jnp.exp(g_inter - m_t) * scale
        s_qk = lax.dot_general(q, kk, (((1,), (1,)), ((), ())), preferred_element_type=F32)
        p = s_qk * scale * jnp.exp(dm - m_t)
        q_c = jnp.dot(q, c_scr[k].astype(BF16), preferred_element_type=F32)
        num = jnp.dot(p.astype(BF16), v, preferred_element_type=F32) + e_inter * q_c
        q_n = jnp.sum(q.astype(F32) * n_scr[k], axis=-1, keepdims=True)
        den = jnp.sum(p, axis=-1, keepdims=True) + e_inter * q_n
        hh = num / jnp.maximum(jnp.abs(den), jnp.exp(-m_t))
        hsl = (pl.ds(r0, L), slice(h * M_V_DIM, (h + 1) * M_V_DIM))
        h_scr[hsl] = h_scr[hsl] + hh
        m_new = jnp.maximum(bt + m_st, jnp.max(wc, axis=0, keepdims=True))
        a = jnp.exp(bt + m_st - m_new)
        ekk = jnp.exp(wc - m_new) * kk.astype(F32)
        c_scr[k] = a * c_scr[k] + lax.dot_general(
            ekk.astype(BF16), v, (((0,), (0,)), ((), ())), preferred_element_type=F32)
        n_scr[k] = a * n_scr[k] + jnp.sum(ekk, axis=0, keepdims=True)
        return m_new

    def body(it, ms):
        out = []
        for k in range(ns):
            c = it if k < M_HEADS else nc - 1 - it
            out.append(chunk_step(k, c, ms[k]))
        return tuple(out)

    lax.fori_loop(0, nc, body, tuple(jnp.zeros((1, 1), F32) for _ in range(ns)))

    rt = 256

    def fin(i, carry):
        r0 = pl.multiple_of(i * rt, rt)
        for h in range(M_HEADS):
            sl = slice(h * M_V_DIM, (h + 1) * M_V_DIM)
            x = h_scr[pl.ds(r0, rt), sl]
            y = x * lax.rsqrt(jnp.mean(x * x, axis=-1, keepdims=True) + EPS) * gain_ref[:, sl]
            o_ref[0, pl.ds(r0, rt), sl] = (y * jax.nn.sigmoid(mo_ref[0, pl.ds(r0, rt), sl].astype(F32))).astype(BF16)
        return carry

    lax.fori_loop(0, seq // rt, fin, 0)


def _mlstm(proj, gates, b_gates, m_gain):
    b, s, _ = proj.shape
    L = M_CHUNK
    nc = s // L
    ns = 2 * M_HEADS
    grow = jnp.transpose(gates, (0, 2, 1)).reshape(b, N_GATES, nc, L)
    col = lambda width, off: pl.BlockSpec((1, s, width), lambda bi: (bi, 0, off // width))
    return pl.pallas_call(
        functools.partial(_mlstm_kernel, seq=s),
        grid=(b,),
        in_specs=[
            col(M_QK_WIDTH, COL_MQ), col(M_QK_WIDTH, COL_MK), col(M_WIDTH, COL_MV), col(M_WIDTH, COL_MO),
            pl.BlockSpec((1, s, N_GATES), lambda bi: (bi, 0, 0)),
            pl.BlockSpec((1, N_GATES, nc, L), lambda bi: (bi, 0, 0, 0)),
            pl.BlockSpec((1, N_GATES), lambda bi: (0, 0)),
            pl.BlockSpec((N_GATES, 1, 1), lambda bi: (0, 0, 0)),
            pl.BlockSpec((1, M_WIDTH), lambda bi: (0, 0)),
        ],
        out_specs=pl.BlockSpec((1, s, M_WIDTH), lambda bi: (bi, 0, 0)),
        out_shape=jax.ShapeDtypeStruct((b, s, M_WIDTH), BF16),
        scratch_shapes=[
            pltpu.VMEM((s, ns), F32), pltpu.VMEM((s, ns), F32), pltpu.VMEM((s, ns), F32),
            pltpu.VMEM((ns, nc, L), F32),
            pltpu.VMEM((ns, M_QK_DIM, M_V_DIM), F32),
            pltpu.VMEM((ns, 1, M_QK_DIM), F32),
            pltpu.VMEM((s, M_WIDTH), F32),
        ],
        compiler_params=_params("arbitrary"),
        name="mlstm",
    )(proj, proj, proj, proj, gates, grow, b_gates.reshape(1, N_GATES), b_gates.reshape(N_GATES, 1, 1),
      m_gain.reshape(1, M_WIDTH))


def _outproj_kernel(att_ref, mo_ref, wa_ref, wm_ref, x_ref, g1_ref, sc_ref, sh_ref, wr_ref,
                    x1_ref, h2_ref, aff_ref):
    j = pl.program_id(2)
    nj = pl.num_programs(2)
    tn = OUTPROJ_TN
    mix = (jnp.dot(att_ref[0], wa_ref[...], preferred_element_type=F32)
           + jnp.dot(mo_ref[0], wm_ref[...], preferred_element_type=F32))
    x1t = x_ref[0] + g1_ref[0] * mix
    for jj in range(D_MODEL // tn):
        @pl.when(j == jj)
        def _(jj=jj):
            x1_ref[0, :, jj * tn:(jj + 1) * tn] = x1t

    @pl.when(j == nj - 1)
    def _():
        xf = x1_ref[0]
        r = lax.rsqrt(jnp.mean(xf * xf, axis=-1, keepdims=True) + EPS)
        h2 = (xf * r) * (1.0 + sc_ref[0]) + sh_ref[0]
        h2_ref[0] = h2.astype(BF16)
        logits = jnp.dot(h2, wr_ref[...], precision=HIGHEST, preferred_element_type=F32)
        e = jnp.exp(logits - jnp.max(logits, axis=-1, keepdims=True))
        aff_ref[0] = e / jnp.sum(e, axis=-1, keepdims=True)


def _outproj(att, m_out, w_out, x, mod6, w_router):
    b, s, d = x.shape
    tm, tn = OUTPROJ_TM, OUTPROJ_TN
    half = ATT_WIDTH
    assert w_out.shape[0] == 2 * half and M_WIDTH == half
    return pl.pallas_call(
        _outproj_kernel,
        grid=(b, s // tm, d // tn),
        in_specs=[
            pl.BlockSpec((1, tm, ATT_WIDTH), lambda bi, i, j: (bi, i, 0)),
            pl.BlockSpec((1, tm, M_WIDTH), lambda bi, i, j: (bi, i, 0)),
            pl.BlockSpec((half, tn), lambda bi, i, j: (0, j)),
            pl.BlockSpec((half, tn), lambda bi, i, j: (1, j)),
            pl.BlockSpec((1, tm, tn), lambda bi, i, j: (bi, i, j)),
            pl.BlockSpec((1, 1, tn), lambda bi, i, j: (bi * 6 + 2, 0, j)),
            pl.BlockSpec((1, 1, d), lambda bi, i, j: (bi * 6 + 4, 0, 0)),
            pl.BlockSpec((1, 1, d), lambda bi, i, j: (bi * 6 + 3, 0, 0)),
            pl.BlockSpec((d, N_EXPERTS), lambda bi, i, j: (0, 0)),
        ],
        out_specs=[
            pl.BlockSpec((1, tm, d), lambda bi, i, j: (bi, i, 0)),
            pl.BlockSpec((1, tm, d), lambda bi, i, j: (bi, i, 0)),
            pl.BlockSpec((1, tm, N_EXPERTS), lambda bi, i, j: (bi, i, 0)),
        ],
        out_shape=[
            jax.ShapeDtypeStruct((b, s, d), F32),
            jax.ShapeDtypeStruct((b, s, d), BF16),
            jax.ShapeDtypeStruct((b, s, N_EXPERTS), F32),
        ],
        compiler_params=_params("arbitrary", "arbitrary", "arbitrary"),
        name="out_proj",
    )(att, m_out, w_out, w_out, x, mod6, mod6, mod6, w_router)


def _prefix_sum_lanes(x, n):
    lane = lax.broadcasted_iota(jnp.int32, x.shape, 1)
    sh = 1
    while sh < n:
        x = x + jnp.where(lane >= sh, pltpu.roll(x, sh, 1), 0.0)
        sh *= 2
    return x


def _select_kernel(aff_ref, pos_ref, gate_ref, *, cap, seq):
    a = aff_ref[0]
    bits = pltpu.bitcast(a, jnp.int32)
    ne = a.shape[0]

    def body(_, lohi):
        lo, hi = lohi
        mid = lo + ((hi - lo + 1) >> 1)
        cnt = jnp.sum(jnp.where(bits >= mid, 1.0, 0.0), axis=-1, keepdims=True)
        ok = cnt >= cap
        return jnp.where(ok, mid, lo), jnp.where(ok, hi, mid - 1)

    lo0 = jnp.zeros((ne, 1), jnp.int32)
    hi0 = jnp.full((ne, 1), 0x7F800000, jnp.int32)
    thr, _ = lax.fori_loop(0, 32, body, (lo0, hi0))
    gt = bits > thr
    tie = bits == thr
    n_gt = jnp.sum(jnp.where(gt, 1.0, 0.0), axis=-1, keepdims=True)
    tie_rank = _prefix_sum_lanes(jnp.where(tie, 1.0, 0.0), seq)
    sel = jnp.where(gt, 1.0, jnp.where(tie, jnp.where(tie_rank <= cap - n_gt, 1.0, 0.0), 0.0))
    pos = _prefix_sum_lanes(sel, seq) - 1.0
    chosen = sel > 0.5
    pos_ref[0] = jnp.where(chosen, pos, -1.0).astype(jnp.int32)
    gate_ref[0] = jnp.where(chosen, a, 0.0)


def _select(aff_t, cap):
    b, ne, s = aff_t.shape
    spec = pl.BlockSpec((1, ne, s), lambda bi: (bi, 0, 0))
    return pl.pallas_call(
        functools.partial(_select_kernel, cap=cap, seq=s),
        grid=(b,),
        in_specs=[spec],
        out_specs=[spec, spec],
        out_shape=[jax.ShapeDtypeStruct((b, ne, s), jnp.int32), jax.ShapeDtypeStruct((b, ne, s), F32)],
        compiler_params=_params("arbitrary"),
        name="expert_select",
    )(aff_t)


def _gather_kernel(pos_ref, h_ref, xe_ref, *, cap):
    pos = pos_ref[0]
    slot = lax.broadcasted_iota(jnp.int32, (cap, pos.shape[-1]), 0)
    onehot = jnp.where(slot == pos, 1.0, 0.0).astype(BF16)
    xe_ref[0, 0] = jnp.dot(onehot, h_ref[0], preferred_element_type=F32).astype(BF16)


def _gather(pos_rows, h2, cap):
    b, s, d = h2.shape
    ne = N_EXPERTS
    return pl.pallas_call(
        functools.partial(_gather_kernel, cap=cap),
        grid=(b, ne),
        in_specs=[
            pl.BlockSpec((1, 1, s), lambda bi, e: (bi * ne + e, 0, 0)),
            pl.BlockSpec((1, s, d), lambda bi, e: (bi, 0, 0)),
        ],
        out_specs=pl.BlockSpec((1, 1, cap, d), lambda bi, e: (bi, e, 0, 0)),
        out_shape=jax.ShapeDtypeStruct((b, ne, cap, d), BF16),
        compiler_params=_params("arbitrary", "arbitrary"),
        name="expert_gather",
    )(pos_rows, h2)


def _ffn_kernel(xe_ref, wg_ref, wu_ref, wd_ref, ye_ref, acc_ref):
    f = pl.program_id(1)
    nf = pl.num_programs(1)
    nb, _, cap, d = xe_ref.shape
    x = xe_ref[...].reshape(nb * cap, d)
    g = jnp.dot(x, wg_ref[0].astype(BF16), preferred_element_type=F32)
    u = jnp.dot(x, wu_ref[0].astype(BF16), preferred_element_type=F32)
    hid = (g * jax.nn.sigmoid(g)) * u
    part = jnp.dot(hid.astype(BF16), wd_ref[0].astype(BF16), preferred_element_type=F32)

    @pl.when(f == 0)
    def _():
        acc_ref[...] = part

    @pl.when(f > 0)
    def _():
        acc_ref[...] = acc_ref[...] + part

    @pl.when(f == nf - 1)
    def _():
        ye_ref[...] = acc_ref[...].reshape(nb, 1, cap, d).astype(BF16)


def _ffn(xe, w_gate, w_up, w_down):
    b, ne, cap, d = xe.shape
    ff = w_gate.shape[-1]
    tf = FFN_TF
    return pl.pallas_call(
        _ffn_kernel,
        grid=(ne, ff // tf),
        in_specs=[
            pl.BlockSpec((b, 1, cap, d), lambda e, f: (0, e, 0, 0)),
            pl.BlockSpec((1, d, tf), lambda e, f: (e, 0, f)),
            pl.BlockSpec((1, d, tf), lambda e, f: (e, 0, f)),
            pl.BlockSpec((1, tf, d), lambda e, f: (e, f, 0)),
        ],
        out_specs=pl.BlockSpec((b, 1, cap, d), lambda e, f: (0, e, 0, 0)),
        out_shape=jax.ShapeDtypeStruct((b, ne, cap, d), BF16),
        scratch_shapes=[pltpu.VMEM((b * cap, d), F32)],
        compiler_params=_params("arbitrary", "arbitrary"),
        name="expert_ffn",
    )(xe, w_gate, w_up, w_down)


def _combine_kernel(pos_ref, gate_ref, ye_ref, x1_ref, g2_ref, o_ref, *, cap):
    e = pl.program_id(2)
    ne = pl.num_programs(2)
    lane = lax.broadcasted_iota(jnp.int32, pos_ref.shape[1:], 1)
    mine = lane == e
    pos = jnp.sum(jnp.where(mine, pos_ref[0].astype(F32), 0.0), axis=-1, keepdims=True)
    gate = jnp.sum(jnp.where(mine, gate_ref[0], 0.0), axis=-1, keepdims=True)
    slot = lax.broadcasted_iota(jnp.int32, (pos.shape[0], cap), 1).astype(F32)
    scat = jnp.where(slot == pos, gate, 0.0).astype(BF16)
    part = jnp.dot(scat, ye_ref[0, 0], preferred_element_type=F32)

    @pl.when(e == 0)
    def _():
        o_ref[0] = part

    @pl.when(e > 0)
    def _():
        o_ref[0] = o_ref[0] + part

    @pl.when(e == ne - 1)
    def _():
        o_ref[0] = x1_ref[0] + g2_ref[0] * o_ref[0]


def _combine(pos_cols, gate_cols, ye, x1, mod6):
    b, s, d = x1.shape
    ne, cap = ye.shape[1], ye.shape[2]
    tq = COMBINE_TQ
    return pl.pallas_call(
        functools.partial(_combine_kernel, cap=cap),
        grid=(b, s // tq, ne),
        in_specs=[
            pl.BlockSpec((1, tq, ne), lambda bi, i, e: (bi, i, 0)),
            pl.BlockSpec((1, tq, ne), lambda bi, i, e: (bi, i, 0)),
            pl.BlockSpec((1, 1, cap, d), lambda bi, i, e: (bi, e, 0, 0)),
            pl.BlockSpec((1, tq, d), lambda bi, i, e: (bi, i, 0)),
            pl.BlockSpec((1, 1, d), lambda bi, i, e: (bi * 6 + 5, 0, 0)),
        ],
        out_specs=pl.BlockSpec((1, tq, d), lambda bi, i, e: (bi, i, 0)),
        out_shape=jax.ShapeDtypeStruct((b, s, d), F32),
        compiler_params=_params("arbitrary", "arbitrary", "arbitrary"),
        name="expert_combine",
    )(pos_cols, gate_cols, ye, x1, mod6)


def _rope_tables(seq):
    inv = 1.0 / (ROPE_THETA ** (jnp.arange(0, ATT_HEAD_DIM, 2, dtype=F32) / ATT_HEAD_DIM))
    ang = jnp.arange(seq, dtype=F32)[:, None] * inv[None, :]
    ang = jnp.concatenate([ang, ang], axis=-1)
    sign = jnp.where(jnp.arange(ATT_HEAD_DIM) < ATT_HEAD_DIM // 2, -1.0, 1.0).astype(F32)
    return jnp.cos(ang), jnp.sin(ang) * sign


def _permute_in_cols(w):
    o_ak = ATT_WIDTH
    o_mq = o_ak + 2 * ATT_KV_WIDTH
    o_mv = o_mq + 2 * M_QK_WIDTH
    o_g = o_mv + 2 * M_WIDTH
    main = jnp.concatenate([w[:, :o_ak], w[:, o_mv:o_g], w[:, o_ak:o_mv]], axis=1)
    return main.astype(BF16), w[:, o_g:].astype(BF16)


def kernel(x, c, w_ada, b_ada, w_in, b_gates, q_gain, k_gain, sink, m_gain, w_out,
           w_router, w_gate, w_up, w_down):
    b, s, d = x.shape
    depth = w_ada.shape[0]
    cap = CAPACITY_FACTOR * s // N_EXPERTS
    cos, sin_signed = _rope_tables(s)
    c_pad = jnp.pad(c, ((0, 8 - b), (0, 0)))
    mod = _ada(c_pad, w_ada, b_ada)
    for l in range(depth):
        mod6 = mod[l, :b].reshape(b * 6, 1, d)
        w_main, w_g = _permute_in_cols(w_in[l])
        proj, gates = _inproj(x, mod6, w_main, w_g, cos, sin_signed,
                              q_gain[l].reshape(1, -1), k_gain[l].reshape(1, -1))
        att = _attention(proj, sink[l])
        m_out = _mlstm(proj, gates, b_gates[l].reshape(-1), m_gain[l])
        x1, h2, aff = _outproj(att, m_out, w_out[l].astype(BF16), x, mod6, w_router[l])
        pos_r, gate_r = _select(jnp.transpose(aff, (0, 2, 1)), cap)
        xe = _gather(pos_r.reshape(b * N_EXPERTS, 1, s), h2, cap)
        ye = _ffn(xe, w_gate[l], w_up[l], w_down[l])
        x = _combine(jnp.transpose(pos_r, (0, 2, 1)), jnp.transpose(gate_r, (0, 2, 1)), ye, x1, mod6)
    return x
```

```python
import functools

import jax
import jax.numpy as jnp
from jax import lax
from jax.experimental import pallas as pl
from jax.experimental.pallas import tpu as pltpu

F32 = jnp.float32
BF16 = jnp.bfloat16
HIGHEST = lax.Precision.HIGHEST

D_MODEL = 2048
ATT_HEAD_DIM = 128
ATT_HEADS = 8
ATT_KV_HEADS = 2
ATT_GROUP = ATT_HEADS // ATT_KV_HEADS
ATT_WIDTH = ATT_HEADS * ATT_HEAD_DIM
ATT_KV_WIDTH = ATT_KV_HEADS * ATT_HEAD_DIM
WINDOW = 128
ATT_BLOCK = 128
ROPE_THETA = 10000.0
M_HEADS = 4
M_V_DIM = 256
M_QK_DIM = 128
M_WIDTH = M_HEADS * M_V_DIM
M_QK_WIDTH = M_HEADS * M_QK_DIM
M_CHUNK = 64
N_GATES = 4 * M_HEADS
N_EXPERTS = 16
EXPERT_FF = D_MODEL // 2
CAPACITY_FACTOR = 2
EPS = 1e-6
NEG = -1e30

COL_AQ = 0
COL_MV = ATT_WIDTH
COL_MO = COL_MV + M_WIDTH
COL_AK = COL_MO + M_WIDTH
COL_AV = COL_AK + ATT_KV_WIDTH
COL_MQ = COL_AV + ATT_KV_WIDTH
COL_MK = COL_MQ + M_QK_WIDTH
PROJ_WIDTH = COL_MK + M_QK_WIDTH

VMEM_LIMIT_BYTES = 56 * 1024 * 1024

ADA_TN = 1024
INPROJ_TM = 1024
INPROJ_TN = 512
OUTPROJ_TM = 512
OUTPROJ_TN = 512
FFN_TF = 256
COMBINE_TQ = 1024
SELECT_BISECTIONS = 64
SELECT_MIN_UPPER = 1e-30


def _params(*sem):
    return pltpu.CompilerParams(dimension_semantics=sem, vmem_limit_bytes=VMEM_LIMIT_BYTES)


def _ada_kernel(c_ref, w_ref, b_ref, o_ref):
    c = c_ref[...]
    c_act = c * jax.nn.sigmoid(c)
    o_ref[0] = jnp.dot(c_act, w_ref[0], precision=HIGHEST, preferred_element_type=F32) + b_ref[0]


def _ada(c_pad, w_ada, b_ada):
    depth, d, n = w_ada.shape
    rows = c_pad.shape[0]
    return pl.pallas_call(
        _ada_kernel,
        grid=(depth, n // ADA_TN),
        in_specs=[
            pl.BlockSpec((rows, d), lambda l, j: (0, 0)),
            pl.BlockSpec((1, d, ADA_TN), lambda l, j: (l, 0, j)),
            pl.BlockSpec((1, 1, ADA_TN), lambda l, j: (l, 0, j)),
        ],
        out_specs=pl.BlockSpec((1, rows, ADA_TN), lambda l, j: (l, 0, j)),
        out_shape=jax.ShapeDtypeStruct((depth, rows, n), F32),
        compiler_params=_params("arbitrary", "arbitrary"),
        name="ada_mod",
    )(c_pad, w_ada, b_ada.reshape(depth, 1, n))


def _inproj_kernel(x_ref, sc_ref, sh_ref, w_ref, wg_ref, cos_ref, sin_ref, qg_ref, kg_ref,
                   o_ref, g_ref, h_scr):
    j = pl.program_id(2)

    @pl.when(j == 0)
    def _():
        x = x_ref[0]
        r = lax.rsqrt(jnp.mean(x * x, axis=-1, keepdims=True) + EPS)
        h = (x * r) * (1.0 + sc_ref[0]) + sh_ref[0]
        hb = h.astype(BF16)
        h_scr[...] = hb
        g_ref[0] = jnp.dot(hb, wg_ref[...], preferred_element_type=F32)

    y = jnp.dot(h_scr[...], w_ref[...], preferred_element_type=F32)

    def norm_rope(t, gain):
        tn = t * lax.rsqrt(jnp.mean(t * t, axis=-1, keepdims=True) + EPS) * gain
        return tn * cos_ref[...] + pltpu.roll(tn, ATT_HEAD_DIM // 2, 1) * sin_ref[...]

    heads_per_tile = INPROJ_TN // ATT_HEAD_DIM
    q_tiles = ATT_WIDTH // INPROJ_TN
    kv_tile = COL_AK // INPROJ_TN
    scale = ATT_HEAD_DIM ** -0.5

    @pl.when(j < q_tiles)
    def _():
        for u in range(heads_per_tile):
            sl = slice(u * ATT_HEAD_DIM, (u + 1) * ATT_HEAD_DIM)
            o_ref[0, :, sl] = (norm_rope(y[:, sl], qg_ref[...]) * scale).astype(BF16)

    @pl.when(j == kv_tile)
    def _():
        for u in range(ATT_KV_HEADS):
            sl = slice(u * ATT_HEAD_DIM, (u + 1) * ATT_HEAD_DIM)
            o_ref[0, :, sl] = norm_rope(y[:, sl], kg_ref[...]).astype(BF16)
        o_ref[0, :, ATT_KV_WIDTH:] = y[:, ATT_KV_WIDTH:].astype(BF16)

    @pl.when(jnp.logical_and(j >= q_tiles, j != kv_tile))
    def _():
        o_ref[0] = y.astype(BF16)


def _inproj(x, mod6, w_main, w_gate, cos, sin_signed, q_gain, k_gain):
    b, s, d = x.shape
    tm, tn = INPROJ_TM, INPROJ_TN
    assert COL_AK % tn == 0 and ATT_WIDTH % tn == 0 and 2 * ATT_KV_WIDTH == tn
    return pl.pallas_call(
        _inproj_kernel,
        grid=(b, s // tm, PROJ_WIDTH // tn),
        in_specs=[
            pl.BlockSpec((1, tm, d), lambda bi, i, j: (bi, i, 0)),
            pl.BlockSpec((1, 1, d), lambda bi, i, j: (bi * 6 + 1, 0, 0)),
            pl.BlockSpec((1, 1, d), lambda bi, i, j: (bi * 6 + 0, 0, 0)),
            pl.BlockSpec((d, tn), lambda bi, i, j: (0, j)),
            pl.BlockSpec((d, N_GATES), lambda bi, i, j: (0, 0)),
            pl.BlockSpec((tm, ATT_HEAD_DIM), lambda bi, i, j: (i, 0)),
            pl.BlockSpec((tm, ATT_HEAD_DIM), lambda bi, i, j: (i, 0)),
            pl.BlockSpec((1, ATT_HEAD_DIM), lambda bi, i, j: (0, 0)),
            pl.BlockSpec((1, ATT_HEAD_DIM), lambda bi, i, j: (0, 0)),
        ],
        out_specs=[
            pl.BlockSpec((1, tm, tn), lambda bi, i, j: (bi, i, j)),
            pl.BlockSpec((1, tm, N_GATES), lambda bi, i, j: (bi, i, 0)),
        ],
        out_shape=[
            jax.ShapeDtypeStruct((b, s, PROJ_WIDTH), BF16),
            jax.ShapeDtypeStruct((b, s, N_GATES), F32),
        ],
        scratch_shapes=[pltpu.VMEM((tm, d), BF16)],
        compiler_params=_params("arbitrary", "arbitrary", "arbitrary"),
        name="in_proj",
    )(x, mod6, mod6, w_main, w_gate, cos, sin_signed, q_gain, k_gain)


def _attn_kernel(sink_ref, q_ref, kl_ref, kc_ref, kr_ref, vl_ref, vc_ref, vr_ref, o_ref, *, n_blocks):
    n = pl.program_id(1)
    L = ATT_BLOCK
    rows = ATT_GROUP * L
    i = lax.broadcasted_iota(jnp.int32, (rows, 3 * L), 0) & (L - 1)
    jj = lax.broadcasted_iota(jnp.int32, (rows, 3 * L), 1)
    dist = jj - L - i
    valid = (jnp.abs(dist) <= WINDOW) & ((jj >= L) | (n > 0)) & ((jj < 2 * L) | (n < n_blocks - 1))
    rgrp = lax.broadcasted_iota(jnp.int32, (rows, 1), 0) // L
    for kv in range(ATT_KV_HEADS):
        hs = slice(kv * ATT_HEAD_DIM, (kv + 1) * ATT_HEAD_DIM)
        q = jnp.concatenate(
            [q_ref[0, :, (kv * ATT_GROUP + g) * ATT_HEAD_DIM:(kv * ATT_GROUP + g + 1) * ATT_HEAD_DIM]
             for g in range(ATT_GROUP)], axis=0)
        kb = jnp.concatenate([kl_ref[0, :, hs], kc_ref[0, :, hs], kr_ref[0, :, hs]], axis=0)
        vb = jnp.concatenate([vl_ref[0, :, hs], vc_ref[0, :, hs], vr_ref[0, :, hs]], axis=0)
        s = lax.dot_general(q, kb, (((1,), (1,)), ((), ())), preferred_element_type=F32)
        s = jnp.where(valid, s, NEG)
        sink = jnp.zeros((rows, 1), F32)
        for g in range(ATT_GROUP):
            sink = jnp.where(rgrp == g, sink_ref[kv * ATT_GROUP + g], sink)
        m = jnp.maximum(jnp.max(s, axis=-1, keepdims=True), sink)
        p = jnp.exp(s - m)
        denom = jnp.sum(p, axis=-1, keepdims=True) + jnp.exp(sink - m)
        o = jnp.dot(p.astype(BF16), vb, preferred_element_type=F32) / denom
        for g in range(ATT_GROUP):
            h = kv * ATT_GROUP + g
            o_ref[0, :, h * ATT_HEAD_DIM:(h + 1) * ATT_HEAD_DIM] = o[g * L:(g + 1) * L].astype(BF16)


def _attention(proj, sink):
    b, s, _ = proj.shape
    L = ATT_BLOCK
    nb = s // L
    kblk = COL_AK // ATT_KV_WIDTH
    vblk = COL_AV // ATT_KV_WIDTH
    left = lambda n: jnp.maximum(n - 1, 0)
    right = lambda n: jnp.minimum(n + 1, nb - 1)
    kv_spec = lambda col, f: pl.BlockSpec((1, L, ATT_KV_WIDTH), lambda bi, n: (bi, f(n), col))
    same = lambda n: n
    return pl.pallas_call(
        functools.partial(_attn_kernel, n_blocks=nb),
        grid=(b, nb),
        in_specs=[
            pl.BlockSpec(memory_space=pltpu.SMEM),
            pl.BlockSpec((1, L, ATT_WIDTH), lambda bi, n: (bi, n, COL_AQ // ATT_WIDTH)),
            kv_spec(kblk, left), kv_spec(kblk, same), kv_spec(kblk, right),
            kv_spec(vblk, left), kv_spec(vblk, same), kv_spec(vblk, right),
        ],
        out_specs=pl.BlockSpec((1, L, ATT_WIDTH), lambda bi, n: (bi, n, 0)),
        out_shape=jax.ShapeDtypeStruct((b, s, ATT_WIDTH), BF16),
        compiler_params=_params("arbitrary", "arbitrary"),
        name="window_attn",
    )(sink, proj, proj, proj, proj, proj, proj, proj)


def _mlstm_kernel(q_ref, k_ref, v_ref, mo_ref, gcol_ref, grow_ref, bcol_ref, brow_ref, gain_ref, o_ref,
                  bc_scr, wc_scr, bt_scr, ar_scr, c_scr, n_scr, h_scr, *, seq):
    L = M_CHUNK
    nc = seq // L
    ns = 2 * M_HEADS
    scale = M_QK_DIM ** -0.5

    g = gcol_ref[0] + bcol_ref[...]
    ig = g[:, :ns]
    logf = jax.nn.log_sigmoid(g[:, ns:])
    t = lax.broadcasted_iota(jnp.int32, (seq, 1), 0) & (L - 1)
    pre = logf
    suf = logf
    sh = 1
    while sh < L:
        pre = pre + jnp.where(t >= sh, pltpu.roll(pre, sh, 0), 0.0)
        suf = suf + jnp.where(t + sh < L, pltpu.roll(suf, seq - sh, 0), 0.0)
        sh *= 2
    btot = pre + suf - logf
    is_fwd = lax.broadcasted_iota(jnp.int32, (1, ns), 1) < M_HEADS
    bcol = jnp.where(is_fwd, pre, suf)
    bc_scr[...] = bcol
    bt_scr[...] = btot
    wc_scr[...] = btot - bcol + ig

    gr = grow_ref[0] + brow_ref[...]
    ig_r = gr[:ns].reshape(ns * nc, L)
    logf_r = jax.nn.log_sigmoid(gr[ns:]).reshape(ns * nc, L)
    u = lax.broadcasted_iota(jnp.int32, (L, L), 0)
    w = lax.broadcasted_iota(jnp.int32, (L, L), 1)
    pre_r = jnp.dot(logf_r, (u <= w).astype(F32), precision=HIGHEST, preferred_element_type=F32)
    suf_r = jnp.dot(logf_r, (u >= w).astype(F32), precision=HIGHEST, preferred_element_type=F32)
    fwd_rows = lax.broadcasted_iota(jnp.int32, (ns * nc, 1), 0) < M_HEADS * nc
    ar_scr[...] = (ig_r - jnp.where(fwd_rows, pre_r, suf_r)).reshape(ns, nc, L)

    c_scr[...] = jnp.zeros_like(c_scr)
    n_scr[...] = jnp.zeros_like(n_scr)
    h_scr[...] = jnp.zeros_like(h_scr)

    row = lax.broadcasted_iota(jnp.int32, (L, L), 0)
    col = lax.broadcasted_iota(jnp.int32, (L, L), 1)
    masks = (col <= row, col >= row)

    def chunk_step(k, c, m_st):
        d, h = divmod(k, M_HEADS)
        r0 = pl.multiple_of(c * L, L)
        q = q_ref[0, pl.ds(r0, L), h * M_QK_DIM:(h + 1) * M_QK_DIM]
        kk = k_ref[0, pl.ds(r0, L), h * M_QK_DIM:(h + 1) * M_QK_DIM]
        v = v_ref[0, pl.ds(r0, L), h * M_V_DIM:(h + 1) * M_V_DIM]
        bc = bc_scr[pl.ds(r0, L), k:k + 1]
        wc = wc_scr[pl.ds(r0, L), k:k + 1]
        bt = bt_scr[pl.ds(r0, 1), k:k + 1]
        ar = ar_scr[k, pl.ds(c, 1), :]
        dm = jnp.where(masks[d], bc + ar, NEG)
        g_inter = bc + m_st
        m_t = jnp.maximum(jnp.max(dm, axis=-1, keepdims=True), g_inter)
        e_inter = jnp.exp(g_inter - m_t) * scale
        s_qk = lax.dot_general(q, kk, (((1,), (1,)), ((), ())), preferred_element_type=F32)
        p = s_qk * scale * jnp.exp(dm - m_t)
        q_c = jnp.dot(q, c_scr[k].astype(BF16), preferred_element_type=F32)
        num = jnp.dot(p.astype(BF16), v, preferred_element_type=F32) + e_inter * q_c
        q_n = jnp.sum(q.astype(F32) * n_scr[k], axis=-1, keepdims=True)
        den = jnp.sum(p, axis=-1, keepdims=True) + e_inter * q_n
        hh = num / jnp.maximum(jnp.abs(den), jnp.exp(-m_t))
        hsl = (pl.ds(r0, L), slice(h * M_V_DIM, (h + 1) * M_V_DIM))
        h_scr[hsl] = h_scr[hsl] + hh
        m_new = jnp.maximum(bt + m_st, jnp.max(wc, axis=0, keepdims=True))
        a = jnp.exp(bt + m_st - m_new)
        ekk = jnp.exp(wc - m_new) * kk.astype(F32)
        c_scr[k] = a * c_scr[k] + lax.dot_general(
            ekk.astype(BF16), v, (((0,), (0,)), ((), ())), preferred_element_type=F32)
        n_scr[k] = a * n_scr[k] + jnp.sum(ekk, axis=0, keepdims=True)
        return m_new

    def body(it, ms):
        out = []
        for k in range(ns):
            c = it if k < M_HEADS else nc - 1 - it
            out.append(chunk_step(k, c, ms[k]))
        return tuple(out)

    lax.fori_loop(0, nc, body, tuple(jnp.zeros((1, 1), F32) for _ in range(ns)))

    rt = 256

    def fin(i, carry):
        r0 = pl.multiple_of(i * rt, rt)
        for h in range(M_HEADS):
            sl = slice(h * M_V_DIM, (h + 1) * M_V_DIM)
            x = h_scr[pl.ds(r0, rt), sl]
            y = x * lax.rsqrt(jnp.mean(x * x, axis=-1, keepdims=True) + EPS) * gain_ref[:, sl]
            o_ref[0, pl.ds(r0, rt), sl] = (y * jax.nn.sigmoid(mo_ref[0, pl.ds(r0, rt), sl].astype(F32))).astype(BF16)
        return carry

    lax.fori_loop(0, seq // rt, fin, 0)


def _mlstm(proj, gates, b_gates, m_gain):
    b, s, _ = proj.shape
    L = M_CHUNK
    nc = s // L
    ns = 2 * M_HEADS
    grow = jnp.transpose(gates, (0, 2, 1)).reshape(b, N_GATES, nc, L)
    col = lambda width, off: pl.BlockSpec((1, s, width), lambda bi: (bi, 0, off // width))
    return pl.pallas_call(
        functools.partial(_mlstm_kernel, seq=s),
        grid=(b,),
        in_specs=[
            col(M_QK_WIDTH, COL_MQ), col(M_QK_WIDTH, COL_MK), col(M_WIDTH, COL_MV), col(M_WIDTH, COL_MO),
            pl.BlockSpec((1, s, N_GATES), lambda bi: (bi, 0, 0)),
            pl.BlockSpec((1, N_GATES, nc, L), lambda bi: (bi, 0, 0, 0)),
            pl.BlockSpec((1, N_GATES), lambda bi: (0, 0)),
            pl.BlockSpec((N_GATES, 1, 1), lambda bi: (0, 0, 0)),
            pl.BlockSpec((1, M_WIDTH), lambda bi: (0, 0)),
        ],
        out_specs=pl.BlockSpec((1, s, M_WIDTH), lambda bi: (bi, 0, 0)),
        out_shape=jax.ShapeDtypeStruct((b, s, M_WIDTH), BF16),
        scratch_shapes=[
            pltpu.VMEM((s, ns), F32), pltpu.VMEM((s, ns), F32), pltpu.VMEM((s, ns), F32),
            pltpu.VMEM((ns, nc, L), F32),
            pltpu.VMEM((ns, M_QK_DIM, M_V_DIM), F32),
            pltpu.VMEM((ns, 1, M_QK_DIM), F32),
            pltpu.VMEM((s, M_WIDTH), F32),
        ],
        compiler_params=_params("arbitrary"),
        name="mlstm",
    )(proj, proj, proj, proj, gates, grow, b_gates.reshape(1, N_GATES), b_gates.reshape(N_GATES, 1, 1),
      m_gain.reshape(1, M_WIDTH))


def _outproj_kernel(att_ref, mo_ref, wa_ref, wm_ref, x_ref, g1_ref, sc_ref, sh_ref, wr_ref,
                    x1_ref, h2_ref, aff_ref):
    j = pl.program_id(2)
    nj = pl.num_programs(2)
    tn = OUTPROJ_TN
    mix = (jnp.dot(att_ref[0], wa_ref[...], preferred_element_type=F32)
           + jnp.dot(mo_ref[0], wm_ref[...], preferred_element_type=F32))
    x1t = x_ref[0] + g1_ref[0] * mix
    for jj in range(D_MODEL // tn):
        @pl.when(j == jj)
        def _(jj=jj):
            x1_ref[0, :, jj * tn:(jj + 1) * tn] = x1t

    @pl.when(j == nj - 1)
    def _():
        xf = x1_ref[0]
        r = lax.rsqrt(jnp.mean(xf * xf, axis=-1, keepdims=True) + EPS)
        h2 = (xf * r) * (1.0 + sc_ref[0]) + sh_ref[0]
        h2_ref[0] = h2.astype(BF16)
        logits = jnp.dot(h2, wr_ref[...], precision=HIGHEST, preferred_element_type=F32)
        e = jnp.exp(logits - jnp.max(logits, axis=-1, keepdims=True))
        aff_ref[0] = e / jnp.sum(e, axis=-1, keepdims=True)


def _outproj(att, m_out, w_out, x, mod6, w_router):
    b, s, d = x.shape
    tm, tn = OUTPROJ_TM, OUTPROJ_TN
    half = ATT_WIDTH
    assert w_out.shape[0] == 2 * half and M_WIDTH == half
    return pl.pallas_call(
        _outproj_kernel,
        grid=(b, s // tm, d // tn),
        in_specs=[
            pl.BlockSpec((1, tm, ATT_WIDTH), lambda bi, i, j: (bi, i, 0)),
            pl.BlockSpec((1, tm, M_WIDTH), lambda bi, i, j: (bi, i, 0)),
            pl.BlockSpec((half, tn), lambda bi, i, j: (0, j)),
            pl.BlockSpec((half, tn), lambda bi, i, j: (1, j)),
            pl.BlockSpec((1, tm, tn), lambda bi, i, j: (bi, i, j)),
            pl.BlockSpec((1, 1, tn), lambda bi, i, j: (bi * 6 + 2, 0, j)),
            pl.BlockSpec((1, 1, d), lambda bi, i, j: (bi * 6 + 4, 0, 0)),
            pl.BlockSpec((1, 1, d), lambda bi, i, j: (bi * 6 + 3, 0, 0)),
            pl.BlockSpec((d, N_EXPERTS), lambda bi, i, j: (0, 0)),
        ],
        out_specs=[
            pl.BlockSpec((1, tm, d), lambda bi, i, j: (bi, i, 0)),
            pl.BlockSpec((1, tm, d), lambda bi, i, j: (bi, i, 0)),
            pl.BlockSpec((1, tm, N_EXPERTS), lambda bi, i, j: (bi, i, 0)),
        ],
        out_shape=[
            jax.ShapeDtypeStruct((b, s, d), F32),
            jax.ShapeDtypeStruct((b, s, d), BF16),
            jax.ShapeDtypeStruct((b, s, N_EXPERTS), F32),
        ],
        compiler_params=_params("arbitrary", "arbitrary", "arbitrary"),
        name="out_proj",
    )(att, m_out, w_out, w_out, x, mod6, mod6, mod6, w_router)


def _prefix_sum_lanes(x, n):
    lane = lax.broadcasted_iota(jnp.int32, x.shape, 1)
    sh = 1
    while sh < n:
        x = x + jnp.where(lane >= sh, pltpu.roll(x, sh, 1), 0.0)
        sh *= 2
    return x


def _select_kernel(aff_ref, pos_ref, gate_ref, *, cap, seq):
    a = aff_ref[...]
    ne = a.shape[0]

    def body(_, lohi):
        lo, hi = lohi
        mid = jnp.where(lo > 0.0, 0.5 * (lo + hi), hi * (1.0 / 256.0))
        cnt = jnp.sum(jnp.where(a >= mid, 1.0, 0.0), axis=-1, keepdims=True)
        ok = cnt >= cap
        return jnp.where(ok, mid, lo), jnp.where(ok, hi, mid)

    lo0 = jnp.zeros((ne, 1), F32)
    hi0 = jnp.maximum(2.0 * jnp.max(a, axis=-1, keepdims=True), SELECT_MIN_UPPER)
    lo, hi = lax.fori_loop(0, SELECT_BISECTIONS, body, (lo0, hi0))
    above = a >= hi
    band = jnp.logical_and(a >= lo, jnp.logical_not(above))
    n_above = jnp.sum(jnp.where(above, 1.0, 0.0), axis=-1, keepdims=True)
    band_rank = _prefix_sum_lanes(jnp.where(band, 1.0, 0.0), seq)
    sel = jnp.where(above, 1.0, jnp.where(band, jnp.where(band_rank <= cap - n_above, 1.0, 0.0), 0.0))
    pos = _prefix_sum_lanes(sel, seq) - 1.0
    chosen = sel > 0.5
    pos_ref[...] = jnp.where(chosen, pos, -1.0)
    gate_ref[...] = jnp.where(chosen, a, 0.0)


def _select(aff_rows, cap):
    n, s = aff_rows.shape
    spec = pl.BlockSpec((n, s), lambda i: (0, 0))
    return pl.pallas_call(
        functools.partial(_select_kernel, cap=cap, seq=s),
        grid=(1,),
        in_specs=[spec],
        out_specs=[spec, spec],
        out_shape=[jax.ShapeDtypeStruct((n, s), F32), jax.ShapeDtypeStruct((n, s), F32)],
        compiler_params=_params("arbitrary"),
        name="expert_select",
    )(aff_rows)


def _gather_kernel(pos_ref, h_ref, xe_ref, *, cap):
    pos = pos_ref[0]
    slot = lax.broadcasted_iota(jnp.int32, (cap, pos.shape[-1]), 0).astype(F32)
    onehot = jnp.where(slot == pos, 1.0, 0.0).astype(BF16)
    xe_ref[0, 0] = jnp.dot(onehot, h_ref[0], preferred_element_type=F32).astype(BF16)


def _gather(pos_rows, h2, cap):
    b, s, d = h2.shape
    ne = N_EXPERTS
    return pl.pallas_call(
        functools.partial(_gather_kernel, cap=cap),
        grid=(b, ne),
        in_specs=[
            pl.BlockSpec((1, 1, s), lambda bi, e: (bi * ne + e, 0, 0)),
            pl.BlockSpec((1, s, d), lambda bi, e: (bi, 0, 0)),
        ],
        out_specs=pl.BlockSpec((1, 1, cap, d), lambda bi, e: (bi, e, 0, 0)),
        out_shape=jax.ShapeDtypeStruct((b, ne, cap, d), BF16),
        compiler_params=_params("arbitrary", "arbitrary"),
        name="expert_gather",
    )(pos_rows, h2)


def _ffn_kernel(xe_ref, wg_ref, wu_ref, wd_ref, ye_ref, acc_ref):
    f = pl.program_id(1)
    nf = pl.num_programs(1)
    nb, _, cap, d = xe_ref.shape
    x = xe_ref[...].reshape(nb * cap, d)
    g = jnp.dot(x, wg_ref[0, 0].astype(BF16), preferred_element_type=F32)
    u = jnp.dot(x, wu_ref[0, 0].astype(BF16), preferred_element_type=F32)
    hid = (g * jax.nn.sigmoid(g)) * u
    part = jnp.dot(hid.astype(BF16), wd_ref[0, 0].astype(BF16), preferred_element_type=F32)

    @pl.when(f == 0)
    def _():
        acc_ref[...] = part

    @pl.when(f > 0)
    def _():
        acc_ref[...] = acc_ref[...] + part

    @pl.when(f == nf - 1)
    def _():
        ye_ref[...] = acc_ref[...].reshape(nb, 1, cap, d).astype(BF16)


def _ffn(xe, w_gate, w_up, w_down, layer):
    b, ne, cap, d = xe.shape
    ff = w_gate.shape[-1]
    tf = FFN_TF
    return pl.pallas_call(
        _ffn_kernel,
        grid=(ne, ff // tf),
        in_specs=[
            pl.BlockSpec((b, 1, cap, d), lambda e, f: (0, e, 0, 0)),
            pl.BlockSpec((1, 1, d, tf), lambda e, f: (layer, e, 0, f)),
            pl.BlockSpec((1, 1, d, tf), lambda e, f: (layer, e, 0, f)),
            pl.BlockSpec((1, 1, tf, d), lambda e, f: (layer, e, f, 0)),
        ],
        out_specs=pl.BlockSpec((b, 1, cap, d), lambda e, f: (0, e, 0, 0)),
        out_shape=jax.ShapeDtypeStruct((b, ne, cap, d), BF16),
        scratch_shapes=[pltpu.VMEM((b * cap, d), F32)],
        compiler_params=_params("arbitrary", "arbitrary"),
        name="expert_ffn",
    )(xe, w_gate, w_up, w_down)


def _combine_kernel(pos_ref, gate_ref, ye_ref, x1_ref, g2_ref, o_ref, *, cap):
    e = pl.program_id(2)
    ne = pl.num_programs(2)
    lane = lax.broadcasted_iota(jnp.int32, pos_ref.shape[1:], 1)
    mine = lane == e
    pos = jnp.sum(jnp.where(mine, pos_ref[0], 0.0), axis=-1, keepdims=True)
    gate = jnp.sum(jnp.where(mine, gate_ref[0], 0.0), axis=-1, keepdims=True)
    slot = lax.broadcasted_iota(jnp.int32, (pos.shape[0], cap), 1).astype(F32)
    scat = jnp.where(slot == pos, gate, 0.0).astype(BF16)
    part = jnp.dot(scat, ye_ref[0, 0], preferred_element_type=F32)

    @pl.when(e == 0)
    def _():
        o_ref[0] = part

    @pl.when(e > 0)
    def _():
        o_ref[0] = o_ref[0] + part

    @pl.when(e == ne - 1)
    def _():
        o_ref[0] = x1_ref[0] + g2_ref[0] * o_ref[0]


def _combine(pos_cols, gate_cols, ye, x1, mod6):
    b, s, d = x1.shape
    ne, cap = ye.shape[1], ye.shape[2]
    tq = COMBINE_TQ
    return pl.pallas_call(
        functools.partial(_combine_kernel, cap=cap),
        grid=(b, s // tq, ne),
        in_specs=[
            pl.BlockSpec((1, tq, ne), lambda bi, i, e: (bi, i, 0)),
            pl.BlockSpec((1, tq, ne), lambda bi, i, e: (bi, i, 0)),
            pl.BlockSpec((1, 1, cap, d), lambda bi, i, e: (bi, e, 0, 0)),
            pl.BlockSpec((1, tq, d), lambda bi, i, e: (bi, i, 0)),
            pl.BlockSpec((1, 1, d), lambda bi, i, e: (bi * 6 + 5, 0, 0)),
        ],
        out_specs=pl.BlockSpec((1, tq, d), lambda bi, i, e: (bi, i, 0)),
        out_shape=jax.ShapeDtypeStruct((b, s, d), F32),
        compiler_params=_params("arbitrary", "arbitrary", "arbitrary"),
        name="expert_combine",
    )(pos_cols, gate_cols, ye, x1, mod6)


def _rope_tables(seq):
    inv = 1.0 / (ROPE_THETA ** (jnp.arange(0, ATT_HEAD_DIM, 2, dtype=F32) / ATT_HEAD_DIM))
    ang = jnp.arange(seq, dtype=F32)[:, None] * inv[None, :]
    ang = jnp.concatenate([ang, ang], axis=-1)
    sign = jnp.where(jnp.arange(ATT_HEAD_DIM) < ATT_HEAD_DIM // 2, -1.0, 1.0).astype(F32)
    return jnp.cos(ang), jnp.sin(ang) * sign


def _permute_in_cols(w):
    o_ak = ATT_WIDTH
    o_mq = o_ak + 2 * ATT_KV_WIDTH
    o_mv = o_mq + 2 * M_QK_WIDTH
    o_g = o_mv + 2 * M_WIDTH
    main = jnp.concatenate([w[:, :o_ak], w[:, o_mv:o_g], w[:, o_ak:o_mv]], axis=1)
    return main.astype(BF16), w[:, o_g:].astype(BF16)


def kernel(x, c, w_ada, b_ada, w_in, b_gates, q_gain, k_gain, sink, m_gain, w_out,
           w_router, w_gate, w_up, w_down):
    b, s, d = x.shape
    depth = w_ada.shape[0]
    cap = CAPACITY_FACTOR * s // N_EXPERTS
    cos, sin_signed = _rope_tables(s)
    c_pad = jnp.pad(c, ((0, 8 - b), (0, 0)))
    mod = _ada(c_pad, w_ada, b_ada)
    for l in range(depth):
        mod6 = mod[l, :b].reshape(b * 6, 1, d)
        w_main, w_g = _permute_in_cols(w_in[l])
        proj, gates = _inproj(x, mod6, w_main, w_g, cos, sin_signed,
                              q_gain[l].reshape(1, -1), k_gain[l].reshape(1, -1))
        att = _attention(proj, sink[l])
        m_out = _mlstm(proj, gates, b_gates[l].reshape(-1), m_gain[l])
        x1, h2, aff = _outproj(att, m_out, w_out[l].astype(BF16), x, mod6, w_router[l])
        pos_r, gate_r = _select(jnp.transpose(aff, (0, 2, 1)).reshape(b * N_EXPERTS, s), cap)
        xe = _gather(pos_r.reshape(b * N_EXPERTS, 1, s), h2, cap)
        ye = _ffn(xe, w_gate, w_up, w_down, l)
        to_cols = lambda t: jnp.transpose(t.reshape(b, N_EXPERTS, s), (0, 2, 1))
        x = _combine(to_cols(pos_r), to_cols(gate_r), ye, x1, mod6)
    return x
```

```python
import functools

import jax
import jax.numpy as jnp
from jax import lax
from jax.experimental import pallas as pl
from jax.experimental.pallas import tpu as pltpu

F32 = jnp.float32
BF16 = jnp.bfloat16
HIGHEST = lax.Precision.HIGHEST

D_MODEL = 2048
ATT_HEAD_DIM = 128
ATT_HEADS = 8
ATT_KV_HEADS = 2
ATT_GROUP = ATT_HEADS // ATT_KV_HEADS
ATT_WIDTH = ATT_HEADS * ATT_HEAD_DIM
ATT_KV_WIDTH = ATT_KV_HEADS * ATT_HEAD_DIM
WINDOW = 128
ATT_BLOCK = 128
ROPE_THETA = 10000.0
M_HEADS = 4
M_V_DIM = 256
M_QK_DIM = 128
M_WIDTH = M_HEADS * M_V_DIM
M_QK_WIDTH = M_HEADS * M_QK_DIM
MLSTM_BLOCK = 256
N_GATES = 4 * M_HEADS
N_EXPERTS = 16
EXPERT_FF = D_MODEL // 2
CAPACITY_FACTOR = 2
EPS = 1e-6
NEG = -1e30

COL_AQ = 0
COL_MV = ATT_WIDTH
COL_MO = COL_MV + M_WIDTH
COL_AK = COL_MO + M_WIDTH
COL_AV = COL_AK + ATT_KV_WIDTH
COL_MQ = COL_AV + ATT_KV_WIDTH
COL_MK = COL_MQ + M_QK_WIDTH
PROJ_WIDTH = COL_MK + M_QK_WIDTH

VMEM_LIMIT_BYTES = 56 * 1024 * 1024

ADA_TN = 1024
INPROJ_TM = 1024
INPROJ_TN = 512
OUTPROJ_TM = 512
OUTPROJ_TN = 512
FFN_TF = 256
COMBINE_TQ = 1024
SELECT_BISECTIONS = 64
SELECT_MIN_UPPER = 1e-30


def _params(*sem):
    return pltpu.CompilerParams(dimension_semantics=sem, vmem_limit_bytes=VMEM_LIMIT_BYTES)


def _ada_kernel(c_ref, w_ref, b_ref, o_ref):
    c = c_ref[...]
    c_act = c * jax.nn.sigmoid(c)
    o_ref[0] = jnp.dot(c_act, w_ref[0], precision=HIGHEST, preferred_element_type=F32) + b_ref[0]


def _ada(c_pad, w_ada, b_ada):
    depth, d, n = w_ada.shape
    rows = c_pad.shape[0]
    return pl.pallas_call(
        _ada_kernel,
        grid=(depth, n // ADA_TN),
        in_specs=[
            pl.BlockSpec((rows, d), lambda l, j: (0, 0)),
            pl.BlockSpec((1, d, ADA_TN), lambda l, j: (l, 0, j)),
            pl.BlockSpec((1, 1, ADA_TN), lambda l, j: (l, 0, j)),
        ],
        out_specs=pl.BlockSpec((1, rows, ADA_TN), lambda l, j: (l, 0, j)),
        out_shape=jax.ShapeDtypeStruct((depth, rows, n), F32),
        compiler_params=_params("arbitrary", "arbitrary"),
        name="ada_mod",
    )(c_pad, w_ada, b_ada.reshape(depth, 1, n))


def _inproj_kernel(x_ref, sc_ref, sh_ref, w_ref, wg_ref, cos_ref, sin_ref, qg_ref, kg_ref,
                   o_ref, g_ref, h_scr):
    j = pl.program_id(2)

    @pl.when(j == 0)
    def _():
        x = x_ref[0]
        r = lax.rsqrt(jnp.mean(x * x, axis=-1, keepdims=True) + EPS)
        h = (x * r) * (1.0 + sc_ref[0]) + sh_ref[0]
        hb = h.astype(BF16)
        h_scr[...] = hb
        g_ref[0] = jnp.dot(hb, wg_ref[...], preferred_element_type=F32)

    y = jnp.dot(h_scr[...], w_ref[...], preferred_element_type=F32)

    def norm_rope(t, gain):
        tn = t * lax.rsqrt(jnp.mean(t * t, axis=-1, keepdims=True) + EPS) * gain
        return tn * cos_ref[...] + pltpu.roll(tn, ATT_HEAD_DIM // 2, 1) * sin_ref[...]

    heads_per_tile = INPROJ_TN // ATT_HEAD_DIM
    q_tiles = ATT_WIDTH // INPROJ_TN
    kv_tile = COL_AK // INPROJ_TN
    scale = ATT_HEAD_DIM ** -0.5

    @pl.when(j < q_tiles)
    def _():
        for u in range(heads_per_tile):
            sl = slice(u * ATT_HEAD_DIM, (u + 1) * ATT_HEAD_DIM)
            o_ref[0, :, sl] = (norm_rope(y[:, sl], qg_ref[...]) * scale).astype(BF16)

    @pl.when(j == kv_tile)
    def _():
        for u in range(ATT_KV_HEADS):
            sl = slice(u * ATT_HEAD_DIM, (u + 1) * ATT_HEAD_DIM)
            o_ref[0, :, sl] = norm_rope(y[:, sl], kg_ref[...]).astype(BF16)
        o_ref[0, :, ATT_KV_WIDTH:] = y[:, ATT_KV_WIDTH:].astype(BF16)

    @pl.when(jnp.logical_and(j >= q_tiles, j != kv_tile))
    def _():
        o_ref[0] = y.astype(BF16)


def _inproj(x, mod6, w_main, w_gate, cos, sin_signed, q_gain, k_gain):
    b, s, d = x.shape
    tm, tn = INPROJ_TM, INPROJ_TN
    assert COL_AK % tn == 0 and ATT_WIDTH % tn == 0 and 2 * ATT_KV_WIDTH == tn
    return pl.pallas_call(
        _inproj_kernel,
        grid=(b, s // tm, PROJ_WIDTH // tn),
        in_specs=[
            pl.BlockSpec((1, tm, d), lambda bi, i, j: (bi, i, 0)),
            pl.BlockSpec((1, 1, d), lambda bi, i, j: (bi * 6 + 1, 0, 0)),
            pl.BlockSpec((1, 1, d), lambda bi, i, j: (bi * 6 + 0, 0, 0)),
            pl.BlockSpec((d, tn), lambda bi, i, j: (0, j)),
            pl.BlockSpec((d, N_GATES), lambda bi, i, j: (0, 0)),
            pl.BlockSpec((tm, ATT_HEAD_DIM), lambda bi, i, j: (i, 0)),
            pl.BlockSpec((tm, ATT_HEAD_DIM), lambda bi, i, j: (i, 0)),
            pl.BlockSpec((1, ATT_HEAD_DIM), lambda bi, i, j: (0, 0)),
            pl.BlockSpec((1, ATT_HEAD_DIM), lambda bi, i, j: (0, 0)),
        ],
        out_specs=[
            pl.BlockSpec((1, tm, tn), lambda bi, i, j: (bi, i, j)),
            pl.BlockSpec((1, tm, N_GATES), lambda bi, i, j: (bi, i, 0)),
        ],
        out_shape=[
            jax.ShapeDtypeStruct((b, s, PROJ_WIDTH), BF16),
            jax.ShapeDtypeStruct((b, s, N_GATES), F32),
        ],
        scratch_shapes=[pltpu.VMEM((tm, d), BF16)],
        compiler_params=_params("arbitrary", "arbitrary", "arbitrary"),
        name="in_proj",
    )(x, mod6, mod6, w_main, w_gate, cos, sin_signed, q_gain, k_gain)


def _attn_kernel(sink_ref, q_ref, kl_ref, kc_ref, kr_ref, vl_ref, vc_ref, vr_ref, o_ref, *, n_blocks):
    n = pl.program_id(1)
    L = ATT_BLOCK
    rows = ATT_GROUP * L
    i = lax.broadcasted_iota(jnp.int32, (rows, 3 * L), 0) & (L - 1)
    jj = lax.broadcasted_iota(jnp.int32, (rows, 3 * L), 1)
    dist = jj - L - i
    valid = (jnp.abs(dist) <= WINDOW) & ((jj >= L) | (n > 0)) & ((jj < 2 * L) | (n < n_blocks - 1))
    rgrp = lax.broadcasted_iota(jnp.int32, (rows, 1), 0) // L
    for kv in range(ATT_KV_HEADS):
        hs = slice(kv * ATT_HEAD_DIM, (kv + 1) * ATT_HEAD_DIM)
        q = jnp.concatenate(
            [q_ref[0, :, (kv * ATT_GROUP + g) * ATT_HEAD_DIM:(kv * ATT_GROUP + g + 1) * ATT_HEAD_DIM]
             for g in range(ATT_GROUP)], axis=0)
        kb = jnp.concatenate([kl_ref[0, :, hs], kc_ref[0, :, hs], kr_ref[0, :, hs]], axis=0)
        vb = jnp.concatenate([vl_ref[0, :, hs], vc_ref[0, :, hs], vr_ref[0, :, hs]], axis=0)
        s = lax.dot_general(q, kb, (((1,), (1,)), ((), ())), preferred_element_type=F32)
        s = jnp.where(valid, s, NEG)
        sink = jnp.zeros((rows, 1), F32)
        for g in range(ATT_GROUP):
            sink = jnp.where(rgrp == g, sink_ref[kv * ATT_GROUP + g], sink)
        m = jnp.maximum(jnp.max(s, axis=-1, keepdims=True), sink)
        p = jnp.exp(s - m)
        denom = jnp.sum(p, axis=-1, keepdims=True) + jnp.exp(sink - m)
        o = jnp.dot(p.astype(BF16), vb, preferred_element_type=F32) / denom
        for g in range(ATT_GROUP):
            h = kv * ATT_GROUP + g
            o_ref[0, :, h * ATT_HEAD_DIM:(h + 1) * ATT_HEAD_DIM] = o[g * L:(g + 1) * L].astype(BF16)


def _attention(proj, sink):
    b, s, _ = proj.shape
    L = ATT_BLOCK
    nb = s // L
    kblk = COL_AK // ATT_KV_WIDTH
    vblk = COL_AV // ATT_KV_WIDTH
    left = lambda n: jnp.maximum(n - 1, 0)
    right = lambda n: jnp.minimum(n + 1, nb - 1)
    kv_spec = lambda col, f: pl.BlockSpec((1, L, ATT_KV_WIDTH), lambda bi, n: (bi, f(n), col))
    same = lambda n: n
    return pl.pallas_call(
        functools.partial(_attn_kernel, n_blocks=nb),
        grid=(b, nb),
        in_specs=[
            pl.BlockSpec(memory_space=pltpu.SMEM),
            pl.BlockSpec((1, L, ATT_WIDTH), lambda bi, n: (bi, n, COL_AQ // ATT_WIDTH)),
            kv_spec(kblk, left), kv_spec(kblk, same), kv_spec(kblk, right),
            kv_spec(vblk, left), kv_spec(vblk, same), kv_spec(vblk, right),
        ],
        out_specs=pl.BlockSpec((1, L, ATT_WIDTH), lambda bi, n: (bi, n, 0)),
        out_shape=jax.ShapeDtypeStruct((b, s, ATT_WIDTH), BF16),
        compiler_params=_params("arbitrary", "arbitrary"),
        name="window_attn",
    )(sink, proj, proj, proj, proj, proj, proj, proj)


def _mlstm_kernel(q_ref, k_ref, v_ref, mo_ref, gcol_ref, grow_ref, bcol_ref, brow_ref, gain_ref, o_ref,
                  bc_scr, kt_scr, ar_scr, wr_scr, bt_scr, c_scr, h_scr, *, seq):
    L = MLSTM_BLOCK
    nc = seq // L
    ns = 2 * M_HEADS
    scale = M_QK_DIM ** -0.5
    row = lax.broadcasted_iota(jnp.int32, (L, L), 0)
    col = lax.broadcasted_iota(jnp.int32, (L, L), 1)
    lower = col <= row
    upper = col >= row
    tril = lower.astype(F32)
    triu = upper.astype(F32)

    fwd_lane = lax.broadcasted_iota(jnp.int32, (1, N_GATES), 1) < ns + M_HEADS
    for c in range(nc):
        rows = slice(c * L, (c + 1) * L)
        lf = jax.nn.log_sigmoid(gcol_ref[0, rows, :] + bcol_ref[...])
        pre = jnp.dot(tril, lf, precision=HIGHEST, preferred_element_type=F32)
        suf = jnp.dot(triu, lf, precision=HIGHEST, preferred_element_type=F32)
        bcol = jnp.where(fwd_lane, pre, suf)
        for k in range(ns):
            bc_scr[k, rows, :] = jnp.broadcast_to(bcol[:, ns + k:ns + k + 1], (L, 128))
        kt_scr[:, rows] = k_ref[0, rows, :].astype(F32).T.astype(BF16)

    gr = grow_ref[0] + brow_ref[...]
    ig_r = gr[:ns].reshape(ns * nc, L)
    lf_r = jax.nn.log_sigmoid(gr[ns:]).reshape(ns * nc, L)
    pre_r = jnp.dot(lf_r, triu, precision=HIGHEST, preferred_element_type=F32)
    suf_r = jnp.dot(lf_r, tril, precision=HIGHEST, preferred_element_type=F32)
    fwd_rows = lax.broadcasted_iota(jnp.int32, (ns * nc, 1), 0) < M_HEADS * nc
    a_r = ig_r - jnp.where(fwd_rows, pre_r, suf_r)
    btot = jnp.sum(lf_r, axis=-1, keepdims=True)
    ar_scr[...] = a_r.reshape(ns, nc, L)
    wr_scr[...] = (btot + a_r).reshape(ns, nc, L)
    bt_scr[...] = jnp.broadcast_to(btot, (ns * nc, 128)).reshape(ns, nc, 128)

    c_scr[...] = jnp.zeros_like(c_scr)
    h_scr[...] = jnp.zeros_like(h_scr)
    ones_cols = jnp.ones((L, 128), BF16)

    twice = lambda t: jnp.concatenate([t, t], axis=1)

    def chunk_step(k, c, m_st):
        d, h = divmod(k, M_HEADS)
        r0 = pl.multiple_of(c * L, L)
        q = q_ref[0, pl.ds(r0, L), h * M_QK_DIM:(h + 1) * M_QK_DIM]
        kk = k_ref[0, pl.ds(r0, L), h * M_QK_DIM:(h + 1) * M_QK_DIM]
        v_ext = jnp.concatenate([v_ref[0, pl.ds(r0, L), h * M_V_DIM:(h + 1) * M_V_DIM], ones_cols], axis=1)
        bc = bc_scr[k, pl.ds(r0, L), :]
        ar = ar_scr[k, pl.ds(c, 1), :]
        wr = wr_scr[k, pl.ds(c, 1), :]
        bt = bt_scr[k, pl.ds(c, 1), :]
        dm = jnp.where(lower if d == 0 else upper, twice(bc) + ar, NEG)
        g_inter = bc + m_st
        m_t = jnp.maximum(jnp.broadcast_to(jnp.max(dm, axis=-1, keepdims=True), (L, 128)), g_inter)
        e_inter = jnp.exp(g_inter - m_t) * scale
        s_qk = lax.dot_general(q, kk, (((1,), (1,)), ((), ())), preferred_element_type=F32)
        p = s_qk * scale * jnp.exp(dm - twice(m_t))
        qc = jnp.dot(q, c_scr[k].astype(BF16), preferred_element_type=F32)
        pv = jnp.dot(p.astype(BF16), v_ext, preferred_element_type=F32)
        num = pv[:, :M_V_DIM] + twice(e_inter) * qc[:, :M_V_DIM]
        den = pv[:, M_V_DIM:] + e_inter * qc[:, M_V_DIM:]
        inv = 1.0 / jnp.maximum(jnp.abs(den), jnp.exp(-m_t))
        hsl = (pl.ds(r0, L), slice(h * M_V_DIM, (h + 1) * M_V_DIM))
        h_scr[hsl] = h_scr[hsl] + num * twice(inv)
        m_new = jnp.maximum(bt + m_st, jnp.broadcast_to(jnp.max(wr, axis=-1, keepdims=True), (1, 128)))
        a = jnp.exp(bt + m_st - m_new)
        kt = kt_scr[h * M_QK_DIM:(h + 1) * M_QK_DIM, pl.ds(r0, L)]
        ek_t = (kt.astype(F32) * jnp.exp(wr - twice(m_new))).astype(BF16)
        c_scr[k] = jnp.concatenate([a, a, a], axis=1) * c_scr[k] + jnp.dot(ek_t, v_ext, preferred_element_type=F32)
        return m_new

    def body(it, ms):
        out = []
        for k in range(ns):
            c = it if k < M_HEADS else nc - 1 - it
            out.append(chunk_step(k, c, ms[k]))
        return tuple(out)

    lax.fori_loop(0, nc, body, tuple(jnp.zeros((1, 128), F32) for _ in range(ns)))

    def fin(i, carry):
        r0 = pl.multiple_of(i * L, L)
        for h in range(M_HEADS):
            sl = slice(h * M_V_DIM, (h + 1) * M_V_DIM)
            x = h_scr[pl.ds(r0, L), sl]
            y = x * lax.rsqrt(jnp.mean(x * x, axis=-1, keepdims=True) + EPS) * gain_ref[:, sl]
            o_ref[0, pl.ds(r0, L), sl] = (y * jax.nn.sigmoid(mo_ref[0, pl.ds(r0, L), sl].astype(F32))).astype(BF16)
        return carry

    lax.fori_loop(0, nc, fin, 0)


def _mlstm(proj, gates, b_gates, m_gain):
    b, s, _ = proj.shape
    L = MLSTM_BLOCK
    nc = s // L
    ns = 2 * M_HEADS
    grow = jnp.transpose(gates, (0, 2, 1)).reshape(b, N_GATES, nc, L)
    col = lambda width, off: pl.BlockSpec((1, s, width), lambda bi: (bi, 0, off // width))
    return pl.pallas_call(
        functools.partial(_mlstm_kernel, seq=s),
        grid=(b,),
        in_specs=[
            col(M_QK_WIDTH, COL_MQ), col(M_QK_WIDTH, COL_MK), col(M_WIDTH, COL_MV),
            pl.BlockSpec((1, s, M_WIDTH), lambda bi: (bi, 0, COL_MO // M_WIDTH), pipeline_mode=pl.Buffered(1)),
            pl.BlockSpec((1, s, N_GATES), lambda bi: (bi, 0, 0)),
            pl.BlockSpec((1, N_GATES, nc, L), lambda bi: (bi, 0, 0, 0)),
            pl.BlockSpec((1, N_GATES), lambda bi: (0, 0)),
            pl.BlockSpec((N_GATES, 1, 1), lambda bi: (0, 0, 0)),
            pl.BlockSpec((1, M_WIDTH), lambda bi: (0, 0)),
        ],
        out_specs=pl.BlockSpec((1, s, M_WIDTH), lambda bi: (bi, 0, 0)),
        out_shape=jax.ShapeDtypeStruct((b, s, M_WIDTH), BF16),
        scratch_shapes=[
            pltpu.VMEM((ns, s, 128), F32),
            pltpu.VMEM((M_QK_WIDTH, s), BF16),
            pltpu.VMEM((ns, nc, L), F32), pltpu.VMEM((ns, nc, L), F32), pltpu.VMEM((ns, nc, 128), F32),
            pltpu.VMEM((ns, M_QK_DIM, M_V_DIM + 128), F32),
            pltpu.VMEM((s, M_WIDTH), F32),
        ],
        compiler_params=_params("arbitrary"),
        name="mlstm",
    )(proj, proj, proj, proj, gates, grow, b_gates.reshape(1, N_GATES), b_gates.reshape(N_GATES, 1, 1),
      m_gain.reshape(1, M_WIDTH))


def _outproj_kernel(att_ref, mo_ref, wa_ref, wm_ref, x_ref, g1_ref, sc_ref, sh_ref, wr_ref,
                    x1_ref, h2_ref, aff_ref):
    j = pl.program_id(2)
    nj = pl.num_programs(2)
    tn = OUTPROJ_TN
    mix = (jnp.dot(att_ref[0], wa_ref[...], preferred_element_type=F32)
           + jnp.dot(mo_ref[0], wm_ref[...], preferred_element_type=F32))
    x1t = x_ref[0] + g1_ref[0] * mix
    for jj in range(D_MODEL // tn):
        @pl.when(j == jj)
        def _(jj=jj):
            x1_ref[0, :, jj * tn:(jj + 1) * tn] = x1t

    @pl.when(j == nj - 1)
    def _():
        xf = x1_ref[0]
        r = lax.rsqrt(jnp.mean(xf * xf, axis=-1, keepdims=True) + EPS)
        h2 = (xf * r) * (1.0 + sc_ref[0]) + sh_ref[0]
        h2_ref[0] = h2.astype(BF16)
        logits = jnp.dot(h2, wr_ref[...], precision=HIGHEST, preferred_element_type=F32)
        e = jnp.exp(logits - jnp.max(logits, axis=-1, keepdims=True))
        aff_ref[0] = e / jnp.sum(e, axis=-1, keepdims=True)


def _outproj(att, m_out, w_out, x, mod6, w_router):
    b, s, d = x.shape
    tm, tn = OUTPROJ_TM, OUTPROJ_TN
    half = ATT_WIDTH
    assert w_out.shape[0] == 2 * half and M_WIDTH == half
    return pl.pallas_call(
        _outproj_kernel,
        grid=(b, s // tm, d // tn),
        in_specs=[
            pl.BlockSpec((1, tm, ATT_WIDTH), lambda bi, i, j: (bi, i, 0)),
            pl.BlockSpec((1, tm, M_WIDTH), lambda bi, i, j: (bi, i, 0)),
            pl.BlockSpec((half, tn), lambda bi, i, j: (0, j)),
            pl.BlockSpec((half, tn), lambda bi, i, j: (1, j)),
            pl.BlockSpec((1, tm, tn), lambda bi, i, j: (bi, i, j)),
            pl.BlockSpec((1, 1, tn), lambda bi, i, j: (bi * 6 + 2, 0, j)),
            pl.BlockSpec((1, 1, d), lambda bi, i, j: (bi * 6 + 4, 0, 0)),
            pl.BlockSpec((1, 1, d), lambda bi, i, j: (bi * 6 + 3, 0, 0)),
            pl.BlockSpec((d, N_EXPERTS), lambda bi, i, j: (0, 0)),
        ],
        out_specs=[
            pl.BlockSpec((1, tm, d), lambda bi, i, j: (bi, i, 0)),
            pl.BlockSpec((1, tm, d), lambda bi, i, j: (bi, i, 0)),
            pl.BlockSpec((1, tm, N_EXPERTS), lambda bi, i, j: (bi, i, 0)),
        ],
        out_shape=[
            jax.ShapeDtypeStruct((b, s, d), F32),
            jax.ShapeDtypeStruct((b, s, d), BF16),
            jax.ShapeDtypeStruct((b, s, N_EXPERTS), F32),
        ],
        compiler_params=_params("arbitrary", "arbitrary", "arbitrary"),
        name="out_proj",
    )(att, m_out, w_out, w_out, x, mod6, mod6, mod6, w_router)


def _prefix_sum_lanes(x, n):
    lane = lax.broadcasted_iota(jnp.int32, x.shape, 1)
    sh = 1
    while sh < n:
        x = x + jnp.where(lane >= sh, pltpu.roll(x, sh, 1), 0.0)
        sh *= 2
    return x


def _select_kernel(aff_ref, pos_ref, gate_ref, *, cap, seq):
    a = aff_ref[...]
    ne = a.shape[0]

    def body(_, lohi):
        lo, hi = lohi
        mid = jnp.where(lo > 0.0, 0.5 * (lo + hi), hi * (1.0 / 256.0))
        cnt = jnp.sum(jnp.where(a >= mid, 1.0, 0.0), axis=-1, keepdims=True)
        ok = cnt >= cap
        return jnp.where(ok, mid, lo), jnp.where(ok, hi, mid)

    lo0 = jnp.zeros((ne, 1), F32)
    hi0 = jnp.maximum(2.0 * jnp.max(a, axis=-1, keepdims=True), SELECT_MIN_UPPER)
    lo, hi = lax.fori_loop(0, SELECT_BISECTIONS, body, (lo0, hi0))
    above = a >= hi
    band = jnp.logical_and(a >= lo, jnp.logical_not(above))
    n_above = jnp.sum(jnp.where(above, 1.0, 0.0), axis=-1, keepdims=True)
    band_rank = _prefix_sum_lanes(jnp.where(band, 1.0, 0.0), seq)
    sel = jnp.where(above, 1.0, jnp.where(band, jnp.where(band_rank <= cap - n_above, 1.0, 0.0), 0.0))
    pos = _prefix_sum_lanes(sel, seq) - 1.0
    chosen = sel > 0.5
    pos_ref[...] = jnp.where(chosen, pos, -1.0)
    gate_ref[...] = jnp.where(chosen, a, 0.0)


def _select(aff_rows, cap):
    n, s = aff_rows.shape
    spec = pl.BlockSpec((n, s), lambda i: (0, 0))
    return pl.pallas_call(
        functools.partial(_select_kernel, cap=cap, seq=s),
        grid=(1,),
        in_specs=[spec],
        out_specs=[spec, spec],
        out_shape=[jax.ShapeDtypeStruct((n, s), F32), jax.ShapeDtypeStruct((n, s), F32)],
        compiler_params=_params("arbitrary"),
        name="expert_select",
    )(aff_rows)


def _gather_kernel(pos_ref, h_ref, xe_ref, *, cap):
    pos = pos_ref[0]
    slot = lax.broadcasted_iota(jnp.int32, (cap, pos.shape[-1]), 0).astype(F32)
    onehot = jnp.where(slot == pos, 1.0, 0.0).astype(BF16)
    xe_ref[0, 0] = jnp.dot(onehot, h_ref[0], preferred_element_type=F32).astype(BF16)


def _gather(pos_rows, h2, cap):
    b, s, d = h2.shape
    ne = N_EXPERTS
    return pl.pallas_call(
        functools.partial(_gather_kernel, cap=cap),
        grid=(b, ne),
        in_specs=[
            pl.BlockSpec((1, 1, s), lambda bi, e: (bi * ne + e, 0, 0)),
            pl.BlockSpec((1, s, d), lambda bi, e: (bi, 0, 0)),
        ],
        out_specs=pl.BlockSpec((1, 1, cap, d), lambda bi, e: (bi, e, 0, 0)),
        out_shape=jax.ShapeDtypeStruct((b, ne, cap, d), BF16),
        compiler_params=_params("arbitrary", "arbitrary"),
        name="expert_gather",
    )(pos_rows, h2)


def _ffn_kernel(xe_ref, wg_ref, wu_ref, wd_ref, ye_ref, acc_ref):
    f = pl.program_id(1)
    nf = pl.num_programs(1)
    nb, _, cap, d = xe_ref.shape
    x = xe_ref[...].reshape(nb * cap, d)
    g = jnp.dot(x, wg_ref[0, 0].astype(BF16), preferred_element_type=F32)
    u = jnp.dot(x, wu_ref[0, 0].astype(BF16), preferred_element_type=F32)
    hid = (g * jax.nn.sigmoid(g)) * u
    part = jnp.dot(hid.astype(BF16), wd_ref[0, 0].astype(BF16), preferred_element_type=F32)

    @pl.when(f == 0)
    def _():
        acc_ref[...] = part

    @pl.when(f > 0)
    def _():
        acc_ref[...] = acc_ref[...] + part

    @pl.when(f == nf - 1)
    def _():
        ye_ref[...] = acc_ref[...].reshape(nb, 1, cap, d).astype(BF16)


def _ffn(xe, w_gate, w_up, w_down, layer):
    b, ne, cap, d = xe.shape
    ff = w_gate.shape[-1]
    tf = FFN_TF
    return pl.pallas_call(
        _ffn_kernel,
        grid=(ne, ff // tf),
        in_specs=[
            pl.BlockSpec((b, 1, cap, d), lambda e, f: (0, e, 0, 0)),
            pl.BlockSpec((1, 1, d, tf), lambda e, f: (layer, e, 0, f)),
            pl.BlockSpec((1, 1, d, tf), lambda e, f: (layer, e, 0, f)),
            pl.BlockSpec((1, 1, tf, d), lambda e, f: (layer, e, f, 0)),
        ],
        out_specs=pl.BlockSpec((b, 1, cap, d), lambda e, f: (0, e, 0, 0)),
        out_shape=jax.ShapeDtypeStruct((b, ne, cap, d), BF16),
        scratch_shapes=[pltpu.VMEM((b * cap, d), F32)],
        compiler_params=_params("arbitrary", "arbitrary"),
        name="expert_ffn",
    )(xe, w_gate, w_up, w_down)


def _combine_kernel(pos_ref, gate_ref, ye_ref, x1_ref, g2_ref, o_ref, *, cap):
    e = pl.program_id(2)
    ne = pl.num_programs(2)
    lane = lax.broadcasted_iota(jnp.int32, pos_ref.shape[1:], 1)
    mine = lane == e
    pos = jnp.sum(jnp.where(mine, pos_ref[0], 0.0), axis=-1, keepdims=True)
    gate = jnp.sum(jnp.where(mine, gate_ref[0], 0.0), axis=-1, keepdims=True)
    slot = lax.broadcasted_iota(jnp.int32, (pos.shape[0], cap), 1).astype(F32)
    scat = jnp.where(slot == pos, gate, 0.0).astype(BF16)
    part = jnp.dot(scat, ye_ref[0, 0], preferred_element_type=F32)

    @pl.when(e == 0)
    def _():
        o_ref[0] = part

    @pl.when(e > 0)
    def _():
        o_ref[0] = o_ref[0] + part

    @pl.when(e == ne - 1)
    def _():
        o_ref[0] = x1_ref[0] + g2_ref[0] * o_ref[0]


def _combine(pos_cols, gate_cols, ye, x1, mod6):
    b, s, d = x1.shape
    ne, cap = ye.shape[1], ye.shape[2]
    tq = COMBINE_TQ
    return pl.pallas_call(
        functools.partial(_combine_kernel, cap=cap),
        grid=(b, s // tq, ne),
        in_specs=[
            pl.BlockSpec((1, tq, ne), lambda bi, i, e: (bi, i, 0)),
            pl.BlockSpec((1, tq, ne), lambda bi, i, e: (bi, i, 0)),
            pl.BlockSpec((1, 1, cap, d), lambda bi, i, e: (bi, e, 0, 0)),
            pl.BlockSpec((1, tq, d), lambda bi, i, e: (bi, i, 0)),
            pl.BlockSpec((1, 1, d), lambda bi, i, e: (bi * 6 + 5, 0, 0)),
        ],
        out_specs=pl.BlockSpec((1, tq, d), lambda bi, i, e: (bi, i, 0)),
        out_shape=jax.ShapeDtypeStruct((b, s, d), F32),
        compiler_params=_params("arbitrary", "arbitrary", "arbitrary"),
        name="expert_combine",
    )(pos_cols, gate_cols, ye, x1, mod6)


def _rope_tables(seq):
    inv = 1.0 / (ROPE_THETA ** (jnp.arange(0, ATT_HEAD_DIM, 2, dtype=F32) / ATT_HEAD_DIM))
    ang = jnp.arange(seq, dtype=F32)[:, None] * inv[None, :]
    ang = jnp.concatenate([ang, ang], axis=-1)
    sign = jnp.where(jnp.arange(ATT_HEAD_DIM) < ATT_HEAD_DIM // 2, -1.0, 1.0).astype(F32)
    return jnp.cos(ang), jnp.sin(ang) * sign


def _permute_in_cols(w):
    o_ak = ATT_WIDTH
    o_mq = o_ak + 2 * ATT_KV_WIDTH
    o_mv = o_mq + 2 * M_QK_WIDTH
    o_g = o_mv + 2 * M_WIDTH
    main = jnp.concatenate([w[:, :o_ak], w[:, o_mv:o_g], w[:, o_ak:o_mv]], axis=1)
    return main.astype(BF16), w[:, o_g:].astype(BF16)


def kernel(x, c, w_ada, b_ada, w_in, b_gates, q_gain, k_gain, sink, m_gain, w_out,
           w_router, w_gate, w_up, w_down):
    b, s, d = x.shape
    depth = w_ada.shape[0]
    cap = CAPACITY_FACTOR * s // N_EXPERTS
    cos, sin_signed = _rope_tables(s)
    c_pad = jnp.pad(c, ((0, 8 - b), (0, 0)))
    mod = _ada(c_pad, w_ada, b_ada)
    for l in range(depth):
        mod6 = mod[l, :b].reshape(b * 6, 1, d)
        w_main, w_g = _permute_in_cols(w_in[l])
        proj, gates = _inproj(x, mod6, w_main, w_g, cos, sin_signed,
                              q_gain[l].reshape(1, -1), k_gain[l].reshape(1, -1))
        att = _attention(proj, sink[l])
        m_out = _mlstm(proj, gates, b_gates[l].reshape(-1), m_gain[l])
        x1, h2, aff = _outproj(att, m_out, w_out[l].astype(BF16), x, mod6, w_router[l])
        pos_r, gate_r = _select(jnp.transpose(aff, (0, 2, 1)).reshape(b * N_EXPERTS, s), cap)
        xe = _gather(pos_r.reshape(b * N_EXPERTS, 1, s), h2, cap)
        ye = _ffn(xe, w_gate, w_up, w_down, l)
        to_cols = lambda t: jnp.transpose(t.reshape(b, N_EXPERTS, s), (0, 2, 1))
        x = _combine(to_cols(pos_r), to_cols(gate_r), ye, x1, mod6)
    return x
```

```python
import functools

import jax
import jax.numpy as jnp
from jax import lax
from jax.experimental import pallas as pl
from jax.experimental.pallas import tpu as pltpu

F32 = jnp.float32
BF16 = jnp.bfloat16
HIGHEST = lax.Precision.HIGHEST

D_MODEL = 2048
ATT_HEAD_DIM = 128
ATT_HEADS = 8
ATT_KV_HEADS = 2
ATT_GROUP = ATT_HEADS // ATT_KV_HEADS
ATT_WIDTH = ATT_HEADS * ATT_HEAD_DIM
ATT_KV_WIDTH = ATT_KV_HEADS * ATT_HEAD_DIM
WINDOW = 128
ATT_BLOCK = 128
ROPE_THETA = 10000.0
M_HEADS = 4
M_V_DIM = 256
M_QK_DIM = 128
M_WIDTH = M_HEADS * M_V_DIM
M_QK_WIDTH = M_HEADS * M_QK_DIM
MLSTM_BLOCK = 256
N_GATES = 4 * M_HEADS
N_EXPERTS = 16
EXPERT_FF = D_MODEL // 2
CAPACITY_FACTOR = 2
EPS = 1e-6
NEG = -1e30

COL_AQ = 0
COL_MV = ATT_WIDTH
COL_MO = COL_MV + M_WIDTH
COL_AK = COL_MO + M_WIDTH
COL_AV = COL_AK + ATT_KV_WIDTH
COL_MQ = COL_AV + ATT_KV_WIDTH
COL_MK = COL_MQ + M_QK_WIDTH
PROJ_WIDTH = COL_MK + M_QK_WIDTH

VMEM_LIMIT_BYTES = 56 * 1024 * 1024

ADA_TN = 1024
INPROJ_TM = 1024
INPROJ_TN = 512
OUTPROJ_TM = 512
OUTPROJ_TN = 512
FFN_TF = 256
COMBINE_TN = 512
COMBINE_BUILD_ROWS = 256
SELECT_BISECTIONS = 64
SELECT_MIN_UPPER = 1e-30


def _params(*sem):
    return pltpu.CompilerParams(dimension_semantics=sem, vmem_limit_bytes=VMEM_LIMIT_BYTES)


def _ada_kernel(c_ref, w_ref, b_ref, o_ref):
    c = c_ref[...]
    c_act = c * jax.nn.sigmoid(c)
    o_ref[0] = jnp.dot(c_act, w_ref[0], precision=HIGHEST, preferred_element_type=F32) + b_ref[0]


def _ada(c_pad, w_ada, b_ada):
    depth, d, n = w_ada.shape
    rows = c_pad.shape[0]
    return pl.pallas_call(
        _ada_kernel,
        grid=(depth, n // ADA_TN),
        in_specs=[
            pl.BlockSpec((rows, d), lambda l, j: (0, 0)),
            pl.BlockSpec((1, d, ADA_TN), lambda l, j: (l, 0, j)),
            pl.BlockSpec((1, 1, ADA_TN), lambda l, j: (l, 0, j)),
        ],
        out_specs=pl.BlockSpec((1, rows, ADA_TN), lambda l, j: (l, 0, j)),
        out_shape=jax.ShapeDtypeStruct((depth, rows, n), F32),
        compiler_params=_params("arbitrary", "arbitrary"),
        name="ada_mod",
    )(c_pad, w_ada, b_ada.reshape(depth, 1, n))


def _inproj_kernel(x_ref, sc_ref, sh_ref, w_ref, wg_ref, cos_ref, sin_ref, qg_ref, kg_ref,
                   o_ref, g_ref, h_scr):
    j = pl.program_id(2)

    @pl.when(j == 0)
    def _():
        x = x_ref[0]
        r = lax.rsqrt(jnp.mean(x * x, axis=-1, keepdims=True) + EPS)
        h = (x * r) * (1.0 + sc_ref[0]) + sh_ref[0]
        hb = h.astype(BF16)
        h_scr[...] = hb
        g_ref[0] = jnp.dot(hb, wg_ref[...], preferred_element_type=F32)

    y = jnp.dot(h_scr[...], w_ref[...], preferred_element_type=F32)

    def norm_rope(t, gain):
        tn = t * lax.rsqrt(jnp.mean(t * t, axis=-1, keepdims=True) + EPS) * gain
        return tn * cos_ref[...] + pltpu.roll(tn, ATT_HEAD_DIM // 2, 1) * sin_ref[...]

    heads_per_tile = INPROJ_TN // ATT_HEAD_DIM
    q_tiles = ATT_WIDTH // INPROJ_TN
    kv_tile = COL_AK // INPROJ_TN
    scale = ATT_HEAD_DIM ** -0.5

    @pl.when(j < q_tiles)
    def _():
        for u in range(heads_per_tile):
            sl = slice(u * ATT_HEAD_DIM, (u + 1) * ATT_HEAD_DIM)
            o_ref[0, :, sl] = (norm_rope(y[:, sl], qg_ref[...]) * scale).astype(BF16)

    @pl.when(j == kv_tile)
    def _():
        for u in range(ATT_KV_HEADS):
            sl = slice(u * ATT_HEAD_DIM, (u + 1) * ATT_HEAD_DIM)
            o_ref[0, :, sl] = norm_rope(y[:, sl], kg_ref[...]).astype(BF16)
        o_ref[0, :, ATT_KV_WIDTH:] = y[:, ATT_KV_WIDTH:].astype(BF16)

    @pl.when(jnp.logical_and(j >= q_tiles, j != kv_tile))
    def _():
        o_ref[0] = y.astype(BF16)


def _inproj(x, mod6, w_main, w_gate, cos, sin_signed, q_gain, k_gain):
    b, s, d = x.shape
    tm, tn = INPROJ_TM, INPROJ_TN
    assert COL_AK % tn == 0 and ATT_WIDTH % tn == 0 and 2 * ATT_KV_WIDTH == tn
    return pl.pallas_call(
        _inproj_kernel,
        grid=(b, s // tm, PROJ_WIDTH // tn),
        in_specs=[
            pl.BlockSpec((1, tm, d), lambda bi, i, j: (bi, i, 0)),
            pl.BlockSpec((1, 1, d), lambda bi, i, j: (bi * 6 + 1, 0, 0)),
            pl.BlockSpec((1, 1, d), lambda bi, i, j: (bi * 6 + 0, 0, 0)),
            pl.BlockSpec((d, tn), lambda bi, i, j: (0, j)),
            pl.BlockSpec((d, N_GATES), lambda bi, i, j: (0, 0)),
            pl.BlockSpec((tm, ATT_HEAD_DIM), lambda bi, i, j: (i, 0)),
            pl.BlockSpec((tm, ATT_HEAD_DIM), lambda bi, i, j: (i, 0)),
            pl.BlockSpec((1, ATT_HEAD_DIM), lambda bi, i, j: (0, 0)),
            pl.BlockSpec((1, ATT_HEAD_DIM), lambda bi, i, j: (0, 0)),
        ],
        out_specs=[
            pl.BlockSpec((1, tm, tn), lambda bi, i, j: (bi, i, j)),
            pl.BlockSpec((1, tm, N_GATES), lambda bi, i, j: (bi, i, 0)),
        ],
        out_shape=[
            jax.ShapeDtypeStruct((b, s, PROJ_WIDTH), BF16),
            jax.ShapeDtypeStruct((b, s, N_GATES), F32),
        ],
        scratch_shapes=[pltpu.VMEM((tm, d), BF16)],
        compiler_params=_params("arbitrary", "arbitrary", "arbitrary"),
        name="in_proj",
    )(x, mod6, mod6, w_main, w_gate, cos, sin_signed, q_gain, k_gain)


def _attn_kernel(sink_ref, q_ref, kl_ref, kc_ref, kr_ref, vl_ref, vc_ref, vr_ref, o_ref, *, n_blocks):
    n = pl.program_id(1)
    L = ATT_BLOCK
    rows = ATT_GROUP * L
    i = lax.broadcasted_iota(jnp.int32, (rows, 3 * L), 0) & (L - 1)
    jj = lax.broadcasted_iota(jnp.int32, (rows, 3 * L), 1)
    dist = jj - L - i
    valid = (jnp.abs(dist) <= WINDOW) & ((jj >= L) | (n > 0)) & ((jj < 2 * L) | (n < n_blocks - 1))
    rgrp = lax.broadcasted_iota(jnp.int32, (rows, 1), 0) // L
    for kv in range(ATT_KV_HEADS):
        hs = slice(kv * ATT_HEAD_DIM, (kv + 1) * ATT_HEAD_DIM)
        q = jnp.concatenate(
            [q_ref[0, :, (kv * ATT_GROUP + g) * ATT_HEAD_DIM:(kv * ATT_GROUP + g + 1) * ATT_HEAD_DIM]
             for g in range(ATT_GROUP)], axis=0)
        kb = jnp.concatenate([kl_ref[0, :, hs], kc_ref[0, :, hs], kr_ref[0, :, hs]], axis=0)
        vb = jnp.concatenate([vl_ref[0, :, hs], vc_ref[0, :, hs], vr_ref[0, :, hs]], axis=0)
        s = lax.dot_general(q, kb, (((1,), (1,)), ((), ())), preferred_element_type=F32)
        s = jnp.where(valid, s, NEG)
        sink = jnp.zeros((rows, 1), F32)
        for g in range(ATT_GROUP):
            sink = jnp.where(rgrp == g, sink_ref[kv * ATT_GROUP + g], sink)
        m = jnp.maximum(jnp.max(s, axis=-1, keepdims=True), sink)
        p = jnp.exp(s - m)
        denom = jnp.sum(p, axis=-1, keepdims=True) + jnp.exp(sink - m)
        o = jnp.dot(p.astype(BF16), vb, preferred_element_type=F32) / denom
        for g in range(ATT_GROUP):
            h = kv * ATT_GROUP + g
            o_ref[0, :, h * ATT_HEAD_DIM:(h + 1) * ATT_HEAD_DIM] = o[g * L:(g + 1) * L].astype(BF16)


def _attention(proj, sink):
    b, s, _ = proj.shape
    L = ATT_BLOCK
    nb = s // L
    kblk = COL_AK // ATT_KV_WIDTH
    vblk = COL_AV // ATT_KV_WIDTH
    left = lambda n: jnp.maximum(n - 1, 0)
    right = lambda n: jnp.minimum(n + 1, nb - 1)
    kv_spec = lambda col, f: pl.BlockSpec((1, L, ATT_KV_WIDTH), lambda bi, n: (bi, f(n), col))
    same = lambda n: n
    return pl.pallas_call(
        functools.partial(_attn_kernel, n_blocks=nb),
        grid=(b, nb),
        in_specs=[
            pl.BlockSpec(memory_space=pltpu.SMEM),
            pl.BlockSpec((1, L, ATT_WIDTH), lambda bi, n: (bi, n, COL_AQ // ATT_WIDTH)),
            kv_spec(kblk, left), kv_spec(kblk, same), kv_spec(kblk, right),
            kv_spec(vblk, left), kv_spec(vblk, same), kv_spec(vblk, right),
        ],
        out_specs=pl.BlockSpec((1, L, ATT_WIDTH), lambda bi, n: (bi, n, 0)),
        out_shape=jax.ShapeDtypeStruct((b, s, ATT_WIDTH), BF16),
        compiler_params=_params("arbitrary", "arbitrary"),
        name="window_attn",
    )(sink, proj, proj, proj, proj, proj, proj, proj)


def _mlstm_kernel(q_ref, k_ref, v_ref, mo_ref, gcol_ref, grow_ref, bcol_ref, brow_ref, gain_ref, o_ref,
                  bc_scr, kt_scr, ar_scr, wr_scr, bt_scr, c_scr, h_scr, *, seq):
    L = MLSTM_BLOCK
    nc = seq // L
    ns = 2 * M_HEADS
    scale = M_QK_DIM ** -0.5
    row = lax.broadcasted_iota(jnp.int32, (L, L), 0)
    col = lax.broadcasted_iota(jnp.int32, (L, L), 1)
    lower = col <= row
    upper = col >= row
    tril = lower.astype(F32)
    triu = upper.astype(F32)

    fwd_lane = lax.broadcasted_iota(jnp.int32, (1, N_GATES), 1) < ns + M_HEADS
    for c in range(nc):
        rows = slice(c * L, (c + 1) * L)
        lf = jax.nn.log_sigmoid(gcol_ref[0, rows, :] + bcol_ref[...])
        pre = jnp.dot(tril, lf, precision=HIGHEST, preferred_element_type=F32)
        suf = jnp.dot(triu, lf, precision=HIGHEST, preferred_element_type=F32)
        bcol = jnp.where(fwd_lane, pre, suf)
        for k in range(ns):
            bc_scr[k, rows, :] = jnp.broadcast_to(bcol[:, ns + k:ns + k + 1], (L, 128))
        kt_scr[:, rows] = k_ref[0, rows, :].astype(F32).T.astype(BF16)

    gr = grow_ref[0] + brow_ref[...]
    ig_r = gr[:ns].reshape(ns * nc, L)
    lf_r = jax.nn.log_sigmoid(gr[ns:]).reshape(ns * nc, L)
    pre_r = jnp.dot(lf_r, triu, precision=HIGHEST, preferred_element_type=F32)
    suf_r = jnp.dot(lf_r, tril, precision=HIGHEST, preferred_element_type=F32)
    fwd_rows = lax.broadcasted_iota(jnp.int32, (ns * nc, 1), 0) < M_HEADS * nc
    a_r = ig_r - jnp.where(fwd_rows, pre_r, suf_r)
    btot = jnp.sum(lf_r, axis=-1, keepdims=True)
    ar_scr[...] = a_r.reshape(ns, nc, L)
    wr_scr[...] = (btot + a_r).reshape(ns, nc, L)
    bt_scr[...] = jnp.broadcast_to(btot, (ns * nc, 128)).reshape(ns, nc, 128)

    c_scr[...] = jnp.zeros_like(c_scr)
    h_scr[...] = jnp.zeros_like(h_scr)
    ones_cols = jnp.ones((L, 128), BF16)

    twice = lambda t: jnp.concatenate([t, t], axis=1)

    def chunk_step(k, c, m_st):
        d, h = divmod(k, M_HEADS)
        r0 = pl.multiple_of(c * L, L)
        q = q_ref[0, pl.ds(r0, L), h * M_QK_DIM:(h + 1) * M_QK_DIM]
        kk = k_ref[0, pl.ds(r0, L), h * M_QK_DIM:(h + 1) * M_QK_DIM]
        v_ext = jnp.concatenate([v_ref[0, pl.ds(r0, L), h * M_V_DIM:(h + 1) * M_V_DIM], ones_cols], axis=1)
        bc = bc_scr[k, pl.ds(r0, L), :]
        ar = ar_scr[k, pl.ds(c, 1), :]
        wr = wr_scr[k, pl.ds(c, 1), :]
        bt = bt_scr[k, pl.ds(c, 1), :]
        dm = jnp.where(lower if d == 0 else upper, twice(bc) + ar, NEG)
        g_inter = bc + m_st
        m_t = jnp.maximum(jnp.broadcast_to(jnp.max(dm, axis=-1, keepdims=True), (L, 128)), g_inter)
        e_inter = jnp.exp(g_inter - m_t) * scale
        s_qk = lax.dot_general(q, kk, (((1,), (1,)), ((), ())), preferred_element_type=F32)
        p = s_qk * scale * jnp.exp(dm - twice(m_t))
        qc = jnp.dot(q, c_scr[k].astype(BF16), preferred_element_type=F32)
        pv = jnp.dot(p.astype(BF16), v_ext, preferred_element_type=F32)
        num = pv[:, :M_V_DIM] + twice(e_inter) * qc[:, :M_V_DIM]
        den = pv[:, M_V_DIM:] + e_inter * qc[:, M_V_DIM:]
        inv = 1.0 / jnp.maximum(jnp.abs(den), jnp.exp(-m_t))
        hsl = (pl.ds(r0, L), slice(h * M_V_DIM, (h + 1) * M_V_DIM))
        h_scr[hsl] = h_scr[hsl] + num * twice(inv)
        m_new = jnp.maximum(bt + m_st, jnp.broadcast_to(jnp.max(wr, axis=-1, keepdims=True), (1, 128)))
        a = jnp.exp(bt + m_st - m_new)
        kt = kt_scr[h * M_QK_DIM:(h + 1) * M_QK_DIM, pl.ds(r0, L)]
        ek_t = (kt.astype(F32) * jnp.exp(wr - twice(m_new))).astype(BF16)
        c_scr[k] = jnp.concatenate([a, a, a], axis=1) * c_scr[k] + jnp.dot(ek_t, v_ext, preferred_element_type=F32)
        return m_new

    def body(it, ms):
        out = []
        for k in range(ns):
            c = it if k < M_HEADS else nc - 1 - it
            out.append(chunk_step(k, c, ms[k]))
        return tuple(out)

    lax.fori_loop(0, nc, body, tuple(jnp.zeros((1, 128), F32) for _ in range(ns)))

    def fin(i, carry):
        r0 = pl.multiple_of(i * L, L)
        for h in range(M_HEADS):
            sl = slice(h * M_V_DIM, (h + 1) * M_V_DIM)
            x = h_scr[pl.ds(r0, L), sl]
            y = x * lax.rsqrt(jnp.mean(x * x, axis=-1, keepdims=True) + EPS) * gain_ref[:, sl]
            o_ref[0, pl.ds(r0, L), sl] = (y * jax.nn.sigmoid(mo_ref[0, pl.ds(r0, L), sl].astype(F32))).astype(BF16)
        return carry

    lax.fori_loop(0, nc, fin, 0)


def _mlstm(proj, gates, b_gates, m_gain):
    b, s, _ = proj.shape
    L = MLSTM_BLOCK
    nc = s // L
    ns = 2 * M_HEADS
    grow = jnp.transpose(gates, (0, 2, 1)).reshape(b, N_GATES, nc, L)
    col = lambda width, off: pl.BlockSpec((1, s, width), lambda bi: (bi, 0, off // width))
    return pl.pallas_call(
        functools.partial(_mlstm_kernel, seq=s),
        grid=(b,),
        in_specs=[
            col(M_QK_WIDTH, COL_MQ), col(M_QK_WIDTH, COL_MK), col(M_WIDTH, COL_MV),
            pl.BlockSpec((1, s, M_WIDTH), lambda bi: (bi, 0, COL_MO // M_WIDTH), pipeline_mode=pl.Buffered(1)),
            pl.BlockSpec((1, s, N_GATES), lambda bi: (bi, 0, 0)),
            pl.BlockSpec((1, N_GATES, nc, L), lambda bi: (bi, 0, 0, 0)),
            pl.BlockSpec((1, N_GATES), lambda bi: (0, 0)),
            pl.BlockSpec((N_GATES, 1, 1), lambda bi: (0, 0, 0)),
            pl.BlockSpec((1, M_WIDTH), lambda bi: (0, 0)),
        ],
        out_specs=pl.BlockSpec((1, s, M_WIDTH), lambda bi: (bi, 0, 0)),
        out_shape=jax.ShapeDtypeStruct((b, s, M_WIDTH), BF16),
        scratch_shapes=[
            pltpu.VMEM((ns, s, 128), F32),
            pltpu.VMEM((M_QK_WIDTH, s), BF16),
            pltpu.VMEM((ns, nc, L), F32), pltpu.VMEM((ns, nc, L), F32), pltpu.VMEM((ns, nc, 128), F32),
            pltpu.VMEM((ns, M_QK_DIM, M_V_DIM + 128), F32),
            pltpu.VMEM((s, M_WIDTH), F32),
        ],
        compiler_params=_params("arbitrary"),
        name="mlstm",
    )(proj, proj, proj, proj, gates, grow, b_gates.reshape(1, N_GATES), b_gates.reshape(N_GATES, 1, 1),
      m_gain.reshape(1, M_WIDTH))


def _outproj_kernel(att_ref, mo_ref, wa_ref, wm_ref, x_ref, g1_ref, sc_ref, sh_ref, wr_ref,
                    x1_ref, h2_ref, aff_ref):
    j = pl.program_id(2)
    nj = pl.num_programs(2)
    tn = OUTPROJ_TN
    mix = (jnp.dot(att_ref[0], wa_ref[...], preferred_element_type=F32)
           + jnp.dot(mo_ref[0], wm_ref[...], preferred_element_type=F32))
    x1t = x_ref[0] + g1_ref[0] * mix
    for jj in range(D_MODEL // tn):
        @pl.when(j == jj)
        def _(jj=jj):
            x1_ref[0, :, jj * tn:(jj + 1) * tn] = x1t

    @pl.when(j == nj - 1)
    def _():
        xf = x1_ref[0]
        r = lax.rsqrt(jnp.mean(xf * xf, axis=-1, keepdims=True) + EPS)
        h2 = (xf * r) * (1.0 + sc_ref[0]) + sh_ref[0]
        h2_hi = h2.astype(BF16)
        h2_ref[0] = h2_hi
        h2_lo = (h2 - h2_hi.astype(F32)).astype(BF16)
        hi_terms = jnp.dot(h2_hi, wr_ref[...], preferred_element_type=F32)
        lo_term = jnp.dot(h2_lo, wr_ref[:, :N_EXPERTS], preferred_element_type=F32)
        logits = hi_terms[:, :N_EXPERTS] + hi_terms[:, N_EXPERTS:] + lo_term
        e = jnp.exp(logits - jnp.max(logits, axis=-1, keepdims=True))
        aff_ref[0] = e / jnp.sum(e, axis=-1, keepdims=True)


def _outproj(att, m_out, w_out, x, mod6, w_router):
    b, s, d = x.shape
    tm, tn = OUTPROJ_TM, OUTPROJ_TN
    half = ATT_WIDTH
    assert w_out.shape[0] == 2 * half and M_WIDTH == half
    wr_hi = w_router.astype(BF16)
    wr_lo = (w_router - wr_hi.astype(F32)).astype(BF16)
    return pl.pallas_call(
        _outproj_kernel,
        grid=(b, s // tm, d // tn),
        in_specs=[
            pl.BlockSpec((1, tm, ATT_WIDTH), lambda bi, i, j: (bi, i, 0)),
            pl.BlockSpec((1, tm, M_WIDTH), lambda bi, i, j: (bi, i, 0)),
            pl.BlockSpec((half, tn), lambda bi, i, j: (0, j)),
            pl.BlockSpec((half, tn), lambda bi, i, j: (1, j)),
            pl.BlockSpec((1, tm, tn), lambda bi, i, j: (bi, i, j)),
            pl.BlockSpec((1, 1, tn), lambda bi, i, j: (bi * 6 + 2, 0, j)),
            pl.BlockSpec((1, 1, d), lambda bi, i, j: (bi * 6 + 4, 0, 0)),
            pl.BlockSpec((1, 1, d), lambda bi, i, j: (bi * 6 + 3, 0, 0)),
            pl.BlockSpec((d, 2 * N_EXPERTS), lambda bi, i, j: (0, 0)),
        ],
        out_specs=[
            pl.BlockSpec((1, tm, d), lambda bi, i, j: (bi, i, 0)),
            pl.BlockSpec((1, tm, d), lambda bi, i, j: (bi, i, 0)),
            pl.BlockSpec((1, tm, N_EXPERTS), lambda bi, i, j: (bi, i, 0)),
        ],
        out_shape=[
            jax.ShapeDtypeStruct((b, s, d), F32),
            jax.ShapeDtypeStruct((b, s, d), BF16),
            jax.ShapeDtypeStruct((b, s, N_EXPERTS), F32),
        ],
        compiler_params=_params("arbitrary", "arbitrary", "arbitrary"),
        name="out_proj",
    )(att, m_out, w_out, w_out, x, mod6, mod6, mod6, jnp.concatenate([wr_hi, wr_lo], axis=1))


def _prefix_sum_lanes(x, n):
    lane = lax.broadcasted_iota(jnp.int32, x.shape, 1)
    sh = 1
    while sh < n:
        x = x + jnp.where(lane >= sh, pltpu.roll(x, sh, 1), 0.0)
        sh *= 2
    return x


def _select_kernel(aff_ref, pos_ref, gate_ref, *, cap, seq):
    a = aff_ref[...]
    ne = a.shape[0]

    def body(_, lohi):
        lo, hi = lohi
        mid = jnp.where(lo > 0.0, 0.5 * (lo + hi), hi * (1.0 / 256.0))
        cnt = jnp.sum(jnp.where(a >= mid, 1.0, 0.0), axis=-1, keepdims=True)
        ok = cnt >= cap
        return jnp.where(ok, mid, lo), jnp.where(ok, hi, mid)

    lo0 = jnp.zeros((ne, 1), F32)
    hi0 = jnp.maximum(2.0 * jnp.max(a, axis=-1, keepdims=True), SELECT_MIN_UPPER)
    lo, hi = lax.fori_loop(0, SELECT_BISECTIONS, body, (lo0, hi0))
    above = a >= hi
    band = jnp.logical_and(a >= lo, jnp.logical_not(above))
    n_above = jnp.sum(jnp.where(above, 1.0, 0.0), axis=-1, keepdims=True)
    band_rank = _prefix_sum_lanes(jnp.where(band, 1.0, 0.0), seq)
    sel = jnp.where(above, 1.0, jnp.where(band, jnp.where(band_rank <= cap - n_above, 1.0, 0.0), 0.0))
    pos = _prefix_sum_lanes(sel, seq) - 1.0
    chosen = sel > 0.5
    pos_ref[...] = jnp.where(chosen, pos, -1.0)
    gate_ref[...] = jnp.where(chosen, a, 0.0)


def _select(aff_rows, cap):
    n, s = aff_rows.shape
    spec = pl.BlockSpec((n, s), lambda i: (0, 0))
    return pl.pallas_call(
        functools.partial(_select_kernel, cap=cap, seq=s),
        grid=(1,),
        in_specs=[spec],
        out_specs=[spec, spec],
        out_shape=[jax.ShapeDtypeStruct((n, s), F32), jax.ShapeDtypeStruct((n, s), F32)],
        compiler_params=_params("arbitrary"),
        name="expert_select",
    )(aff_rows)


def _gather_kernel(pos_ref, h_ref, xe_ref, *, cap):
    pos = pos_ref[0]
    slot = lax.broadcasted_iota(jnp.int32, (cap, pos.shape[-1]), 0).astype(F32)
    onehot = jnp.where(slot == pos, 1.0, 0.0).astype(BF16)
    xe_ref[0, 0] = jnp.dot(onehot, h_ref[0], preferred_element_type=F32).astype(BF16)


def _gather(pos_rows, h2, cap):
    b, s, d = h2.shape
    ne = N_EXPERTS
    return pl.pallas_call(
        functools.partial(_gather_kernel, cap=cap),
        grid=(b, ne),
        in_specs=[
            pl.BlockSpec((1, 1, s), lambda bi, e: (bi * ne + e, 0, 0)),
            pl.BlockSpec((1, s, d), lambda bi, e: (bi, 0, 0)),
        ],
        out_specs=pl.BlockSpec((1, 1, cap, d), lambda bi, e: (bi, e, 0, 0)),
        out_shape=jax.ShapeDtypeStruct((b, ne, cap, d), BF16),
        compiler_params=_params("arbitrary", "arbitrary"),
        name="expert_gather",
    )(pos_rows, h2)


def _ffn_kernel(xe_ref, wg_ref, wu_ref, wd_ref, ye_ref, acc_ref):
    f = pl.program_id(1)
    nf = pl.num_programs(1)
    nb, _, cap, d = xe_ref.shape
    x = xe_ref[...].reshape(nb * cap, d)
    g = jnp.dot(x, wg_ref[0, 0].astype(BF16), preferred_element_type=F32)
    u = jnp.dot(x, wu_ref[0, 0].astype(BF16), preferred_element_type=F32)
    hid = (g * jax.nn.sigmoid(g)) * u
    part = jnp.dot(hid.astype(BF16), wd_ref[0, 0].astype(BF16), preferred_element_type=F32)

    @pl.when(f == 0)
    def _():
        acc_ref[...] = part

    @pl.when(f > 0)
    def _():
        acc_ref[...] = acc_ref[...] + part

    @pl.when(f == nf - 1)
    def _():
        ye_ref[...] = acc_ref[...].reshape(nb, 1, cap, d).astype(BF16)


def _ffn(xe, w_gate, w_up, w_down, layer):
    b, ne, cap, d = xe.shape
    ff = w_gate.shape[-1]
    tf = FFN_TF
    return pl.pallas_call(
        _ffn_kernel,
        grid=(ne, ff // tf),
        in_specs=[
            pl.BlockSpec((b, 1, cap, d), lambda e, f: (0, e, 0, 0)),
            pl.BlockSpec((1, 1, d, tf), lambda e, f: (layer, e, 0, f)),
            pl.BlockSpec((1, 1, d, tf), lambda e, f: (layer, e, 0, f)),
            pl.BlockSpec((1, 1, tf, d), lambda e, f: (layer, e, f, 0)),
        ],
        out_specs=pl.BlockSpec((b, 1, cap, d), lambda e, f: (0, e, 0, 0)),
        out_shape=jax.ShapeDtypeStruct((b, ne, cap, d), BF16),
        scratch_shapes=[pltpu.VMEM((b * cap, d), F32)],
        compiler_params=_params("arbitrary", "arbitrary"),
        name="expert_ffn",
    )(xe, w_gate, w_up, w_down)


def _combine_kernel(pos_ref, gate_ref, ye_ref, x1_ref, g2_ref, o_ref, scat_scr, *, cap):
    j = pl.program_id(1)
    seq, ne = pos_ref.shape[1], pos_ref.shape[2]
    rt = COMBINE_BUILD_ROWS

    @pl.when(j == 0)
    def _():
        slot = lax.broadcasted_iota(jnp.int32, (rt, cap), 1).astype(F32)

        def build(i, carry):
            r0 = pl.multiple_of(i * rt, rt)
            pos = pos_ref[0, pl.ds(r0, rt), :]
            gate = gate_ref[0, pl.ds(r0, rt), :]
            for e in range(ne):
                scat_scr[pl.ds(r0, rt), e * cap:(e + 1) * cap] = jnp.where(
                    slot == pos[:, e:e + 1], gate[:, e:e + 1], 0.0).astype(BF16)
            return carry

        lax.fori_loop(0, seq // rt, build, 0)

    acc = jnp.dot(scat_scr[...], ye_ref[0], preferred_element_type=F32)
    o_ref[0] = x1_ref[0] + g2_ref[0] * acc


def _combine(pos_cols, gate_cols, ye, x1, mod6):
    b, s, d = x1.shape
    ne, cap = ye.shape[1], ye.shape[2]
    tn = COMBINE_TN
    return pl.pallas_call(
        functools.partial(_combine_kernel, cap=cap),
        grid=(b, d // tn),
        in_specs=[
            pl.BlockSpec((1, s, ne), lambda bi, j: (bi, 0, 0)),
            pl.BlockSpec((1, s, ne), lambda bi, j: (bi, 0, 0)),
            pl.BlockSpec((1, ne * cap, tn), lambda bi, j: (bi, 0, j)),
            pl.BlockSpec((1, s, tn), lambda bi, j: (bi, 0, j)),
            pl.BlockSpec((1, 1, tn), lambda bi, j: (bi * 6 + 5, 0, j)),
        ],
        out_specs=pl.BlockSpec((1, s, tn), lambda bi, j: (bi, 0, j)),
        out_shape=jax.ShapeDtypeStruct((b, s, d), F32),
        scratch_shapes=[pltpu.VMEM((s, ne * cap), BF16)],
        compiler_params=_params("arbitrary", "arbitrary"),
        name="expert_combine",
    )(pos_cols, gate_cols, ye.reshape(b, ne * cap, d), x1, mod6)


def _rope_tables(seq):
    inv = 1.0 / (ROPE_THETA ** (jnp.arange(0, ATT_HEAD_DIM, 2, dtype=F32) / ATT_HEAD_DIM))
    ang = jnp.arange(seq, dtype=F32)[:, None] * inv[None, :]
    ang = jnp.concatenate([ang, ang], axis=-1)
    sign = jnp.where(jnp.arange(ATT_HEAD_DIM) < ATT_HEAD_DIM // 2, -1.0, 1.0).astype(F32)
    return jnp.cos(ang), jnp.sin(ang) * sign


def _permute_in_cols(w):
    o_ak = ATT_WIDTH
    o_mq = o_ak + 2 * ATT_KV_WIDTH
    o_mv = o_mq + 2 * M_QK_WIDTH
    o_g = o_mv + 2 * M_WIDTH
    main = jnp.concatenate([w[:, :o_ak], w[:, o_mv:o_g], w[:, o_ak:o_mv]], axis=1)
    return main.astype(BF16), w[:, o_g:].astype(BF16)


def kernel(x, c, w_ada, b_ada, w_in, b_gates, q_gain, k_gain, sink, m_gain, w_out,
           w_router, w_gate, w_up, w_down):
    b, s, d = x.shape
    depth = w_ada.shape[0]
    cap = CAPACITY_FACTOR * s // N_EXPERTS
    cos, sin_signed = _rope_tables(s)
    c_pad = jnp.pad(c, ((0, 8 - b), (0, 0)))
    mod = _ada(c_pad, w_ada, b_ada)
    for l in range(depth):
        mod6 = mod[l, :b].reshape(b * 6, 1, d)
        w_main, w_g = _permute_in_cols(w_in[l])
        proj, gates = _inproj(x, mod6, w_main, w_g, cos, sin_signed,
                              q_gain[l].reshape(1, -1), k_gain[l].reshape(1, -1))
        att = _attention(proj, sink[l])
        m_out = _mlstm(proj, gates, b_gates[l].reshape(-1), m_gain[l])
        x1, h2, aff = _outproj(att, m_out, w_out[l].astype(BF16), x, mod6, w_router[l])
        pos_r, gate_r = _select(jnp.transpose(aff, (0, 2, 1)).reshape(b * N_EXPERTS, s), cap)
        xe = _gather(pos_r.reshape(b * N_EXPERTS, 1, s), h2, cap)
        ye = _ffn(xe, w_gate, w_up, w_down, l)
        to_cols = lambda t: jnp.transpose(t.reshape(b, N_EXPERTS, s), (0, 2, 1))
        x = _combine(to_cols(pos_r), to_cols(gate_r), ye, x1, mod6)
    return x
```

```python
import functools

import jax
import jax.numpy as jnp
from jax import lax
from jax.experimental import pallas as pl
from jax.experimental.pallas import tpu as pltpu

F32 = jnp.float32
BF16 = jnp.bfloat16
HIGHEST = lax.Precision.HIGHEST

D_MODEL = 2048
ATT_HEAD_DIM = 128
ATT_HEADS = 8
ATT_KV_HEADS = 2
ATT_GROUP = ATT_HEADS // ATT_KV_HEADS
ATT_WIDTH = ATT_HEADS * ATT_HEAD_DIM
ATT_KV_WIDTH = ATT_KV_HEADS * ATT_HEAD_DIM
WINDOW = 128
ATT_BLOCK = 128
ROPE_THETA = 10000.0
M_HEADS = 4
M_V_DIM = 256
M_QK_DIM = 128
M_WIDTH = M_HEADS * M_V_DIM
M_QK_WIDTH = M_HEADS * M_QK_DIM
MLSTM_BLOCK = 256
N_GATES = 4 * M_HEADS
N_EXPERTS = 16
EXPERT_FF = D_MODEL // 2
CAPACITY_FACTOR = 2
EPS = 1e-6
NEG = -1e30

COL_AQ = 0
COL_MV = ATT_WIDTH
COL_MO = COL_MV + M_WIDTH
COL_AK = COL_MO + M_WIDTH
COL_AV = COL_AK + ATT_KV_WIDTH
COL_MQ = COL_AV + ATT_KV_WIDTH
COL_MK = COL_MQ + M_QK_WIDTH
PROJ_WIDTH = COL_MK + M_QK_WIDTH

VMEM_LIMIT_BYTES = 56 * 1024 * 1024

ADA_TN = 1024
INPROJ_TM = 1024
INPROJ_TN = 512
OUTPROJ_TM = 512
OUTPROJ_TN = 512
FFN_UP_STEPS = 4
FFN_DOWN_STEPS = 2
COMBINE_TN = 512
COMBINE_BUILD_ROWS = 256
SELECT_BISECTIONS = 64
SELECT_MIN_UPPER = 1e-30


def _params(*sem):
    return pltpu.CompilerParams(dimension_semantics=sem, vmem_limit_bytes=VMEM_LIMIT_BYTES)


def _ada_kernel(c_ref, w_ref, b_ref, o_ref):
    c = c_ref[...]
    c_act = c * jax.nn.sigmoid(c)
    o_ref[0] = jnp.dot(c_act, w_ref[0], precision=HIGHEST, preferred_element_type=F32) + b_ref[0]


def _ada(c_pad, w_ada, b_ada):
    depth, d, n = w_ada.shape
    rows = c_pad.shape[0]
    return pl.pallas_call(
        _ada_kernel,
        grid=(depth, n // ADA_TN),
        in_specs=[
            pl.BlockSpec((rows, d), lambda l, j: (0, 0)),
            pl.BlockSpec((1, d, ADA_TN), lambda l, j: (l, 0, j)),
            pl.BlockSpec((1, 1, ADA_TN), lambda l, j: (l, 0, j)),
        ],
        out_specs=pl.BlockSpec((1, rows, ADA_TN), lambda l, j: (l, 0, j)),
        out_shape=jax.ShapeDtypeStruct((depth, rows, n), F32),
        compiler_params=_params("arbitrary", "arbitrary"),
        name="ada_mod",
    )(c_pad, w_ada, b_ada.reshape(depth, 1, n))


def _inproj_kernel(x_ref, sc_ref, sh_ref, w_ref, wg_ref, cos_ref, sin_ref, qg_ref, kg_ref,
                   o_ref, g_ref, h_scr):
    j = pl.program_id(2)

    @pl.when(j == 0)
    def _():
        x = x_ref[0]
        r = lax.rsqrt(jnp.mean(x * x, axis=-1, keepdims=True) + EPS)
        h = (x * r) * (1.0 + sc_ref[0]) + sh_ref[0]
        hb = h.astype(BF16)
        h_scr[...] = hb
        g_ref[0] = jnp.dot(hb, wg_ref[...], preferred_element_type=F32)

    y = jnp.dot(h_scr[...], w_ref[...], preferred_element_type=F32)

    def norm_rope(t, gain):
        tn = t * lax.rsqrt(jnp.mean(t * t, axis=-1, keepdims=True) + EPS) * gain
        return tn * cos_ref[...] + pltpu.roll(tn, ATT_HEAD_DIM // 2, 1) * sin_ref[...]

    heads_per_tile = INPROJ_TN // ATT_HEAD_DIM
    q_tiles = ATT_WIDTH // INPROJ_TN
    kv_tile = COL_AK // INPROJ_TN
    scale = ATT_HEAD_DIM ** -0.5

    @pl.when(j < q_tiles)
    def _():
        for u in range(heads_per_tile):
            sl = slice(u * ATT_HEAD_DIM, (u + 1) * ATT_HEAD_DIM)
            o_ref[0, :, sl] = (norm_rope(y[:, sl], qg_ref[...]) * scale).astype(BF16)

    @pl.when(j == kv_tile)
    def _():
        for u in range(ATT_KV_HEADS):
            sl = slice(u * ATT_HEAD_DIM, (u + 1) * ATT_HEAD_DIM)
            o_ref[0, :, sl] = norm_rope(y[:, sl], kg_ref[...]).astype(BF16)
        o_ref[0, :, ATT_KV_WIDTH:] = y[:, ATT_KV_WIDTH:].astype(BF16)

    @pl.when(jnp.logical_and(j >= q_tiles, j != kv_tile))
    def _():
        o_ref[0] = y.astype(BF16)


def _inproj(x, mod6, w_main, w_gate, cos, sin_signed, q_gain, k_gain):
    b, s, d = x.shape
    tm, tn = INPROJ_TM, INPROJ_TN
    assert COL_AK % tn == 0 and ATT_WIDTH % tn == 0 and 2 * ATT_KV_WIDTH == tn
    return pl.pallas_call(
        _inproj_kernel,
        grid=(b, s // tm, PROJ_WIDTH // tn),
        in_specs=[
            pl.BlockSpec((1, tm, d), lambda bi, i, j: (bi, i, 0)),
            pl.BlockSpec((1, 1, d), lambda bi, i, j: (bi * 6 + 1, 0, 0)),
            pl.BlockSpec((1, 1, d), lambda bi, i, j: (bi * 6 + 0, 0, 0)),
            pl.BlockSpec((d, tn), lambda bi, i, j: (0, j)),
            pl.BlockSpec((d, N_GATES), lambda bi, i, j: (0, 0)),
            pl.BlockSpec((tm, ATT_HEAD_DIM), lambda bi, i, j: (i, 0)),
            pl.BlockSpec((tm, ATT_HEAD_DIM), lambda bi, i, j: (i, 0)),
            pl.BlockSpec((1, ATT_HEAD_DIM), lambda bi, i, j: (0, 0)),
            pl.BlockSpec((1, ATT_HEAD_DIM), lambda bi, i, j: (0, 0)),
        ],
        out_specs=[
            pl.BlockSpec((1, tm, tn), lambda bi, i, j: (bi, i, j)),
            pl.BlockSpec((1, tm, N_GATES), lambda bi, i, j: (bi, i, 0)),
        ],
        out_shape=[
            jax.ShapeDtypeStruct((b, s, PROJ_WIDTH), BF16),
            jax.ShapeDtypeStruct((b, s, N_GATES), F32),
        ],
        scratch_shapes=[pltpu.VMEM((tm, d), BF16)],
        compiler_params=_params("arbitrary", "arbitrary", "arbitrary"),
        name="in_proj",
    )(x, mod6, mod6, w_main, w_gate, cos, sin_signed, q_gain, k_gain)


def _attn_kernel(sink_ref, q_ref, kl_ref, kc_ref, kr_ref, vl_ref, vc_ref, vr_ref, o_ref, *, n_blocks):
    n = pl.program_id(1)
    L = ATT_BLOCK
    rows = ATT_GROUP * L
    i = lax.broadcasted_iota(jnp.int32, (rows, 3 * L), 0) & (L - 1)
    jj = lax.broadcasted_iota(jnp.int32, (rows, 3 * L), 1)
    dist = jj - L - i
    valid = (jnp.abs(dist) <= WINDOW) & ((jj >= L) | (n > 0)) & ((jj < 2 * L) | (n < n_blocks - 1))
    rgrp = lax.broadcasted_iota(jnp.int32, (rows, 1), 0) // L
    for kv in range(ATT_KV_HEADS):
        hs = slice(kv * ATT_HEAD_DIM, (kv + 1) * ATT_HEAD_DIM)
        q = jnp.concatenate(
            [q_ref[0, :, (kv * ATT_GROUP + g) * ATT_HEAD_DIM:(kv * ATT_GROUP + g + 1) * ATT_HEAD_DIM]
             for g in range(ATT_GROUP)], axis=0)
        kb = jnp.concatenate([kl_ref[0, :, hs], kc_ref[0, :, hs], kr_ref[0, :, hs]], axis=0)
        vb = jnp.concatenate([vl_ref[0, :, hs], vc_ref[0, :, hs], vr_ref[0, :, hs]], axis=0)
        s = lax.dot_general(q, kb, (((1,), (1,)), ((), ())), preferred_element_type=F32)
        s = jnp.where(valid, s, NEG)
        sink = jnp.zeros((rows, 1), F32)
        for g in range(ATT_GROUP):
            sink = jnp.where(rgrp == g, sink_ref[kv * ATT_GROUP + g], sink)
        m = jnp.maximum(jnp.max(s, axis=-1, keepdims=True), sink)
        p = jnp.exp(s - m)
        denom = jnp.sum(p, axis=-1, keepdims=True) + jnp.exp(sink - m)
        o = jnp.dot(p.astype(BF16), vb, preferred_element_type=F32) / denom
        for g in range(ATT_GROUP):
            h = kv * ATT_GROUP + g
            o_ref[0, :, h * ATT_HEAD_DIM:(h + 1) * ATT_HEAD_DIM] = o[g * L:(g + 1) * L].astype(BF16)


def _attention(proj, sink):
    b, s, _ = proj.shape
    L = ATT_BLOCK
    nb = s // L
    kblk = COL_AK // ATT_KV_WIDTH
    vblk = COL_AV // ATT_KV_WIDTH
    left = lambda n: jnp.maximum(n - 1, 0)
    right = lambda n: jnp.minimum(n + 1, nb - 1)
    kv_spec = lambda col, f: pl.BlockSpec((1, L, ATT_KV_WIDTH), lambda bi, n: (bi, f(n), col))
    same = lambda n: n
    return pl.pallas_call(
        functools.partial(_attn_kernel, n_blocks=nb),
        grid=(b, nb),
        in_specs=[
            pl.BlockSpec(memory_space=pltpu.SMEM),
            pl.BlockSpec((1, L, ATT_WIDTH), lambda bi, n: (bi, n, COL_AQ // ATT_WIDTH)),
            kv_spec(kblk, left), kv_spec(kblk, same), kv_spec(kblk, right),
            kv_spec(vblk, left), kv_spec(vblk, same), kv_spec(vblk, right),
        ],
        out_specs=pl.BlockSpec((1, L, ATT_WIDTH), lambda bi, n: (bi, n, 0)),
        out_shape=jax.ShapeDtypeStruct((b, s, ATT_WIDTH), BF16),
        compiler_params=_params("arbitrary", "arbitrary"),
        name="window_attn",
    )(sink, proj, proj, proj, proj, proj, proj, proj)


def _mlstm_kernel(q_ref, k_ref, v_ref, mo_ref, gcol_ref, grow_ref, bcol_ref, brow_ref, gain_ref, o_ref,
                  bc_scr, kt_scr, ar_scr, wr_scr, bt_scr, c_scr, h_scr, *, seq):
    L = MLSTM_BLOCK
    nc = seq // L
    ns = 2 * M_HEADS
    scale = M_QK_DIM ** -0.5
    row = lax.broadcasted_iota(jnp.int32, (L, L), 0)
    col = lax.broadcasted_iota(jnp.int32, (L, L), 1)
    lower = col <= row
    upper = col >= row
    tril = lower.astype(F32)
    triu = upper.astype(F32)

    fwd_lane = lax.broadcasted_iota(jnp.int32, (1, N_GATES), 1) < ns + M_HEADS
    for c in range(nc):
        rows = slice(c * L, (c + 1) * L)
        lf = jax.nn.log_sigmoid(gcol_ref[0, rows, :] + bcol_ref[...])
        pre = jnp.dot(tril, lf, precision=HIGHEST, preferred_element_type=F32)
        suf = jnp.dot(triu, lf, precision=HIGHEST, preferred_element_type=F32)
        bcol = jnp.where(fwd_lane, pre, suf)
        for k in range(ns):
            bc_scr[k, rows, :] = jnp.broadcast_to(bcol[:, ns + k:ns + k + 1], (L, 128))
        kt_scr[:, rows] = k_ref[0, rows, :].astype(F32).T.astype(BF16)

    gr = grow_ref[0] + brow_ref[...]
    ig_r = gr[:ns].reshape(ns * nc, L)
    lf_r = jax.nn.log_sigmoid(gr[ns:]).reshape(ns * nc, L)
    pre_r = jnp.dot(lf_r, triu, precision=HIGHEST, preferred_element_type=F32)
    suf_r = jnp.dot(lf_r, tril, precision=HIGHEST, preferred_element_type=F32)
    fwd_rows = lax.broadcasted_iota(jnp.int32, (ns * nc, 1), 0) < M_HEADS * nc
    a_r = ig_r - jnp.where(fwd_rows, pre_r, suf_r)
    btot = jnp.sum(lf_r, axis=-1, keepdims=True)
    ar_scr[...] = a_r.reshape(ns, nc, L)
    wr_scr[...] = (btot + a_r).reshape(ns, nc, L)
    bt_scr[...] = jnp.broadcast_to(btot, (ns * nc, 128)).reshape(ns, nc, 128)

    c_scr[...] = jnp.zeros_like(c_scr)
    h_scr[...] = jnp.zeros_like(h_scr)
    ones_cols = jnp.ones((L, 128), BF16)

    twice = lambda t: jnp.concatenate([t, t], axis=1)

    def chunk_step(k, c, m_st):
        d, h = divmod(k, M_HEADS)
        r0 = pl.multiple_of(c * L, L)
        q = q_ref[0, pl.ds(r0, L), h * M_QK_DIM:(h + 1) * M_QK_DIM]
        kk = k_ref[0, pl.ds(r0, L), h * M_QK_DIM:(h + 1) * M_QK_DIM]
        v_ext = jnp.concatenate([v_ref[0, pl.ds(r0, L), h * M_V_DIM:(h + 1) * M_V_DIM], ones_cols], axis=1)
        bc = bc_scr[k, pl.ds(r0, L), :]
        ar = ar_scr[k, pl.ds(c, 1), :]
        wr = wr_scr[k, pl.ds(c, 1), :]
        bt = bt_scr[k, pl.ds(c, 1), :]
        dm = jnp.where(lower if d == 0 else upper, twice(bc) + ar, NEG)
        g_inter = bc + m_st
        m_t = jnp.maximum(jnp.broadcast_to(jnp.max(dm, axis=-1, keepdims=True), (L, 128)), g_inter)
        e_inter = jnp.exp(g_inter - m_t) * scale
        s_qk = lax.dot_general(q, kk, (((1,), (1,)), ((), ())), preferred_element_type=F32)
        p = s_qk * scale * jnp.exp(dm - twice(m_t))
        qc = jnp.dot(q, c_scr[k].astype(BF16), preferred_element_type=F32)
        pv = jnp.dot(p.astype(BF16), v_ext, preferred_element_type=F32)
        num = pv[:, :M_V_DIM] + twice(e_inter) * qc[:, :M_V_DIM]
        den = pv[:, M_V_DIM:] + e_inter * qc[:, M_V_DIM:]
        inv = 1.0 / jnp.maximum(jnp.abs(den), jnp.exp(-m_t))
        hsl = (pl.ds(r0, L), slice(h * M_V_DIM, (h + 1) * M_V_DIM))
        h_scr[hsl] = h_scr[hsl] + num * twice(inv)
        m_new = jnp.maximum(bt + m_st, jnp.broadcast_to(jnp.max(wr, axis=-1, keepdims=True), (1, 128)))
        a = jnp.exp(bt + m_st - m_new)
        kt = kt_scr[h * M_QK_DIM:(h + 1) * M_QK_DIM, pl.ds(r0, L)]
        ek_t = (kt.astype(F32) * jnp.exp(wr - twice(m_new))).astype(BF16)
        c_scr[k] = jnp.concatenate([a, a, a], axis=1) * c_scr[k] + jnp.dot(ek_t, v_ext, preferred_element_type=F32)
        return m_new

    def body(it, ms):
        out = []
        for k in range(ns):
            c = it if k < M_HEADS else nc - 1 - it
            out.append(chunk_step(k, c, ms[k]))
        return tuple(out)

    lax.fori_loop(0, nc, body, tuple(jnp.zeros((1, 128), F32) for _ in range(ns)))

    def fin(i, carry):
        r0 = pl.multiple_of(i * L, L)
        for h in range(M_HEADS):
            sl = slice(h * M_V_DIM, (h + 1) * M_V_DIM)
            x = h_scr[pl.ds(r0, L), sl]
            y = x * lax.rsqrt(jnp.mean(x * x, axis=-1, keepdims=True) + EPS) * gain_ref[:, sl]
            o_ref[0, pl.ds(r0, L), sl] = (y * jax.nn.sigmoid(mo_ref[0, pl.ds(r0, L), sl].astype(F32))).astype(BF16)
        return carry

    lax.fori_loop(0, nc, fin, 0)


def _mlstm(proj, gates, b_gates, m_gain):
    b, s, _ = proj.shape
    L = MLSTM_BLOCK
    nc = s // L
    ns = 2 * M_HEADS
    grow = jnp.transpose(gates, (0, 2, 1)).reshape(b, N_GATES, nc, L)
    col = lambda width, off: pl.BlockSpec((1, s, width), lambda bi: (bi, 0, off // width))
    return pl.pallas_call(
        functools.partial(_mlstm_kernel, seq=s),
        grid=(b,),
        in_specs=[
            col(M_QK_WIDTH, COL_MQ), col(M_QK_WIDTH, COL_MK), col(M_WIDTH, COL_MV),
            pl.BlockSpec((1, s, M_WIDTH), lambda bi: (bi, 0, COL_MO // M_WIDTH), pipeline_mode=pl.Buffered(1)),
            pl.BlockSpec((1, s, N_GATES), lambda bi: (bi, 0, 0)),
            pl.BlockSpec((1, N_GATES, nc, L), lambda bi: (bi, 0, 0, 0)),
            pl.BlockSpec((1, N_GATES), lambda bi: (0, 0)),
            pl.BlockSpec((N_GATES, 1, 1), lambda bi: (0, 0, 0)),
            pl.BlockSpec((1, M_WIDTH), lambda bi: (0, 0)),
        ],
        out_specs=pl.BlockSpec((1, s, M_WIDTH), lambda bi: (bi, 0, 0)),
        out_shape=jax.ShapeDtypeStruct((b, s, M_WIDTH), BF16),
        scratch_shapes=[
            pltpu.VMEM((ns, s, 128), F32),
            pltpu.VMEM((M_QK_WIDTH, s), BF16),
            pltpu.VMEM((ns, nc, L), F32), pltpu.VMEM((ns, nc, L), F32), pltpu.VMEM((ns, nc, 128), F32),
            pltpu.VMEM((ns, M_QK_DIM, M_V_DIM + 128), F32),
            pltpu.VMEM((s, M_WIDTH), F32),
        ],
        compiler_params=_params("arbitrary"),
        name="mlstm",
    )(proj, proj, proj, proj, gates, grow, b_gates.reshape(1, N_GATES), b_gates.reshape(N_GATES, 1, 1),
      m_gain.reshape(1, M_WIDTH))


def _outproj_kernel(att_ref, mo_ref, wa_ref, wm_ref, x_ref, g1_ref, sc_ref, sh_ref, wr_ref,
                    x1_ref, h2_ref, aff_ref):
    j = pl.program_id(2)
    nj = pl.num_programs(2)
    tn = OUTPROJ_TN
    mix = (jnp.dot(att_ref[0], wa_ref[...], preferred_element_type=F32)
           + jnp.dot(mo_ref[0], wm_ref[...], preferred_element_type=F32))
    x1t = x_ref[0] + g1_ref[0] * mix
    for jj in range(D_MODEL // tn):
        @pl.when(j == jj)
        def _(jj=jj):
            x1_ref[0, :, jj * tn:(jj + 1) * tn] = x1t

    @pl.when(j == nj - 1)
    def _():
        xf = x1_ref[0]
        r = lax.rsqrt(jnp.mean(xf * xf, axis=-1, keepdims=True) + EPS)
        h2 = (xf * r) * (1.0 + sc_ref[0]) + sh_ref[0]
        h2_hi = h2.astype(BF16)
        h2_ref[0] = h2
        h2_lo = (h2 - h2_hi.astype(F32)).astype(BF16)
        hi_terms = jnp.dot(h2_hi, wr_ref[...], preferred_element_type=F32)
        lo_term = jnp.dot(h2_lo, wr_ref[:, :N_EXPERTS], preferred_element_type=F32)
        logits = hi_terms[:, :N_EXPERTS] + hi_terms[:, N_EXPERTS:] + lo_term
        e = jnp.exp(logits - jnp.max(logits, axis=-1, keepdims=True))
        aff_ref[0] = e / jnp.sum(e, axis=-1, keepdims=True)


def _outproj(att, m_out, w_out, x, mod6, w_router):
    b, s, d = x.shape
    tm, tn = OUTPROJ_TM, OUTPROJ_TN
    half = ATT_WIDTH
    assert w_out.shape[0] == 2 * half and M_WIDTH == half
    wr_hi = w_router.astype(BF16)
    wr_lo = (w_router - wr_hi.astype(F32)).astype(BF16)
    return pl.pallas_call(
        _outproj_kernel,
        grid=(b, s // tm, d // tn),
        in_specs=[
            pl.BlockSpec((1, tm, ATT_WIDTH), lambda bi, i, j: (bi, i, 0)),
            pl.BlockSpec((1, tm, M_WIDTH), lambda bi, i, j: (bi, i, 0)),
            pl.BlockSpec((half, tn), lambda bi, i, j: (0, j)),
            pl.BlockSpec((half, tn), lambda bi, i, j: (1, j)),
            pl.BlockSpec((1, tm, tn), lambda bi, i, j: (bi, i, j)),
            pl.BlockSpec((1, 1, tn), lambda bi, i, j: (bi * 6 + 2, 0, j)),
            pl.BlockSpec((1, 1, d), lambda bi, i, j: (bi * 6 + 4, 0, 0)),
            pl.BlockSpec((1, 1, d), lambda bi, i, j: (bi * 6 + 3, 0, 0)),
            pl.BlockSpec((d, 2 * N_EXPERTS), lambda bi, i, j: (0, 0)),
        ],
        out_specs=[
            pl.BlockSpec((1, tm, d), lambda bi, i, j: (bi, i, 0)),
            pl.BlockSpec((1, tm, d), lambda bi, i, j: (bi, i, 0)),
            pl.BlockSpec((1, tm, N_EXPERTS), lambda bi, i, j: (bi, i, 0)),
        ],
        out_shape=[
            jax.ShapeDtypeStruct((b, s, d), F32),
            jax.ShapeDtypeStruct((b, s, d), F32),
            jax.ShapeDtypeStruct((b, s, N_EXPERTS), F32),
        ],
        compiler_params=_params("arbitrary", "arbitrary", "arbitrary"),
        name="out_proj",
    )(att, m_out, w_out, w_out, x, mod6, mod6, mod6, jnp.concatenate([wr_hi, wr_lo], axis=1))


def _prefix_sum_lanes(x, n):
    lane = lax.broadcasted_iota(jnp.int32, x.shape, 1)
    sh = 1
    while sh < n:
        x = x + jnp.where(lane >= sh, pltpu.roll(x, sh, 1), 0.0)
        sh *= 2
    return x


def _select_kernel(aff_ref, pos_ref, gate_ref, *, cap, seq):
    a = aff_ref[...]
    ne = a.shape[0]

    def body(_, lohi):
        lo, hi = lohi
        mid = jnp.where(lo > 0.0, 0.5 * (lo + hi), hi * (1.0 / 256.0))
        cnt = jnp.sum(jnp.where(a >= mid, 1.0, 0.0), axis=-1, keepdims=True)
        ok = cnt >= cap
        return jnp.where(ok, mid, lo), jnp.where(ok, hi, mid)

    lo0 = jnp.zeros((ne, 1), F32)
    hi0 = jnp.maximum(2.0 * jnp.max(a, axis=-1, keepdims=True), SELECT_MIN_UPPER)
    lo, hi = lax.fori_loop(0, SELECT_BISECTIONS, body, (lo0, hi0))
    above = a >= hi
    band = jnp.logical_and(a >= lo, jnp.logical_not(above))
    n_above = jnp.sum(jnp.where(above, 1.0, 0.0), axis=-1, keepdims=True)
    band_rank = _prefix_sum_lanes(jnp.where(band, 1.0, 0.0), seq)
    sel = jnp.where(above, 1.0, jnp.where(band, jnp.where(band_rank <= cap - n_above, 1.0, 0.0), 0.0))
    pos = _prefix_sum_lanes(sel, seq) - 1.0
    chosen = sel > 0.5
    pos_ref[...] = jnp.where(chosen, pos, -1.0)
    gate_ref[...] = jnp.where(chosen, a, 0.0)


def _select(aff_rows, cap):
    n, s = aff_rows.shape
    spec = pl.BlockSpec((n, s), lambda i: (0, 0))
    return pl.pallas_call(
        functools.partial(_select_kernel, cap=cap, seq=s),
        grid=(1,),
        in_specs=[spec],
        out_specs=[spec, spec],
        out_shape=[jax.ShapeDtypeStruct((n, s), F32), jax.ShapeDtypeStruct((n, s), F32)],
        compiler_params=_params("arbitrary"),
        name="expert_select",
    )(aff_rows)


def _compact_kernel(pos_ref, idx_ref, *, cap):
    seq, ne = pos_ref.shape[1], pos_ref.shape[2]
    rt = COMBINE_BUILD_ROWS
    slot = lax.broadcasted_iota(jnp.int32, (rt, cap), 1).astype(F32)
    tok0 = lax.broadcasted_iota(jnp.int32, (rt, 1), 0).astype(F32)

    def body(i, accs):
        r0 = pl.multiple_of(i * rt, rt)
        pos = pos_ref[0, pl.ds(r0, rt), :]
        tok = tok0 + r0.astype(F32)
        return tuple(
            accs[e] + jnp.sum(jnp.where(slot == pos[:, e:e + 1], tok, 0.0), axis=0, keepdims=True)
            for e in range(ne))

    accs = lax.fori_loop(0, seq // rt, body, tuple(jnp.zeros((1, cap), F32) for _ in range(ne)))
    for e in range(ne):
        idx_ref[0, e:e + 1, :] = accs[e].astype(jnp.int32)


def _compact(pos_cols, cap):
    b, s, ne = pos_cols.shape
    return pl.pallas_call(
        functools.partial(_compact_kernel, cap=cap),
        grid=(b,),
        in_specs=[pl.BlockSpec((1, s, ne), lambda bi: (bi, 0, 0))],
        out_specs=pl.BlockSpec((1, ne, cap), lambda bi: (bi, 0, 0)),
        out_shape=jax.ShapeDtypeStruct((b, ne, cap), jnp.int32),
        compiler_params=_params("arbitrary"),
        name="expert_compact",
    )(pos_cols)


def _ffn_kernel(idx_ref, h2_hbm, wg_ref, wu_ref, wd_ref, ye_ref, x_scr, hid_scr, sem, *, nb, cap):
    e = pl.program_id(0)
    f = pl.program_id(1)
    ne = pl.num_programs(0)
    nf = pl.num_programs(1)
    rows = nb * cap
    per_step = rows // FFN_UP_STEPS
    tf = wg_ref.shape[-1]
    slot = lax.rem(e, 2)
    nslot = 1 - slot
    nxt = jnp.minimum(e + 1, ne - 1)

    def row_copy(expert, r, b, sl):
        tok = idx_ref[expert, r]
        return pltpu.make_async_copy(h2_hbm.at[b, pl.ds(tok, 1), :], x_scr.at[sl, pl.ds(r, 1), :], sem.at[sl])

    def wait_rows(sl):
        pltpu.make_async_copy(x_scr.at[sl], x_scr.at[sl], sem.at[sl]).wait()

    @pl.when(jnp.logical_and(e == 0, f == 0))
    def _():
        def first(r, carry):
            row_copy(0, r, r // cap, 0).start()
            return carry
        lax.fori_loop(0, rows, first, 0)

    for ff in range(FFN_UP_STEPS):
        @pl.when(f == ff)
        def _(ff=ff):
            if ff == 0:
                wait_rows(slot)
            for r in range(ff * per_step, (ff + 1) * per_step):
                row_copy(nxt, r, r // cap, nslot).start()
            x = x_scr[slot].astype(BF16)
            g = jnp.dot(x, wg_ref[0, 0].astype(BF16), preferred_element_type=F32)
            u = jnp.dot(x, wu_ref[0, 0].astype(BF16), preferred_element_type=F32)
            hid_scr[:, ff * tf:(ff + 1) * tf] = ((g * jax.nn.sigmoid(g)) * u).astype(BF16)

    @pl.when(f >= FFN_UP_STEPS)
    def _():
        y = jnp.dot(hid_scr[...], wd_ref[0, 0].astype(BF16), preferred_element_type=F32)
        ye_ref[...] = y.reshape(nb, 1, cap, y.shape[-1]).astype(BF16)

    @pl.when(jnp.logical_and(e == ne - 1, f == nf - 1))
    def _():
        wait_rows(nslot)


def _ffn(idx, h2, w_gate, w_up, w_down, layer):
    b, s, d = h2.shape
    ne = idx.shape[0]
    cap = idx.shape[1] // b
    ff = w_gate.shape[-1]
    tf = ff // FFN_UP_STEPS
    dn = d // FFN_DOWN_STEPS
    up = lambda f: jnp.minimum(f, FFN_UP_STEPS - 1)
    down = lambda f: jnp.maximum(f - FFN_UP_STEPS, 0)
    grid_spec = pltpu.PrefetchScalarGridSpec(
        num_scalar_prefetch=1,
        grid=(ne, FFN_UP_STEPS + FFN_DOWN_STEPS),
        in_specs=[
            pl.BlockSpec(memory_space=pl.ANY),
            pl.BlockSpec((1, 1, d, tf), lambda e, f, idx_ref: (layer, e, 0, up(f))),
            pl.BlockSpec((1, 1, d, tf), lambda e, f, idx_ref: (layer, e, 0, up(f))),
            pl.BlockSpec((1, 1, ff, dn), lambda e, f, idx_ref: (layer, e, 0, down(f))),
        ],
        out_specs=pl.BlockSpec((b, 1, cap, dn), lambda e, f, idx_ref: (0, e, 0, down(f))),
        scratch_shapes=[
            pltpu.VMEM((2, b * cap, d), F32),
            pltpu.VMEM((b * cap, ff), BF16),
            pltpu.SemaphoreType.DMA((2,)),
        ],
    )
    return pl.pallas_call(
        functools.partial(_ffn_kernel, nb=b, cap=cap),
        grid_spec=grid_spec,
        out_shape=jax.ShapeDtypeStruct((b, ne, cap, d), BF16),
        compiler_params=_params("arbitrary", "arbitrary"),
        name="expert_ffn",
    )(idx, h2, w_gate, w_up, w_down)


def _combine_kernel(pos_ref, gate_ref, ye_ref, x1_ref, g2_ref, o_ref, scat_scr, *, cap):
    j = pl.program_id(1)
    seq, ne = pos_ref.shape[1], pos_ref.shape[2]
    rt = COMBINE_BUILD_ROWS

    @pl.when(j == 0)
    def _():
        slot = lax.broadcasted_iota(jnp.int32, (rt, cap), 1).astype(F32)

        def build(i, carry):
            r0 = pl.multiple_of(i * rt, rt)
            pos = pos_ref[0, pl.ds(r0, rt), :]
            gate = gate_ref[0, pl.ds(r0, rt), :]
            for e in range(ne):
                scat_scr[pl.ds(r0, rt), e * cap:(e + 1) * cap] = jnp.where(
                    slot == pos[:, e:e + 1], gate[:, e:e + 1], 0.0).astype(BF16)
            return carry

        lax.fori_loop(0, seq // rt, build, 0)

    acc = jnp.dot(scat_scr[...], ye_ref[0], preferred_element_type=F32)
    o_ref[0] = x1_ref[0] + g2_ref[0] * acc


def _combine(pos_cols, gate_cols, ye, x1, mod6):
    b, s, d = x1.shape
    ne, cap = ye.shape[1], ye.shape[2]
    tn = COMBINE_TN
    return pl.pallas_call(
        functools.partial(_combine_kernel, cap=cap),
        grid=(b, d // tn),
        in_specs=[
            pl.BlockSpec((1, s, ne), lambda bi, j: (bi, 0, 0)),
            pl.BlockSpec((1, s, ne), lambda bi, j: (bi, 0, 0)),
            pl.BlockSpec((1, ne * cap, tn), lambda bi, j: (bi, 0, j)),
            pl.BlockSpec((1, s, tn), lambda bi, j: (bi, 0, j)),
            pl.BlockSpec((1, 1, tn), lambda bi, j: (bi * 6 + 5, 0, j)),
        ],
        out_specs=pl.BlockSpec((1, s, tn), lambda bi, j: (bi, 0, j)),
        out_shape=jax.ShapeDtypeStruct((b, s, d), F32),
        scratch_shapes=[pltpu.VMEM((s, ne * cap), BF16)],
        compiler_params=_params("arbitrary", "arbitrary"),
        name="expert_combine",
    )(pos_cols, gate_cols, ye.reshape(b, ne * cap, d), x1, mod6)


def _rope_tables(seq):
    inv = 1.0 / (ROPE_THETA ** (jnp.arange(0, ATT_HEAD_DIM, 2, dtype=F32) / ATT_HEAD_DIM))
    ang = jnp.arange(seq, dtype=F32)[:, None] * inv[None, :]
    ang = jnp.concatenate([ang, ang], axis=-1)
    sign = jnp.where(jnp.arange(ATT_HEAD_DIM) < ATT_HEAD_DIM // 2, -1.0, 1.0).astype(F32)
    return jnp.cos(ang), jnp.sin(ang) * sign


def _permute_in_cols(w):
    o_ak = ATT_WIDTH
    o_mq = o_ak + 2 * ATT_KV_WIDTH
    o_mv = o_mq + 2 * M_QK_WIDTH
    o_g = o_mv + 2 * M_WIDTH
    main = jnp.concatenate([w[:, :o_ak], w[:, o_mv:o_g], w[:, o_ak:o_mv]], axis=1)
    return main.astype(BF16), w[:, o_g:].astype(BF16)


def kernel(x, c, w_ada, b_ada, w_in, b_gates, q_gain, k_gain, sink, m_gain, w_out,
           w_router, w_gate, w_up, w_down):
    b, s, d = x.shape
    depth = w_ada.shape[0]
    cap = CAPACITY_FACTOR * s // N_EXPERTS
    cos, sin_signed = _rope_tables(s)
    c_pad = jnp.pad(c, ((0, 8 - b), (0, 0)))
    mod = _ada(c_pad, w_ada, b_ada)
    for l in range(depth):
        mod6 = mod[l, :b].reshape(b * 6, 1, d)
        w_main, w_g = _permute_in_cols(w_in[l])
        proj, gates = _inproj(x, mod6, w_main, w_g, cos, sin_signed,
                              q_gain[l].reshape(1, -1), k_gain[l].reshape(1, -1))
        att = _attention(proj, sink[l])
        m_out = _mlstm(proj, gates, b_gates[l].reshape(-1), m_gain[l])
        x1, h2, aff = _outproj(att, m_out, w_out[l].astype(BF16), x, mod6, w_router[l])
        pos_r, gate_r = _select(jnp.transpose(aff, (0, 2, 1)).reshape(b * N_EXPERTS, s), cap)
        to_cols = lambda t: jnp.transpose(t.reshape(b, N_EXPERTS, s), (0, 2, 1))
        pos_c = to_cols(pos_r)
        idx = jnp.transpose(_compact(pos_c, cap), (1, 0, 2)).reshape(N_EXPERTS, b * cap)
        ye = _ffn(idx, h2, w_gate, w_up, w_down, l)
        x = _combine(pos_c, to_cols(gate_r), ye, x1, mod6)
    return x
```

```python
import functools

import jax
import jax.numpy as jnp
from jax import lax
from jax.experimental import pallas as pl
from jax.experimental.pallas import tpu as pltpu

F32 = jnp.float32
BF16 = jnp.bfloat16
HIGHEST = lax.Precision.HIGHEST

D_MODEL = 2048
ATT_HEAD_DIM = 128
ATT_HEADS = 8
ATT_KV_HEADS = 2
ATT_GROUP = ATT_HEADS // ATT_KV_HEADS
ATT_WIDTH = ATT_HEADS * ATT_HEAD_DIM
ATT_KV_WIDTH = ATT_KV_HEADS * ATT_HEAD_DIM
WINDOW = 128
ATT_BLOCK = 128
ATT_STEP_BLOCKS = 4
ROPE_THETA = 10000.0
M_HEADS = 4
M_V_DIM = 256
M_QK_DIM = 128
M_WIDTH = M_HEADS * M_V_DIM
M_QK_WIDTH = M_HEADS * M_QK_DIM
MLSTM_BLOCK = 256
N_GATES = 4 * M_HEADS
N_EXPERTS = 16
EXPERT_FF = D_MODEL // 2
CAPACITY_FACTOR = 2
EPS = 1e-6
NEG = -1e30

COL_AQ = 0
COL_MV = ATT_WIDTH
COL_MO = COL_MV + M_WIDTH
COL_AK = COL_MO + M_WIDTH
COL_AV = COL_AK + ATT_KV_WIDTH
COL_MQ = COL_AV + ATT_KV_WIDTH
COL_MK = COL_MQ + M_QK_WIDTH
PROJ_WIDTH = COL_MK + M_QK_WIDTH
IN_COL_BLOCKS = (0, 1, 5, 6, 7, 8, 2, 3, 4)

VMEM_LIMIT_BYTES = 56 * 1024 * 1024

ADA_TN = 1024
INPROJ_TM = 1024
INPROJ_TN = 512
OUTPROJ_TM = 512
OUTPROJ_TN = 1024
FFN_UP_STEPS = 4
FFN_DOWN_STEPS = 2
COMBINE_TN = 512
COMBINE_BUILD_ROWS = 256
SELECT_BISECTIONS = 64
SELECT_MIN_UPPER = 1e-30


def _params(*sem):
    return pltpu.CompilerParams(dimension_semantics=sem, vmem_limit_bytes=VMEM_LIMIT_BYTES)


def _ada_kernel(c_ref, w_ref, b_ref, o_ref):
    c = c_ref[...]
    c_act = c * jax.nn.sigmoid(c)
    w = w_ref[0]
    w_hi = w.astype(BF16)
    w_lo = (w - w_hi.astype(F32)).astype(BF16)
    c_hi = c_act.astype(BF16)
    c_lo = (c_act - c_hi.astype(F32)).astype(BF16)
    acc = jnp.dot(c_hi, w_hi, preferred_element_type=F32)
    acc = acc + jnp.dot(c_lo, w_hi, preferred_element_type=F32)
    acc = acc + jnp.dot(c_hi, w_lo, preferred_element_type=F32)
    o_ref[0] = acc + b_ref[0]


def _ada(c_pad, w_ada, b_ada):
    depth, d, n = w_ada.shape
    rows = c_pad.shape[0]
    return pl.pallas_call(
        _ada_kernel,
        grid=(depth, n // ADA_TN),
        in_specs=[
            pl.BlockSpec((rows, d), lambda l, j: (0, 0)),
            pl.BlockSpec((1, d, ADA_TN), lambda l, j: (l, 0, j)),
            pl.BlockSpec((1, 1, ADA_TN), lambda l, j: (l, 0, j)),
        ],
        out_specs=pl.BlockSpec((1, rows, ADA_TN), lambda l, j: (l, 0, j)),
        out_shape=jax.ShapeDtypeStruct((depth, rows, n), F32),
        compiler_params=_params("arbitrary", "arbitrary"),
        name="ada_mod",
    )(c_pad, w_ada, b_ada.reshape(depth, 1, n))


def _inproj_kernel(perm_ref, x_ref, sc_ref, sh_ref, w_ref, wg_ref, cos_ref, sin_ref, qg_ref, kg_ref,
                   o_ref, g_ref, h_scr):
    del perm_ref
    j = pl.program_id(2)

    @pl.when(j == 0)
    def _():
        x = x_ref[0]
        r = lax.rsqrt(jnp.mean(x * x, axis=-1, keepdims=True) + EPS)
        h = (x * r) * (1.0 + sc_ref[0]) + sh_ref[0]
        hb = h.astype(BF16)
        h_scr[...] = hb
        g_ref[0] = jnp.dot(hb, wg_ref[0, :, :N_GATES].astype(BF16), preferred_element_type=F32)

    y = jnp.dot(h_scr[...], w_ref[0].astype(BF16), preferred_element_type=F32)

    def norm_rope(t, gain):
        tn = t * lax.rsqrt(jnp.mean(t * t, axis=-1, keepdims=True) + EPS) * gain
        return tn * cos_ref[...] + pltpu.roll(tn, ATT_HEAD_DIM // 2, 1) * sin_ref[...]

    heads_per_tile = INPROJ_TN // ATT_HEAD_DIM
    q_tiles = ATT_WIDTH // INPROJ_TN
    kv_tile = COL_AK // INPROJ_TN
    scale = ATT_HEAD_DIM ** -0.5

    @pl.when(j < q_tiles)
    def _():
        for u in range(heads_per_tile):
            sl = slice(u * ATT_HEAD_DIM, (u + 1) * ATT_HEAD_DIM)
            o_ref[0, :, sl] = (norm_rope(y[:, sl], qg_ref[...]) * scale).astype(BF16)

    @pl.when(j == kv_tile)
    def _():
        for u in range(ATT_KV_HEADS):
            sl = slice(u * ATT_HEAD_DIM, (u + 1) * ATT_HEAD_DIM)
            o_ref[0, :, sl] = norm_rope(y[:, sl], kg_ref[...]).astype(BF16)
        o_ref[0, :, ATT_KV_WIDTH:] = y[:, ATT_KV_WIDTH:].astype(BF16)

    @pl.when(jnp.logical_and(j >= q_tiles, j != kv_tile))
    def _():
        o_ref[0] = y.astype(BF16)


def _inproj(x, mod6, w_in, layer, cos, sin_signed, q_gain, k_gain):
    b, s, d = x.shape
    tm, tn = INPROJ_TM, INPROJ_TN
    assert COL_AK % tn == 0 and ATT_WIDTH % tn == 0 and 2 * ATT_KV_WIDTH == tn
    gate_blk = PROJ_WIDTH // 128
    grid_spec = pltpu.PrefetchScalarGridSpec(
        num_scalar_prefetch=1,
        grid=(b, s // tm, PROJ_WIDTH // tn),
        in_specs=[
            pl.BlockSpec((1, tm, d), lambda bi, i, j, perm: (bi, i, 0)),
            pl.BlockSpec((1, 1, d), lambda bi, i, j, perm: (bi * 6 + 1, 0, 0)),
            pl.BlockSpec((1, 1, d), lambda bi, i, j, perm: (bi * 6 + 0, 0, 0)),
            pl.BlockSpec((1, d, tn), lambda bi, i, j, perm: (layer, 0, perm[j])),
            pl.BlockSpec((1, d, 128), lambda bi, i, j, perm: (layer, 0, gate_blk)),
            pl.BlockSpec((tm, ATT_HEAD_DIM), lambda bi, i, j, perm: (i, 0)),
            pl.BlockSpec((tm, ATT_HEAD_DIM), lambda bi, i, j, perm: (i, 0)),
            pl.BlockSpec((1, ATT_HEAD_DIM), lambda bi, i, j, perm: (0, 0)),
            pl.BlockSpec((1, ATT_HEAD_DIM), lambda bi, i, j, perm: (0, 0)),
        ],
        out_specs=[
            pl.BlockSpec((1, tm, tn), lambda bi, i, j, perm: (bi, i, j)),
            pl.BlockSpec((1, tm, N_GATES), lambda bi, i, j, perm: (bi, i, 0)),
        ],
        scratch_shapes=[pltpu.VMEM((tm, d), BF16)],
    )
    return pl.pallas_call(
        _inproj_kernel,
        grid_spec=grid_spec,
        out_shape=[
            jax.ShapeDtypeStruct((b, s, PROJ_WIDTH), BF16),
            jax.ShapeDtypeStruct((b, s, N_GATES), F32),
        ],
        compiler_params=_params("arbitrary", "arbitrary", "arbitrary"),
        name="in_proj",
    )(jnp.asarray(IN_COL_BLOCKS, jnp.int32), x, mod6, mod6, w_in, w_in, cos, sin_signed, q_gain, k_gain)


def _attn_kernel(sink_ref, q_ref, kl_ref, kc_ref, kr_ref, vl_ref, vc_ref, vr_ref, o_ref, *, n_steps):
    n = pl.program_id(1)
    L = ATT_BLOCK
    rows = ATT_GROUP * L
    i = lax.broadcasted_iota(jnp.int32, (rows, 3 * L), 0) & (L - 1)
    jj = lax.broadcasted_iota(jnp.int32, (rows, 3 * L), 1)
    in_window = jnp.abs(jj - L - i) <= WINDOW
    first_mask = in_window & ((jj >= L) | (n > 0))
    last_mask = in_window & ((jj < 2 * L) | (n < n_steps - 1))
    rgrp = lax.broadcasted_iota(jnp.int32, (rows, 1), 0) // L
    for kv in range(ATT_KV_HEADS):
        hs = slice(kv * ATT_HEAD_DIM, (kv + 1) * ATT_HEAD_DIM)
        kband = jnp.concatenate([kl_ref[0, :, hs], kc_ref[0, :, hs], kr_ref[0, :, hs]], axis=0)
        vband = jnp.concatenate([vl_ref[0, :, hs], vc_ref[0, :, hs], vr_ref[0, :, hs]], axis=0)
        sink = jnp.zeros((rows, 1), F32)
        for g in range(ATT_GROUP):
            sink = jnp.where(rgrp == g, sink_ref[kv * ATT_GROUP + g], sink)
        for blk in range(ATT_STEP_BLOCKS):
            qrows = slice(blk * L, (blk + 1) * L)
            q = jnp.concatenate(
                [q_ref[0, qrows, (kv * ATT_GROUP + g) * ATT_HEAD_DIM:(kv * ATT_GROUP + g + 1) * ATT_HEAD_DIM]
                 for g in range(ATT_GROUP)], axis=0)
            kb = kband[blk * L:(blk + 3) * L]
            vb = vband[blk * L:(blk + 3) * L]
            valid = first_mask if blk == 0 else (last_mask if blk == ATT_STEP_BLOCKS - 1 else in_window)
            s = lax.dot_general(q, kb, (((1,), (1,)), ((), ())), preferred_element_type=F32)
            s = jnp.where(valid, s, NEG)
            m = jnp.maximum(jnp.max(s, axis=-1, keepdims=True), sink)
            p = jnp.exp(s - m)
            denom = jnp.sum(p, axis=-1, keepdims=True) + jnp.exp(sink - m)
            o = jnp.dot(p.astype(BF16), vb, preferred_element_type=F32) / denom
            for g in range(ATT_GROUP):
                h = kv * ATT_GROUP + g
                o_ref[0, qrows, h * ATT_HEAD_DIM:(h + 1) * ATT_HEAD_DIM] = o[g * L:(g + 1) * L].astype(BF16)


def _attention(proj, sink):
    b, s, _ = proj.shape
    L = ATT_BLOCK
    nb = s // L
    sb = ATT_STEP_BLOCKS
    assert sb >= 2 and nb % sb == 0
    kblk = COL_AK // ATT_KV_WIDTH
    vblk = COL_AV // ATT_KV_WIDTH
    left = lambda col: pl.BlockSpec((1, L, ATT_KV_WIDTH), lambda bi, n: (bi, jnp.maximum(n * sb - 1, 0), col))
    right = lambda col: pl.BlockSpec((1, L, ATT_KV_WIDTH), lambda bi, n: (bi, jnp.minimum((n + 1) * sb, nb - 1), col))
    centre = lambda col: pl.BlockSpec((1, sb * L, ATT_KV_WIDTH), lambda bi, n: (bi, n, col))
    return pl.pallas_call(
        functools.partial(_attn_kernel, n_steps=nb // sb),
        grid=(b, nb // sb),
        in_specs=[
            pl.BlockSpec(memory_space=pltpu.SMEM),
            pl.BlockSpec((1, sb * L, ATT_WIDTH), lambda bi, n: (bi, n, COL_AQ // ATT_WIDTH)),
            left(kblk), centre(kblk), right(kblk),
            left(vblk), centre(vblk), right(vblk),
        ],
        out_specs=pl.BlockSpec((1, sb * L, ATT_WIDTH), lambda bi, n: (bi, n, 0)),
        out_shape=jax.ShapeDtypeStruct((b, s, ATT_WIDTH), BF16),
        compiler_params=_params("arbitrary", "arbitrary"),
        name="window_attn",
    )(sink, proj, proj, proj, proj, proj, proj, proj)


def _mlstm_kernel(q_ref, k_ref, v_ref, mo_ref, gcol_ref, grow_ref, bcol_ref, brow_ref, gain_ref, o_ref,
                  bc_scr, kt_scr, ar_scr, wr_scr, bt_scr, c_scr, h_scr, *, seq):
    L = MLSTM_BLOCK
    nc = seq // L
    ns = 2 * M_HEADS
    scale = M_QK_DIM ** -0.5
    row = lax.broadcasted_iota(jnp.int32, (L, L), 0)
    col = lax.broadcasted_iota(jnp.int32, (L, L), 1)
    lower = col <= row
    upper = col >= row
    tril = lower.astype(F32)
    triu = upper.astype(F32)

    fwd_lane = lax.broadcasted_iota(jnp.int32, (1, N_GATES), 1) < ns + M_HEADS
    for c in range(nc):
        rows = slice(c * L, (c + 1) * L)
        lf = jax.nn.log_sigmoid(gcol_ref[0, rows, :] + bcol_ref[...])
        pre = jnp.dot(tril, lf, precision=HIGHEST, preferred_element_type=F32)
        suf = jnp.dot(triu, lf, precision=HIGHEST, preferred_element_type=F32)
        bcol = jnp.where(fwd_lane, pre, suf)
        for k in range(ns):
            bc_scr[k, rows, :] = jnp.broadcast_to(bcol[:, ns + k:ns + k + 1], (L, 128))
        kt_scr[:, rows] = k_ref[0, rows, :].astype(F32).T.astype(BF16)

    gr = grow_ref[0] + brow_ref[...]
    ig_r = gr[:ns].reshape(ns * nc, L)
    lf_r = jax.nn.log_sigmoid(gr[ns:]).reshape(ns * nc, L)
    pre_r = jnp.dot(lf_r, triu, precision=HIGHEST, preferred_element_type=F32)
    suf_r = jnp.dot(lf_r, tril, precision=HIGHEST, preferred_element_type=F32)
    fwd_rows = lax.broadcasted_iota(jnp.int32, (ns * nc, 1), 0) < M_HEADS * nc
    a_r = ig_r - jnp.where(fwd_rows, pre_r, suf_r)
    btot = jnp.sum(lf_r, axis=-1, keepdims=True)
    ar_scr[...] = a_r.reshape(ns, nc, L)
    wr_scr[...] = (btot + a_r).reshape(ns, nc, L)
    bt_scr[...] = jnp.broadcast_to(btot, (ns * nc, 128)).reshape(ns, nc, 128)

    c_scr[...] = jnp.zeros_like(c_scr)
    h_scr[...] = jnp.zeros_like(h_scr)
    ones_cols = jnp.ones((L, 128), BF16)

    twice = lambda t: jnp.concatenate([t, t], axis=1)

    def chunk_step(k, c, m_st):
        d, h = divmod(k, M_HEADS)
        r0 = pl.multiple_of(c * L, L)
        q = q_ref[0, pl.ds(r0, L), h * M_QK_DIM:(h + 1) * M_QK_DIM]
        kk = k_ref[0, pl.ds(r0, L), h * M_QK_DIM:(h + 1) * M_QK_DIM]
        v_ext = jnp.concatenate([v_ref[0, pl.ds(r0, L), h * M_V_DIM:(h + 1) * M_V_DIM], ones_cols], axis=1)
        bc = bc_scr[k, pl.ds(r0, L), :]
        ar = ar_scr[k, pl.ds(c, 1), :]
        wr = wr_scr[k, pl.ds(c, 1), :]
        bt = bt_scr[k, pl.ds(c, 1), :]
        dm = jnp.where(lower if d == 0 else upper, twice(bc) + ar, NEG)
        g_inter = bc + m_st
        m_t = jnp.maximum(jnp.broadcast_to(jnp.max(dm, axis=-1, keepdims=True), (L, 128)), g_inter)
        e_inter = jnp.exp(g_inter - m_t) * scale
        s_qk = lax.dot_general(q, kk, (((1,), (1,)), ((), ())), preferred_element_type=F32)
        p = s_qk * scale * jnp.exp(dm - twice(m_t))
        qc = jnp.dot(q, c_scr[k].astype(BF16), preferred_element_type=F32)
        pv = jnp.dot(p.astype(BF16), v_ext, preferred_element_type=F32)
        num = pv[:, :M_V_DIM] + twice(e_inter) * qc[:, :M_V_DIM]
        den = pv[:, M_V_DIM:] + e_inter * qc[:, M_V_DIM:]
        inv = 1.0 / jnp.maximum(jnp.abs(den), jnp.exp(-m_t))
        hsl = (pl.ds(r0, L), slice(h * M_V_DIM, (h + 1) * M_V_DIM))
        h_scr[hsl] = h_scr[hsl] + num * twice(inv)
        m_new = jnp.maximum(bt + m_st, jnp.broadcast_to(jnp.max(wr, axis=-1, keepdims=True), (1, 128)))
        a = jnp.exp(bt + m_st - m_new)
        kt = kt_scr[h * M_QK_DIM:(h + 1) * M_QK_DIM, pl.ds(r0, L)]
        ek_t = (kt.astype(F32) * jnp.exp(wr - twice(m_new))).astype(BF16)
        c_scr[k] = jnp.concatenate([a, a, a], axis=1) * c_scr[k] + jnp.dot(ek_t, v_ext, preferred_element_type=F32)
        return m_new

    def body(it, ms):
        out = []
        for k in range(ns):
            c = it if k < M_HEADS else nc - 1 - it
            out.append(chunk_step(k, c, ms[k]))
        return tuple(out)

    lax.fori_loop(0, nc, body, tuple(jnp.zeros((1, 128), F32) for _ in range(ns)))

    def fin(i, carry):
        r0 = pl.multiple_of(i * L, L)
        for h in range(M_HEADS):
            sl = slice(h * M_V_DIM, (h + 1) * M_V_DIM)
            x = h_scr[pl.ds(r0, L), sl]
            y = x * lax.rsqrt(jnp.mean(x * x, axis=-1, keepdims=True) + EPS) * gain_ref[:, sl]
            o_ref[0, pl.ds(r0, L), sl] = (y * jax.nn.sigmoid(mo_ref[0, pl.ds(r0, L), sl].astype(F32))).astype(BF16)
        return carry

    lax.fori_loop(0, nc, fin, 0)


def _mlstm(proj, gates, b_gates, m_gain):
    b, s, _ = proj.shape
    L = MLSTM_BLOCK
    nc = s // L
    ns = 2 * M_HEADS
    grow = jnp.transpose(gates, (0, 2, 1)).reshape(b, N_GATES, nc, L)
    col = lambda width, off: pl.BlockSpec((1, s, width), lambda bi: (bi, 0, off // width))
    return pl.pallas_call(
        functools.partial(_mlstm_kernel, seq=s),
        grid=(b,),
        in_specs=[
            col(M_QK_WIDTH, COL_MQ), col(M_QK_WIDTH, COL_MK), col(M_WIDTH, COL_MV),
            pl.BlockSpec((1, s, M_WIDTH), lambda bi: (bi, 0, COL_MO // M_WIDTH), pipeline_mode=pl.Buffered(1)),
            pl.BlockSpec((1, s, N_GATES), lambda bi: (bi, 0, 0)),
            pl.BlockSpec((1, N_GATES, nc, L), lambda bi: (bi, 0, 0, 0)),
            pl.BlockSpec((1, N_GATES), lambda bi: (0, 0)),
            pl.BlockSpec((N_GATES, 1, 1), lambda bi: (0, 0, 0)),
            pl.BlockSpec((1, M_WIDTH), lambda bi: (0, 0)),
        ],
        out_specs=pl.BlockSpec((1, s, M_WIDTH), lambda bi: (bi, 0, 0)),
        out_shape=jax.ShapeDtypeStruct((b, s, M_WIDTH), BF16),
        scratch_shapes=[
            pltpu.VMEM((ns, s, 128), F32),
            pltpu.VMEM((M_QK_WIDTH, s), BF16),
            pltpu.VMEM((ns, nc, L), F32), pltpu.VMEM((ns, nc, L), F32), pltpu.VMEM((ns, nc, 128), F32),
            pltpu.VMEM((ns, M_QK_DIM, M_V_DIM + 128), F32),
            pltpu.VMEM((s, M_WIDTH), F32),
        ],
        compiler_params=_params("arbitrary"),
        name="mlstm",
    )(proj, proj, proj, proj, gates, grow, b_gates.reshape(1, N_GATES), b_gates.reshape(N_GATES, 1, 1),
      m_gain.reshape(1, M_WIDTH))


def _outproj_kernel(att_ref, mo_ref, wa_ref, wm_ref, x_ref, g1_ref, sc_ref, sh_ref, wr_ref,
                    x1_ref, h2_ref, aff_ref):
    j = pl.program_id(2)
    nj = pl.num_programs(2)
    tn = OUTPROJ_TN
    mix = (jnp.dot(att_ref[0], wa_ref[...], preferred_element_type=F32)
           + jnp.dot(mo_ref[0], wm_ref[...], preferred_element_type=F32))
    x1t = x_ref[0] + g1_ref[0] * mix
    for jj in range(D_MODEL // tn):
        @pl.when(j == jj)
        def _(jj=jj):
            x1_ref[0, :, jj * tn:(jj + 1) * tn] = x1t

    @pl.when(j == nj - 1)
    def _():
        xf = x1_ref[0]
        r = lax.rsqrt(jnp.mean(xf * xf, axis=-1, keepdims=True) + EPS)
        h2 = (xf * r) * (1.0 + sc_ref[0]) + sh_ref[0]
        h2_hi = h2.astype(BF16)
        h2_ref[0] = h2
        h2_lo = (h2 - h2_hi.astype(F32)).astype(BF16)
        hi_terms = jnp.dot(h2_hi, wr_ref[...], preferred_element_type=F32)
        lo_term = jnp.dot(h2_lo, wr_ref[:, :N_EXPERTS], preferred_element_type=F32)
        logits = hi_terms[:, :N_EXPERTS] + hi_terms[:, N_EXPERTS:] + lo_term
        e = jnp.exp(logits - jnp.max(logits, axis=-1, keepdims=True))
        aff_ref[0] = e / jnp.sum(e, axis=-1, keepdims=True)


def _outproj(att, m_out, w_out, x, mod6, w_router):
    b, s, d = x.shape
    tm, tn = OUTPROJ_TM, OUTPROJ_TN
    half = ATT_WIDTH
    assert w_out.shape[0] == 2 * half and M_WIDTH == half
    wr_hi = w_router.astype(BF16)
    wr_lo = (w_router - wr_hi.astype(F32)).astype(BF16)
    return pl.pallas_call(
        _outproj_kernel,
        grid=(b, s // tm, d // tn),
        in_specs=[
            pl.BlockSpec((1, tm, ATT_WIDTH), lambda bi, i, j: (bi, i, 0)),
            pl.BlockSpec((1, tm, M_WIDTH), lambda bi, i, j: (bi, i, 0)),
            pl.BlockSpec((half, tn), lambda bi, i, j: (0, j)),
            pl.BlockSpec((half, tn), lambda bi, i, j: (1, j)),
            pl.BlockSpec((1, tm, tn), lambda bi, i, j: (bi, i, j)),
            pl.BlockSpec((1, 1, tn), lambda bi, i, j: (bi * 6 + 2, 0, j)),
            pl.BlockSpec((1, 1, d), lambda bi, i, j: (bi * 6 + 4, 0, 0)),
            pl.BlockSpec((1, 1, d), lambda bi, i, j: (bi * 6 + 3, 0, 0)),
            pl.BlockSpec((d, 2 * N_EXPERTS), lambda bi, i, j: (0, 0)),
        ],
        out_specs=[
            pl.BlockSpec((1, tm, d), lambda bi, i, j: (bi, i, 0)),
            pl.BlockSpec((1, tm, d), lambda bi, i, j: (bi, i, 0)),
            pl.BlockSpec((1, tm, N_EXPERTS), lambda bi, i, j: (bi, i, 0)),
        ],
        out_shape=[
            jax.ShapeDtypeStruct((b, s, d), F32),
            jax.ShapeDtypeStruct((b, s, d), F32),
            jax.ShapeDtypeStruct((b, s, N_EXPERTS), F32),
        ],
        compiler_params=_params("arbitrary", "arbitrary", "arbitrary"),
        name="out_proj",
    )(att, m_out, w_out, w_out, x, mod6, mod6, mod6, jnp.concatenate([wr_hi, wr_lo], axis=1))


def _prefix_sum_lanes(x, n):
    lane = lax.broadcasted_iota(jnp.int32, x.shape, 1)
    sh = 1
    while sh < n:
        x = x + jnp.where(lane >= sh, pltpu.roll(x, sh, 1), 0.0)
        sh *= 2
    return x


def _select_kernel(aff_ref, pos_ref, gate_ref, *, cap, seq):
    a = aff_ref[...]
    ne = a.shape[0]

    def body(_, lohi):
        lo, hi = lohi
        mid = jnp.where(lo > 0.0, 0.5 * (lo + hi), hi * (1.0 / 256.0))
        cnt = jnp.sum(jnp.where(a >= mid, 1.0, 0.0), axis=-1, keepdims=True)
        ok = cnt >= cap
        return jnp.where(ok, mid, lo), jnp.where(ok, hi, mid)

    lo0 = jnp.zeros((ne, 1), F32)
    hi0 = jnp.maximum(2.0 * jnp.max(a, axis=-1, keepdims=True), SELECT_MIN_UPPER)
    lo, hi = lax.fori_loop(0, SELECT_BISECTIONS, body, (lo0, hi0))
    above = a >= hi
    band = jnp.logical_and(a >= lo, jnp.logical_not(above))
    n_above = jnp.sum(jnp.where(above, 1.0, 0.0), axis=-1, keepdims=True)
    band_rank = _prefix_sum_lanes(jnp.where(band, 1.0, 0.0), seq)
    sel = jnp.where(above, 1.0, jnp.where(band, jnp.where(band_rank <= cap - n_above, 1.0, 0.0), 0.0))
    pos = _prefix_sum_lanes(sel, seq) - 1.0
    chosen = sel > 0.5
    pos_ref[...] = jnp.where(chosen, pos, -1.0)
    gate_ref[...] = jnp.where(chosen, a, 0.0)


def _select(aff_rows, cap):
    n, s = aff_rows.shape
    spec = pl.BlockSpec((n, s), lambda i: (0, 0))
    return pl.pallas_call(
        functools.partial(_select_kernel, cap=cap, seq=s),
        grid=(1,),
        in_specs=[spec],
        out_specs=[spec, spec],
        out_shape=[jax.ShapeDtypeStruct((n, s), F32), jax.ShapeDtypeStruct((n, s), F32)],
        compiler_params=_params("arbitrary"),
        name="expert_select",
    )(aff_rows)


def _compact_kernel(pos_ref, idx_ref, *, cap):
    seq, ne = pos_ref.shape[1], pos_ref.shape[2]
    rt = COMBINE_BUILD_ROWS
    slot = lax.broadcasted_iota(jnp.int32, (rt, cap), 1).astype(F32)
    tok0 = lax.broadcasted_iota(jnp.int32, (rt, 1), 0).astype(F32)

    def body(i, accs):
        r0 = pl.multiple_of(i * rt, rt)
        pos = pos_ref[0, pl.ds(r0, rt), :]
        tok = tok0 + r0.astype(F32)
        return tuple(
            accs[e] + jnp.sum(jnp.where(slot == pos[:, e:e + 1], tok, 0.0), axis=0, keepdims=True)
            for e in range(ne))

    accs = lax.fori_loop(0, seq // rt, body, tuple(jnp.zeros((1, cap), F32) for _ in range(ne)))
    for e in range(ne):
        idx_ref[0, e:e + 1, :] = accs[e].astype(jnp.int32)


def _compact(pos_cols, cap):
    b, s, ne = pos_cols.shape
    return pl.pallas_call(
        functools.partial(_compact_kernel, cap=cap),
        grid=(b,),
        in_specs=[pl.BlockSpec((1, s, ne), lambda bi: (bi, 0, 0))],
        out_specs=pl.BlockSpec((1, ne, cap), lambda bi: (bi, 0, 0)),
        out_shape=jax.ShapeDtypeStruct((b, ne, cap), jnp.int32),
        compiler_params=_params("arbitrary"),
        name="expert_compact",
    )(pos_cols)


def _ffn_kernel(idx_ref, h2_hbm, wg_ref, wu_ref, wd_ref, ye_ref, x_scr, hid_scr, sem, *, nb, cap):
    e = pl.program_id(0)
    f = pl.program_id(1)
    ne = pl.num_programs(0)
    nf = pl.num_programs(1)
    rows = nb * cap
    per_step = rows // FFN_UP_STEPS
    tf = wg_ref.shape[-1]
    slot = lax.rem(e, 2)
    nslot = 1 - slot
    nxt = jnp.minimum(e + 1, ne - 1)

    def row_copy(expert, r, b, sl):
        tok = idx_ref[expert, r]
        return pltpu.make_async_copy(h2_hbm.at[b, pl.ds(tok, 1), :], x_scr.at[sl, pl.ds(r, 1), :], sem.at[sl])

    def wait_rows(sl):
        pltpu.make_async_copy(x_scr.at[sl], x_scr.at[sl], sem.at[sl]).wait()

    @pl.when(jnp.logical_and(e == 0, f == 0))
    def _():
        def first(r, carry):
            row_copy(0, r, r // cap, 0).start()
            return carry
        lax.fori_loop(0, rows, first, 0)

    for ff in range(FFN_UP_STEPS):
        @pl.when(f == ff)
        def _(ff=ff):
            if ff == 0:
                wait_rows(slot)
            for r in range(ff * per_step, (ff + 1) * per_step):
                row_copy(nxt, r, r // cap, nslot).start(priority=r % 2)
            x = x_scr[slot].astype(BF16)
            g = jnp.dot(x, wg_ref[0, 0].astype(BF16), preferred_element_type=F32)
            u = jnp.dot(x, wu_ref[0, 0].astype(BF16), preferred_element_type=F32)
            hid_scr[:, ff * tf:(ff + 1) * tf] = ((g * jax.nn.sigmoid(g)) * u).astype(BF16)

    @pl.when(f >= FFN_UP_STEPS)
    def _():
        y = jnp.dot(hid_scr[...], wd_ref[0, 0].astype(BF16), preferred_element_type=F32)
        ye_ref[...] = y.reshape(nb, 1, cap, y.shape[-1]).astype(BF16)

    @pl.when(jnp.logical_and(e == ne - 1, f == nf - 1))
    def _():
        wait_rows(nslot)


def _ffn(idx, h2, w_gate, w_up, w_down, layer):
    b, s, d = h2.shape
    ne = idx.shape[0]
    cap = idx.shape[1] // b
    ff = w_gate.shape[-1]
    tf = ff // FFN_UP_STEPS
    dn = d // FFN_DOWN_STEPS
    up = lambda f: jnp.minimum(f, FFN_UP_STEPS - 1)
    down = lambda f: jnp.maximum(f - FFN_UP_STEPS, 0)
    grid_spec = pltpu.PrefetchScalarGridSpec(
        num_scalar_prefetch=1,
        grid=(ne, FFN_UP_STEPS + FFN_DOWN_STEPS),
        in_specs=[
            pl.BlockSpec(memory_space=pl.ANY),
            pl.BlockSpec((1, 1, d, tf), lambda e, f, idx_ref: (layer, e, 0, up(f))),
            pl.BlockSpec((1, 1, d, tf), lambda e, f, idx_ref: (layer, e, 0, up(f))),
            pl.BlockSpec((1, 1, ff, dn), lambda e, f, idx_ref: (layer, e, 0, down(f))),
        ],
        out_specs=pl.BlockSpec((b, 1, cap, dn), lambda e, f, idx_ref: (0, e, 0, down(f))),
        scratch_shapes=[
            pltpu.VMEM((2, b * cap, d), F32),
            pltpu.VMEM((b * cap, ff), BF16),
            pltpu.SemaphoreType.DMA((2,)),
        ],
    )
    return pl.pallas_call(
        functools.partial(_ffn_kernel, nb=b, cap=cap),
        grid_spec=grid_spec,
        out_shape=jax.ShapeDtypeStruct((b, ne, cap, d), BF16),
        compiler_params=_params("arbitrary", "arbitrary"),
        name="expert_ffn",
    )(idx, h2, w_gate, w_up, w_down)


def _combine_kernel(pos_ref, gate_ref, ye_ref, x1_ref, g2_ref, o_ref, scat_scr, *, cap):
    j = pl.program_id(1)
    seq, ne = pos_ref.shape[1], pos_ref.shape[2]
    rt = COMBINE_BUILD_ROWS

    @pl.when(j == 0)
    def _():
        slot = lax.broadcasted_iota(jnp.int32, (rt, cap), 1).astype(F32)

        def build(i, carry):
            r0 = pl.multiple_of(i * rt, rt)
            pos = pos_ref[0, pl.ds(r0, rt), :]
            gate = gate_ref[0, pl.ds(r0, rt), :]
            for e in range(ne):
                scat_scr[pl.ds(r0, rt), e * cap:(e + 1) * cap] = jnp.where(
                    slot == pos[:, e:e + 1], gate[:, e:e + 1], 0.0).astype(BF16)
            return carry

        lax.fori_loop(0, seq // rt, build, 0)

    acc = jnp.dot(scat_scr[...], ye_ref[0], preferred_element_type=F32)
    o_ref[0] = x1_ref[0] + g2_ref[0] * acc


def _combine(pos_cols, gate_cols, ye, x1, mod6):
    b, s, d = x1.shape
    ne, cap = ye.shape[1], ye.shape[2]
    tn = COMBINE_TN
    return pl.pallas_call(
        functools.partial(_combine_kernel, cap=cap),
        grid=(b, d // tn),
        in_specs=[
            pl.BlockSpec((1, s, ne), lambda bi, j: (bi, 0, 0)),
            pl.BlockSpec((1, s, ne), lambda bi, j: (bi, 0, 0)),
            pl.BlockSpec((1, ne * cap, tn), lambda bi, j: (bi, 0, j)),
            pl.BlockSpec((1, s, tn), lambda bi, j: (bi, 0, j)),
            pl.BlockSpec((1, 1, tn), lambda bi, j: (bi * 6 + 5, 0, j)),
        ],
        out_specs=pl.BlockSpec((1, s, tn), lambda bi, j: (bi, 0, j)),
        out_shape=jax.ShapeDtypeStruct((b, s, d), F32),
        scratch_shapes=[pltpu.VMEM((s, ne * cap), BF16)],
        compiler_params=_params("arbitrary", "arbitrary"),
        name="expert_combine",
    )(pos_cols, gate_cols, ye.reshape(b, ne * cap, d), x1, mod6)


def _rope_tables(seq):
    inv = 1.0 / (ROPE_THETA ** (jnp.arange(0, ATT_HEAD_DIM, 2, dtype=F32) / ATT_HEAD_DIM))
    ang = jnp.arange(seq, dtype=F32)[:, None] * inv[None, :]
    ang = jnp.concatenate([ang, ang], axis=-1)
    sign = jnp.where(jnp.arange(ATT_HEAD_DIM) < ATT_HEAD_DIM // 2, -1.0, 1.0).astype(F32)
    return jnp.cos(ang), jnp.sin(ang) * sign


def kernel(x, c, w_ada, b_ada, w_in, b_gates, q_gain, k_gain, sink, m_gain, w_out,
           w_router, w_gate, w_up, w_down):
    b, s, d = x.shape
    depth = w_ada.shape[0]
    cap = CAPACITY_FACTOR * s // N_EXPERTS
    cos, sin_signed = _rope_tables(s)
    c_pad = jnp.pad(c, ((0, 8 - b), (0, 0)))
    mod = _ada(c_pad, w_ada, b_ada)
    for l in range(depth):
        mod6 = mod[l, :b].reshape(b * 6, 1, d)
        proj, gates = _inproj(x, mod6, w_in, l, cos, sin_signed,
                              q_gain[l].reshape(1, -1), k_gain[l].reshape(1, -1))
        att = _attention(proj, sink[l])
        m_out = _mlstm(proj, gates, b_gates[l].reshape(-1), m_gain[l])
        x1, h2, aff = _outproj(att, m_out, w_out[l].astype(BF16), x, mod6, w_router[l])
        pos_r, gate_r = _select(jnp.transpose(aff, (0, 2, 1)).reshape(b * N_EXPERTS, s), cap)
        to_cols = lambda t: jnp.transpose(t.reshape(b, N_EXPERTS, s), (0, 2, 1))
        pos_c = to_cols(pos_r)
        idx = jnp.transpose(_compact(pos_c, cap), (1, 0, 2)).reshape(N_EXPERTS, b * cap)
        ye = _ffn(idx, h2, w_gate, w_up, w_down, l)
        x = _combine(pos_c, to_cols(gate_r), ye, x1, mod6)
    return x
```

```python
import functools

import jax
import jax.numpy as jnp
from jax import lax
from jax.experimental import pallas as pl
from jax.experimental.pallas import tpu as pltpu

F32 = jnp.float32
BF16 = jnp.bfloat16
HIGHEST = lax.Precision.HIGHEST

D_MODEL = 2048
ATT_HEAD_DIM = 128
ATT_HEADS = 8
ATT_KV_HEADS = 2
ATT_GROUP = ATT_HEADS // ATT_KV_HEADS
ATT_WIDTH = ATT_HEADS * ATT_HEAD_DIM
ATT_KV_WIDTH = ATT_KV_HEADS * ATT_HEAD_DIM
WINDOW = 128
ATT_BLOCK = 128
ATT_STEP_BLOCKS = 4
ROPE_THETA = 10000.0
M_HEADS = 4
M_V_DIM = 256
M_QK_DIM = 128
M_WIDTH = M_HEADS * M_V_DIM
M_QK_WIDTH = M_HEADS * M_QK_DIM
MLSTM_BLOCK = 256
N_GATES = 4 * M_HEADS
N_EXPERTS = 16
EXPERT_FF = D_MODEL // 2
CAPACITY_FACTOR = 2
EPS = 1e-6
NEG = -1e30

COL_AQ = 0
COL_MV = ATT_WIDTH
COL_MO = COL_MV + M_WIDTH
COL_AK = COL_MO + M_WIDTH
COL_AV = COL_AK + ATT_KV_WIDTH
COL_MQ = COL_AV + ATT_KV_WIDTH
COL_MK = COL_MQ + M_QK_WIDTH
PROJ_WIDTH = COL_MK + M_QK_WIDTH
IN_COL_BLOCKS = (0, 1, 5, 6, 7, 8, 2, 3, 4)

VMEM_LIMIT_BYTES = 56 * 1024 * 1024

ADA_TN = 1024
INPROJ_TM = 1024
INPROJ_TN = 512
OUTPROJ_TM = 1024
OUTPROJ_TN = 512
FFN_UP_STEPS = 4
FFN_DOWN_STEPS = 2
COMBINE_TN = 512
COMBINE_BUILD_ROWS = 256
SELECT_BISECTIONS = 64
SELECT_MIN_UPPER = 1e-30


def _params(*sem):
    return pltpu.CompilerParams(dimension_semantics=sem, vmem_limit_bytes=VMEM_LIMIT_BYTES)


def _ada_kernel(c_ref, w_ref, b_ref, o_ref):
    c = c_ref[...]
    c_act = c * jax.nn.sigmoid(c)
    w = w_ref[0]
    w_hi = w.astype(BF16)
    w_lo = (w - w_hi.astype(F32)).astype(BF16)
    c_hi = c_act.astype(BF16)
    c_lo = (c_act - c_hi.astype(F32)).astype(BF16)
    acc = jnp.dot(c_hi, w_hi, preferred_element_type=F32)
    acc = acc + jnp.dot(c_lo, w_hi, preferred_element_type=F32)
    acc = acc + jnp.dot(c_hi, w_lo, preferred_element_type=F32)
    o_ref[0] = acc + b_ref[0]


def _ada(c_pad, w_ada, b_ada):
    depth, d, n = w_ada.shape
    rows = c_pad.shape[0]
    return pl.pallas_call(
        _ada_kernel,
        grid=(depth, n // ADA_TN),
        in_specs=[
            pl.BlockSpec((rows, d), lambda l, j: (0, 0)),
            pl.BlockSpec((1, d, ADA_TN), lambda l, j: (l, 0, j)),
            pl.BlockSpec((1, 1, ADA_TN), lambda l, j: (l, 0, j)),
        ],
        out_specs=pl.BlockSpec((1, rows, ADA_TN), lambda l, j: (l, 0, j)),
        out_shape=jax.ShapeDtypeStruct((depth, rows, n), F32),
        compiler_params=_params("arbitrary", "arbitrary"),
        name="ada_mod",
    )(c_pad, w_ada, b_ada.reshape(depth, 1, n))


def _inproj_kernel(perm_ref, x_ref, sc_ref, sh_ref, w_ref, wg_ref, cos_ref, sin_ref, qg_ref, kg_ref,
                   o_ref, g_ref, h_scr):
    del perm_ref
    j = pl.program_id(2)

    @pl.when(j == 0)
    def _():
        x = x_ref[0]
        r = lax.rsqrt(jnp.mean(x * x, axis=-1, keepdims=True) + EPS)
        h = (x * r) * (1.0 + sc_ref[0]) + sh_ref[0]
        hb = h.astype(BF16)
        h_scr[...] = hb
        g_ref[0] = jnp.dot(hb, wg_ref[...], preferred_element_type=F32)

    y = jnp.dot(h_scr[...], w_ref[...], preferred_element_type=F32)

    def norm_rope(t, gain):
        tn = t * lax.rsqrt(jnp.mean(t * t, axis=-1, keepdims=True) + EPS) * gain
        return tn * cos_ref[...] + pltpu.roll(tn, ATT_HEAD_DIM // 2, 1) * sin_ref[...]

    heads_per_tile = INPROJ_TN // ATT_HEAD_DIM
    q_tiles = ATT_WIDTH // INPROJ_TN
    kv_tile = COL_AK // INPROJ_TN
    scale = ATT_HEAD_DIM ** -0.5

    @pl.when(j < q_tiles)
    def _():
        for u in range(heads_per_tile):
            sl = slice(u * ATT_HEAD_DIM, (u + 1) * ATT_HEAD_DIM)
            o_ref[0, :, sl] = (norm_rope(y[:, sl], qg_ref[...]) * scale).astype(BF16)

    @pl.when(j == kv_tile)
    def _():
        for u in range(ATT_KV_HEADS):
            sl = slice(u * ATT_HEAD_DIM, (u + 1) * ATT_HEAD_DIM)
            o_ref[0, :, sl] = norm_rope(y[:, sl], kg_ref[...]).astype(BF16)
        o_ref[0, :, ATT_KV_WIDTH:] = y[:, ATT_KV_WIDTH:].astype(BF16)

    @pl.when(jnp.logical_and(j >= q_tiles, j != kv_tile))
    def _():
        o_ref[0] = y.astype(BF16)


def _inproj(x, mod6, w_in, layer, cos, sin_signed, q_gain, k_gain):
    b, s, d = x.shape
    tm, tn = INPROJ_TM, INPROJ_TN
    assert COL_AK % tn == 0 and ATT_WIDTH % tn == 0 and 2 * ATT_KV_WIDTH == tn
    w_main = w_in[layer, :, :PROJ_WIDTH].astype(BF16)
    w_gate = w_in[layer, :, PROJ_WIDTH:].astype(BF16)
    grid_spec = pltpu.PrefetchScalarGridSpec(
        num_scalar_prefetch=1,
        grid=(b, s // tm, PROJ_WIDTH // tn),
        in_specs=[
            pl.BlockSpec((1, tm, d), lambda bi, i, j, perm: (bi, i, 0)),
            pl.BlockSpec((1, 1, d), lambda bi, i, j, perm: (bi * 6 + 1, 0, 0)),
            pl.BlockSpec((1, 1, d), lambda bi, i, j, perm: (bi * 6 + 0, 0, 0)),
            pl.BlockSpec((d, tn), lambda bi, i, j, perm: (0, perm[j])),
            pl.BlockSpec((d, N_GATES), lambda bi, i, j, perm: (0, 0)),
            pl.BlockSpec((tm, ATT_HEAD_DIM), lambda bi, i, j, perm: (i, 0)),
            pl.BlockSpec((tm, ATT_HEAD_DIM), lambda bi, i, j, perm: (i, 0)),
            pl.BlockSpec((1, ATT_HEAD_DIM), lambda bi, i, j, perm: (0, 0)),
            pl.BlockSpec((1, ATT_HEAD_DIM), lambda bi, i, j, perm: (0, 0)),
        ],
        out_specs=[
            pl.BlockSpec((1, tm, tn), lambda bi, i, j, perm: (bi, i, j)),
            pl.BlockSpec((1, tm, N_GATES), lambda bi, i, j, perm: (bi, i, 0)),
        ],
        scratch_shapes=[pltpu.VMEM((tm, d), BF16)],
    )
    return pl.pallas_call(
        _inproj_kernel,
        grid_spec=grid_spec,
        out_shape=[
            jax.ShapeDtypeStruct((b, s, PROJ_WIDTH), BF16),
            jax.ShapeDtypeStruct((b, s, N_GATES), F32),
        ],
        compiler_params=_params("arbitrary", "arbitrary", "arbitrary"),
        name="in_proj",
    )(jnp.asarray(IN_COL_BLOCKS, jnp.int32), x, mod6, mod6, w_main, w_gate, cos, sin_signed, q_gain, k_gain)


def _attn_kernel(sink_ref, q_ref, kl_ref, kc_ref, kr_ref, vl_ref, vc_ref, vr_ref, o_ref, *, n_steps):
    n = pl.program_id(1)
    L = ATT_BLOCK
    rows = ATT_GROUP * L
    i = lax.broadcasted_iota(jnp.int32, (rows, 3 * L), 0) & (L - 1)
    jj = lax.broadcasted_iota(jnp.int32, (rows, 3 * L), 1)
    in_window = jnp.abs(jj - L - i) <= WINDOW
    first_mask = in_window & ((jj >= L) | (n > 0))
    last_mask = in_window & ((jj < 2 * L) | (n < n_steps - 1))
    rgrp = lax.broadcasted_iota(jnp.int32, (rows, 1), 0) // L
    for kv in range(ATT_KV_HEADS):
        hs = slice(kv * ATT_HEAD_DIM, (kv + 1) * ATT_HEAD_DIM)
        kband = jnp.concatenate([kl_ref[0, :, hs], kc_ref[0, :, hs], kr_ref[0, :, hs]], axis=0)
        vband = jnp.concatenate([vl_ref[0, :, hs], vc_ref[0, :, hs], vr_ref[0, :, hs]], axis=0)
        sink = jnp.zeros((rows, 1), F32)
        for g in range(ATT_GROUP):
            sink = jnp.where(rgrp == g, sink_ref[kv * ATT_GROUP + g], sink)
        for blk in range(ATT_STEP_BLOCKS):
            qrows = slice(blk * L, (blk + 1) * L)
            q = jnp.concatenate(
                [q_ref[0, qrows, (kv * ATT_GROUP + g) * ATT_HEAD_DIM:(kv * ATT_GROUP + g + 1) * ATT_HEAD_DIM]
                 for g in range(ATT_GROUP)], axis=0)
            kb = kband[blk * L:(blk + 3) * L]
            vb = vband[blk * L:(blk + 3) * L]
            valid = first_mask if blk == 0 else (last_mask if blk == ATT_STEP_BLOCKS - 1 else in_window)
            s = lax.dot_general(q, kb, (((1,), (1,)), ((), ())), preferred_element_type=F32)
            s = jnp.where(valid, s, NEG)
            m = jnp.maximum(jnp.max(s, axis=-1, keepdims=True), sink)
            p = jnp.exp(s - m)
            denom = jnp.sum(p, axis=-1, keepdims=True) + jnp.exp(sink - m)
            o = jnp.dot(p.astype(BF16), vb, preferred_element_type=F32) / denom
            for g in range(ATT_GROUP):
                h = kv * ATT_GROUP + g
                o_ref[0, qrows, h * ATT_HEAD_DIM:(h + 1) * ATT_HEAD_DIM] = o[g * L:(g + 1) * L].astype(BF16)


def _attention(proj, sink):
    b, s, _ = proj.shape
    L = ATT_BLOCK
    nb = s // L
    sb = ATT_STEP_BLOCKS
    assert sb >= 2 and nb % sb == 0
    kblk = COL_AK // ATT_KV_WIDTH
    vblk = COL_AV // ATT_KV_WIDTH
    left = lambda col: pl.BlockSpec((1, L, ATT_KV_WIDTH), lambda bi, n: (bi, jnp.maximum(n * sb - 1, 0), col))
    right = lambda col: pl.BlockSpec((1, L, ATT_KV_WIDTH), lambda bi, n: (bi, jnp.minimum((n + 1) * sb, nb - 1), col))
    centre = lambda col: pl.BlockSpec((1, sb * L, ATT_KV_WIDTH), lambda bi, n: (bi, n, col))
    return pl.pallas_call(
        functools.partial(_attn_kernel, n_steps=nb // sb),
        grid=(b, nb // sb),
        in_specs=[
            pl.BlockSpec(memory_space=pltpu.SMEM),
            pl.BlockSpec((1, sb * L, ATT_WIDTH), lambda bi, n: (bi, n, COL_AQ // ATT_WIDTH)),
            left(kblk), centre(kblk), right(kblk),
            left(vblk), centre(vblk), right(vblk),
        ],
        out_specs=pl.BlockSpec((1, sb * L, ATT_WIDTH), lambda bi, n: (bi, n, 0)),
        out_shape=jax.ShapeDtypeStruct((b, s, ATT_WIDTH), BF16),
        compiler_params=_params("arbitrary", "arbitrary"),
        name="window_attn",
    )(sink, proj, proj, proj, proj, proj, proj, proj)


def _split3(x):
    hi = x.astype(BF16)
    rest = x - hi.astype(F32)
    mid = rest.astype(BF16)
    return hi, mid, (rest - mid.astype(F32)).astype(BF16)


def _mlstm_kernel(q_ref, k_ref, v_ref, mo_ref, gcol_ref, grow_ref, bcol_ref, brow_ref, gain_ref, o_ref,
                  bc_scr, kt_scr, ar_scr, wr_scr, bt_scr, c_scr, h_scr, *, seq):
    L = MLSTM_BLOCK
    nc = seq // L
    ns = 2 * M_HEADS
    scale = M_QK_DIM ** -0.5
    row = lax.broadcasted_iota(jnp.int32, (L, L), 0)
    col = lax.broadcasted_iota(jnp.int32, (L, L), 1)
    lower = col <= row
    upper = col >= row
    tril = jnp.where(lower, 1.0, 0.0).astype(BF16)
    triu = jnp.where(upper, 1.0, 0.0).astype(BF16)

    fwd_lane = lax.broadcasted_iota(jnp.int32, (1, N_GATES), 1) < ns + M_HEADS
    for c in range(nc):
        rows = slice(c * L, (c + 1) * L)
        lf = jax.nn.log_sigmoid(gcol_ref[0, rows, :] + bcol_ref[...])
        lf3 = _split3(lf)
        pre = sum(jnp.dot(tril, part, preferred_element_type=F32) for part in lf3)
        suf = sum(jnp.dot(triu, part, preferred_element_type=F32) for part in lf3)
        bcol = jnp.where(fwd_lane, pre, suf)
        for k in range(ns):
            bc_scr[k, rows, :] = jnp.broadcast_to(bcol[:, ns + k:ns + k + 1], (L, 128))
        kt_scr[:, rows] = k_ref[0, rows, :].astype(F32).T.astype(BF16)

    gr = grow_ref[0] + brow_ref[...]
    ig_r = gr[:ns].reshape(ns * nc, L)
    lf_r = jax.nn.log_sigmoid(gr[ns:]).reshape(ns * nc, L)
    lf_r3 = _split3(lf_r)
    pre_r = sum(jnp.dot(part, triu, preferred_element_type=F32) for part in lf_r3)
    suf_r = sum(jnp.dot(part, tril, preferred_element_type=F32) for part in lf_r3)
    fwd_rows = lax.broadcasted_iota(jnp.int32, (ns * nc, 1), 0) < M_HEADS * nc
    a_r = ig_r - jnp.where(fwd_rows, pre_r, suf_r)
    btot = jnp.sum(lf_r, axis=-1, keepdims=True)
    ar_scr[...] = a_r.reshape(ns, nc, L)
    wr_scr[...] = (btot + a_r).reshape(ns, nc, L)
    bt_scr[...] = jnp.broadcast_to(btot, (ns * nc, 128)).reshape(ns, nc, 128)

    c_scr[...] = jnp.zeros_like(c_scr)
    h_scr[...] = jnp.zeros_like(h_scr)
    ones_cols = jnp.ones((L, 128), BF16)

    twice = lambda t: jnp.concatenate([t, t], axis=1)

    def chunk_step(k, c, m_st):
        d, h = divmod(k, M_HEADS)
        r0 = pl.multiple_of(c * L, L)
        q = q_ref[0, pl.ds(r0, L), h * M_QK_DIM:(h + 1) * M_QK_DIM]
        kk = k_ref[0, pl.ds(r0, L), h * M_QK_DIM:(h + 1) * M_QK_DIM]
        v_ext = jnp.concatenate([v_ref[0, pl.ds(r0, L), h * M_V_DIM:(h + 1) * M_V_DIM], ones_cols], axis=1)
        bc = bc_scr[k, pl.ds(r0, L), :]
        ar = ar_scr[k, pl.ds(c, 1), :]
        wr = wr_scr[k, pl.ds(c, 1), :]
        bt = bt_scr[k, pl.ds(c, 1), :]
        dm = jnp.where(lower if d == 0 else upper, twice(bc) + ar, NEG)
        g_inter = bc + m_st
        m_t = jnp.maximum(jnp.broadcast_to(jnp.max(dm, axis=-1, keepdims=True), (L, 128)), g_inter)
        e_inter = jnp.exp(g_inter - m_t) * scale
        s_qk = lax.dot_general(q, kk, (((1,), (1,)), ((), ())), preferred_element_type=F32)
        p = s_qk * scale * jnp.exp(dm - twice(m_t))
        qc = jnp.dot(q, c_scr[k].astype(BF16), preferred_element_type=F32)
        pv = jnp.dot(p.astype(BF16), v_ext, preferred_element_type=F32)
        num = pv[:, :M_V_DIM] + twice(e_inter) * qc[:, :M_V_DIM]
        den = pv[:, M_V_DIM:] + e_inter * qc[:, M_V_DIM:]
        inv = 1.0 / jnp.maximum(jnp.abs(den), jnp.exp(-m_t))
        hsl = (pl.ds(r0, L), slice(h * M_V_DIM, (h + 1) * M_V_DIM))
        h_scr[hsl] = h_scr[hsl] + num * twice(inv)
        m_new = jnp.maximum(bt + m_st, jnp.broadcast_to(jnp.max(wr, axis=-1, keepdims=True), (1, 128)))
        a = jnp.exp(bt + m_st - m_new)
        kt = kt_scr[h * M_QK_DIM:(h + 1) * M_QK_DIM, pl.ds(r0, L)]
        ek_t = (kt.astype(F32) * jnp.exp(wr - twice(m_new))).astype(BF16)
        c_scr[k] = jnp.concatenate([a, a, a], axis=1) * c_scr[k] + jnp.dot(ek_t, v_ext, preferred_element_type=F32)
        return m_new

    def body(it, ms):
        out = []
        for k in range(ns):
            c = it if k < M_HEADS else nc - 1 - it
            out.append(chunk_step(k, c, ms[k]))
        return tuple(out)

    lax.fori_loop(0, nc, body, tuple(jnp.zeros((1, 128), F32) for _ in range(ns)))

    def fin(i, carry):
        r0 = pl.multiple_of(i * L, L)
        for h in range(M_HEADS):
            sl = slice(h * M_V_DIM, (h + 1) * M_V_DIM)
            x = h_scr[pl.ds(r0, L), sl]
            y = x * lax.rsqrt(jnp.mean(x * x, axis=-1, keepdims=True) + EPS) * gain_ref[:, sl]
            o_ref[0, pl.ds(r0, L), sl] = (y * jax.nn.sigmoid(mo_ref[0, pl.ds(r0, L), sl].astype(F32))).astype(BF16)
        return carry

    lax.fori_loop(0, nc, fin, 0)


def _mlstm(proj, gates, b_gates, m_gain):
    b, s, _ = proj.shape
    L = MLSTM_BLOCK
    nc = s // L
    ns = 2 * M_HEADS
    grow = jnp.transpose(gates, (0, 2, 1)).reshape(b, N_GATES, nc, L)
    col = lambda width, off: pl.BlockSpec((1, s, width), lambda bi: (bi, 0, off // width))
    return pl.pallas_call(
        functools.partial(_mlstm_kernel, seq=s),
        grid=(b,),
        in_specs=[
            col(M_QK_WIDTH, COL_MQ), col(M_QK_WIDTH, COL_MK), col(M_WIDTH, COL_MV),
            pl.BlockSpec((1, s, M_WIDTH), lambda bi: (bi, 0, COL_MO // M_WIDTH), pipeline_mode=pl.Buffered(1)),
            pl.BlockSpec((1, s, N_GATES), lambda bi: (bi, 0, 0)),
            pl.BlockSpec((1, N_GATES, nc, L), lambda bi: (bi, 0, 0, 0)),
            pl.BlockSpec((1, N_GATES), lambda bi: (0, 0)),
            pl.BlockSpec((N_GATES, 1, 1), lambda bi: (0, 0, 0)),
            pl.BlockSpec((1, M_WIDTH), lambda bi: (0, 0)),
        ],
        out_specs=pl.BlockSpec((1, s, M_WIDTH), lambda bi: (bi, 0, 0)),
        out_shape=jax.ShapeDtypeStruct((b, s, M_WIDTH), BF16),
        scratch_shapes=[
            pltpu.VMEM((ns, s, 128), F32),
            pltpu.VMEM((M_QK_WIDTH, s), BF16),
            pltpu.VMEM((ns, nc, L), F32), pltpu.VMEM((ns, nc, L), F32), pltpu.VMEM((ns, nc, 128), F32),
            pltpu.VMEM((ns, M_QK_DIM, M_V_DIM + 128), F32),
            pltpu.VMEM((s, M_WIDTH), F32),
        ],
        compiler_params=_params("arbitrary"),
        name="mlstm",
    )(proj, proj, proj, proj, gates, grow, b_gates.reshape(1, N_GATES), b_gates.reshape(N_GATES, 1, 1),
      m_gain.reshape(1, M_WIDTH))


def _outproj_kernel(att_ref, mo_ref, wa_ref, wm_ref, x_ref, g1_ref, sc_ref, sh_ref, wr_ref,
                    x1_ref, h2_ref, aff_ref):
    j = pl.program_id(2)
    nj = pl.num_programs(2)
    tn = OUTPROJ_TN
    mix = (jnp.dot(att_ref[0], wa_ref[...], preferred_element_type=F32)
           + jnp.dot(mo_ref[0], wm_ref[...], preferred_element_type=F32))
    x1t = x_ref[0] + g1_ref[0] * mix
    for jj in range(D_MODEL // tn):
        @pl.when(j == jj)
        def _(jj=jj):
            x1_ref[0, :, jj * tn:(jj + 1) * tn] = x1t

    @pl.when(j == nj - 1)
    def _():
        xf = x1_ref[0]
        r = lax.rsqrt(jnp.mean(xf * xf, axis=-1, keepdims=True) + EPS)
        h2 = (xf * r) * (1.0 + sc_ref[0]) + sh_ref[0]
        h2_hi = h2.astype(BF16)
        h2_ref[0] = h2
        h2_lo = (h2 - h2_hi.astype(F32)).astype(BF16)
        hi_terms = jnp.dot(h2_hi, wr_ref[...], preferred_element_type=F32)
        lo_term = jnp.dot(h2_lo, wr_ref[:, :N_EXPERTS], preferred_element_type=F32)
        logits = hi_terms[:, :N_EXPERTS] + hi_terms[:, N_EXPERTS:] + lo_term
        e = jnp.exp(logits - jnp.max(logits, axis=-1, keepdims=True))
        aff_ref[0] = e / jnp.sum(e, axis=-1, keepdims=True)


def _outproj(att, m_out, w_out, x, mod6, w_router):
    b, s, d = x.shape
    tm, tn = OUTPROJ_TM, OUTPROJ_TN
    half = ATT_WIDTH
    assert w_out.shape[0] == 2 * half and M_WIDTH == half
    wr_hi = w_router.astype(BF16)
    wr_lo = (w_router - wr_hi.astype(F32)).astype(BF16)
    return pl.pallas_call(
        _outproj_kernel,
        grid=(b, s // tm, d // tn),
        in_specs=[
            pl.BlockSpec((1, tm, ATT_WIDTH), lambda bi, i, j: (bi, i, 0)),
            pl.BlockSpec((1, tm, M_WIDTH), lambda bi, i, j: (bi, i, 0)),
            pl.BlockSpec((half, tn), lambda bi, i, j: (0, j)),
            pl.BlockSpec((half, tn), lambda bi, i, j: (1, j)),
            pl.BlockSpec((1, tm, tn), lambda bi, i, j: (bi, i, j)),
            pl.BlockSpec((1, 1, tn), lambda bi, i, j: (bi * 6 + 2, 0, j)),
            pl.BlockSpec((1, 1, d), lambda bi, i, j: (bi * 6 + 4, 0, 0)),
            pl.BlockSpec((1, 1, d), lambda bi, i, j: (bi * 6 + 3, 0, 0)),
            pl.BlockSpec((d, 2 * N_EXPERTS), lambda bi, i, j: (0, 0)),
        ],
        out_specs=[
            pl.BlockSpec((1, tm, d), lambda bi, i, j: (bi, i, 0)),
            pl.BlockSpec((1, tm, d), lambda bi, i, j: (bi, i, 0)),
            pl.BlockSpec((1, tm, N_EXPERTS), lambda bi, i, j: (bi, i, 0)),
        ],
        out_shape=[
            jax.ShapeDtypeStruct((b, s, d), F32),
            jax.ShapeDtypeStruct((b, s, d), F32),
            jax.ShapeDtypeStruct((b, s, N_EXPERTS), F32),
        ],
        compiler_params=_params("arbitrary", "arbitrary", "arbitrary"),
        name="out_proj",
    )(att, m_out, w_out, w_out, x, mod6, mod6, mod6, jnp.concatenate([wr_hi, wr_lo], axis=1))


def _prefix_sum_lanes(x, n):
    lane = lax.broadcasted_iota(jnp.int32, x.shape, 1)
    sh = 1
    while sh < n:
        x = x + jnp.where(lane >= sh, pltpu.roll(x, sh, 1), 0.0)
        sh *= 2
    return x


def _select_kernel(aff_ref, pos_ref, gate_ref, *, cap, seq):
    a = aff_ref[...]
    ne = a.shape[0]

    def body(_, lohi):
        lo, hi = lohi
        mid = jnp.where(lo > 0.0, 0.5 * (lo + hi), hi * (1.0 / 256.0))
        cnt = jnp.sum(jnp.where(a >= mid, 1.0, 0.0), axis=-1, keepdims=True)
        ok = cnt >= cap
        return jnp.where(ok, mid, lo), jnp.where(ok, hi, mid)

    lo0 = jnp.zeros((ne, 1), F32)
    hi0 = jnp.maximum(2.0 * jnp.max(a, axis=-1, keepdims=True), SELECT_MIN_UPPER)
    lo, hi = lax.fori_loop(0, SELECT_BISECTIONS, body, (lo0, hi0))
    above = a >= hi
    band = jnp.logical_and(a >= lo, jnp.logical_not(above))
    n_above = jnp.sum(jnp.where(above, 1.0, 0.0), axis=-1, keepdims=True)
    band_rank = _prefix_sum_lanes(jnp.where(band, 1.0, 0.0), seq)
    sel = jnp.where(above, 1.0, jnp.where(band, jnp.where(band_rank <= cap - n_above, 1.0, 0.0), 0.0))
    pos = _prefix_sum_lanes(sel, seq) - 1.0
    chosen = sel > 0.5
    pos_ref[...] = jnp.where(chosen, pos, -1.0)
    gate_ref[...] = jnp.where(chosen, a, 0.0)


def _select(aff_rows, cap):
    n, s = aff_rows.shape
    spec = pl.BlockSpec((n, s), lambda i: (0, 0))
    return pl.pallas_call(
        functools.partial(_select_kernel, cap=cap, seq=s),
        grid=(1,),
        in_specs=[spec],
        out_specs=[spec, spec],
        out_shape=[jax.ShapeDtypeStruct((n, s), F32), jax.ShapeDtypeStruct((n, s), F32)],
        compiler_params=_params("arbitrary"),
        name="expert_select",
    )(aff_rows)


def _compact_kernel(pos_ref, idx_ref, *, cap):
    seq, ne = pos_ref.shape[1], pos_ref.shape[2]
    rt = COMBINE_BUILD_ROWS
    slot = lax.broadcasted_iota(jnp.int32, (rt, cap), 1).astype(F32)
    tok0 = lax.broadcasted_iota(jnp.int32, (rt, 1), 0).astype(F32)

    def body(i, accs):
        r0 = pl.multiple_of(i * rt, rt)
        pos = pos_ref[0, pl.ds(r0, rt), :]
        tok = tok0 + r0.astype(F32)
        return tuple(
            accs[e] + jnp.sum(jnp.where(slot == pos[:, e:e + 1], tok, 0.0), axis=0, keepdims=True)
            for e in range(ne))

    accs = lax.fori_loop(0, seq // rt, body, tuple(jnp.zeros((1, cap), F32) for _ in range(ne)))
    for e in range(ne):
        idx_ref[0, e:e + 1, :] = accs[e].astype(jnp.int32)


def _compact(pos_cols, cap):
    b, s, ne = pos_cols.shape
    return pl.pallas_call(
        functools.partial(_compact_kernel, cap=cap),
        grid=(b,),
        in_specs=[pl.BlockSpec((1, s, ne), lambda bi: (bi, 0, 0))],
        out_specs=pl.BlockSpec((1, ne, cap), lambda bi: (bi, 0, 0)),
        out_shape=jax.ShapeDtypeStruct((b, ne, cap), jnp.int32),
        compiler_params=_params("arbitrary"),
        name="expert_compact",
    )(pos_cols)


def _ffn_kernel(idx_ref, h2_hbm, wg_ref, wu_ref, wd_ref, ye_ref, x_scr, hid_scr, sem, *, nb, cap):
    e = pl.program_id(0)
    f = pl.program_id(1)
    ne = pl.num_programs(0)
    nf = pl.num_programs(1)
    rows = nb * cap
    per_step = rows // FFN_UP_STEPS
    tf = wg_ref.shape[-1]
    slot = lax.rem(e, 2)
    nslot = 1 - slot
    nxt = jnp.minimum(e + 1, ne - 1)

    def row_copy(expert, r, b, sl):
        tok = idx_ref[expert, r]
        return pltpu.make_async_copy(h2_hbm.at[b, pl.ds(tok, 1), :], x_scr.at[sl, pl.ds(r, 1), :], sem.at[sl])

    def wait_rows(sl):
        pltpu.make_async_copy(x_scr.at[sl], x_scr.at[sl], sem.at[sl]).wait()

    @pl.when(jnp.logical_and(e == 0, f == 0))
    def _():
        def first(r, carry):
            row_copy(0, r, r // cap, 0).start()
            return carry
        lax.fori_loop(0, rows, first, 0)

    for ff in range(FFN_UP_STEPS):
        for cur in range(2):
            @pl.when(jnp.logical_and(f == ff, slot == cur))
            def _(ff=ff, cur=cur):
                if ff == 0:
                    wait_rows(cur)
                for r in range(ff * per_step, (ff + 1) * per_step):
                    row_copy(nxt, r, r // cap, 1 - cur).start(priority=r % 2)
                x = x_scr[cur].astype(BF16)
                g = jnp.dot(x, wg_ref[0, 0].astype(BF16), preferred_element_type=F32)
                u = jnp.dot(x, wu_ref[0, 0].astype(BF16), preferred_element_type=F32)
                hid_scr[:, ff * tf:(ff + 1) * tf] = ((g * jax.nn.sigmoid(g)) * u).astype(BF16)

    @pl.when(f >= FFN_UP_STEPS)
    def _():
        y = jnp.dot(hid_scr[...], wd_ref[0, 0].astype(BF16), preferred_element_type=F32)
        ye_ref[...] = y.reshape(nb, 1, cap, y.shape[-1]).astype(BF16)

    @pl.when(jnp.logical_and(e == ne - 1, f == nf - 1))
    def _():
        wait_rows(nslot)


def _ffn(idx, h2, w_gate, w_up, w_down, layer):
    b, s, d = h2.shape
    ne = idx.shape[0]
    cap = idx.shape[1] // b
    ff = w_gate.shape[-1]
    tf = ff // FFN_UP_STEPS
    dn = d // FFN_DOWN_STEPS
    up = lambda f: jnp.minimum(f, FFN_UP_STEPS - 1)
    down = lambda f: jnp.maximum(f - FFN_UP_STEPS, 0)
    grid_spec = pltpu.PrefetchScalarGridSpec(
        num_scalar_prefetch=1,
        grid=(ne, FFN_UP_STEPS + FFN_DOWN_STEPS),
        in_specs=[
            pl.BlockSpec(memory_space=pl.ANY),
            pl.BlockSpec((1, 1, d, tf), lambda e, f, idx_ref: (layer, e, 0, up(f))),
            pl.BlockSpec((1, 1, d, tf), lambda e, f, idx_ref: (layer, e, 0, up(f))),
            pl.BlockSpec((1, 1, ff, dn), lambda e, f, idx_ref: (layer, e, 0, down(f))),
        ],
        out_specs=pl.BlockSpec((b, 1, cap, dn), lambda e, f, idx_ref: (0, e, 0, down(f))),
        scratch_shapes=[
            pltpu.VMEM((2, b * cap, d), F32),
            pltpu.VMEM((b * cap, ff), BF16),
            pltpu.SemaphoreType.DMA((2,)),
        ],
    )
    return pl.pallas_call(
        functools.partial(_ffn_kernel, nb=b, cap=cap),
        grid_spec=grid_spec,
        out_shape=jax.ShapeDtypeStruct((b, ne, cap, d), BF16),
        compiler_params=_params("arbitrary", "arbitrary"),
        name="expert_ffn",
    )(idx, h2, w_gate, w_up, w_down)


def _combine_kernel(pos_ref, gate_ref, ye_ref, x1_ref, g2_ref, o_ref, scat_scr, *, cap):
    j = pl.program_id(1)
    seq, ne = pos_ref.shape[1], pos_ref.shape[2]
    rt = COMBINE_BUILD_ROWS

    @pl.when(j == 0)
    def _():
        slot = lax.broadcasted_iota(jnp.int32, (rt, cap), 1).astype(F32)

        def build(i, carry):
            r0 = pl.multiple_of(i * rt, rt)
            pos = pos_ref[0, pl.ds(r0, rt), :]
            gate = gate_ref[0, pl.ds(r0, rt), :]
            for e in range(ne):
                scat_scr[pl.ds(r0, rt), e * cap:(e + 1) * cap] = jnp.where(
                    slot == pos[:, e:e + 1], gate[:, e:e + 1], 0.0).astype(BF16)
            return carry

        lax.fori_loop(0, seq // rt, build, 0)

    acc = jnp.dot(scat_scr[...], ye_ref[0], preferred_element_type=F32)
    o_ref[0] = x1_ref[0] + g2_ref[0] * acc


def _combine(pos_cols, gate_cols, ye, x1, mod6):
    b, s, d = x1.shape
    ne, cap = ye.shape[1], ye.shape[2]
    tn = COMBINE_TN
    return pl.pallas_call(
        functools.partial(_combine_kernel, cap=cap),
        grid=(b, d // tn),
        in_specs=[
            pl.BlockSpec((1, s, ne), lambda bi, j: (bi, 0, 0)),
            pl.BlockSpec((1, s, ne), lambda bi, j: (bi, 0, 0)),
            pl.BlockSpec((1, ne * cap, tn), lambda bi, j: (bi, 0, j)),
            pl.BlockSpec((1, s, tn), lambda bi, j: (bi, 0, j)),
            pl.BlockSpec((1, 1, tn), lambda bi, j: (bi * 6 + 5, 0, j)),
        ],
        out_specs=pl.BlockSpec((1, s, tn), lambda bi, j: (bi, 0, j)),
        out_shape=jax.ShapeDtypeStruct((b, s, d), F32),
        scratch_shapes=[pltpu.VMEM((s, ne * cap), BF16)],
        compiler_params=_params("arbitrary", "arbitrary"),
        name="expert_combine",
    )(pos_cols, gate_cols, ye.reshape(b, ne * cap, d), x1, mod6)


def _rope_tables(seq):
    inv = 1.0 / (ROPE_THETA ** (jnp.arange(0, ATT_HEAD_DIM, 2, dtype=F32) / ATT_HEAD_DIM))
    ang = jnp.arange(seq, dtype=F32)[:, None] * inv[None, :]
    ang = jnp.concatenate([ang, ang], axis=-1)
    sign = jnp.where(jnp.arange(ATT_HEAD_DIM) < ATT_HEAD_DIM // 2, -1.0, 1.0).astype(F32)
    return jnp.cos(ang), jnp.sin(ang) * sign


def kernel(x, c, w_ada, b_ada, w_in, b_gates, q_gain, k_gain, sink, m_gain, w_out,
           w_router, w_gate, w_up, w_down):
    b, s, d = x.shape
    depth = w_ada.shape[0]
    cap = CAPACITY_FACTOR * s // N_EXPERTS
    cos, sin_signed = _rope_tables(s)
    c_pad = jnp.pad(c, ((0, 8 - b), (0, 0)))
    mod = _ada(c_pad, w_ada, b_ada)
    for l in range(depth):
        mod6 = mod[l, :b].reshape(b * 6, 1, d)
        proj, gates = _inproj(x, mod6, w_in, l, cos, sin_signed,
                              q_gain[l].reshape(1, -1), k_gain[l].reshape(1, -1))
        att = _attention(proj, sink[l])
        m_out = _mlstm(proj, gates, b_gates[l].reshape(-1), m_gain[l])
        x1, h2, aff = _outproj(att, m_out, w_out[l].astype(BF16), x, mod6, w_router[l])
        pos_r, gate_r = _select(jnp.transpose(aff, (0, 2, 1)).reshape(b * N_EXPERTS, s), cap)
        to_cols = lambda t: jnp.transpose(t.reshape(b, N_EXPERTS, s), (0, 2, 1))
        pos_c = to_cols(pos_r)
        idx = jnp.transpose(_compact(pos_c, cap), (1, 0, 2)).reshape(N_EXPERTS, b * cap)
        ye = _ffn(idx, h2, w_gate, w_up, w_down, l)
        x = _combine(pos_c, to_cols(gate_r), ye, x1, mod6)
    return x
```

```python
import functools

import jax
import jax.numpy as jnp
from jax import lax
from jax.experimental import pallas as pl
from jax.experimental.pallas import tpu as pltpu

F32 = jnp.float32
BF16 = jnp.bfloat16
HIGHEST = lax.Precision.HIGHEST

D_MODEL = 2048
ATT_HEAD_DIM = 128
ATT_HEADS = 8
ATT_KV_HEADS = 2
ATT_GROUP = ATT_HEADS // ATT_KV_HEADS
ATT_WIDTH = ATT_HEADS * ATT_HEAD_DIM
ATT_KV_WIDTH = ATT_KV_HEADS * ATT_HEAD_DIM
WINDOW = 128
ATT_BLOCK = 128
ATT_STEP_BLOCKS = 4
ROPE_THETA = 10000.0
M_HEADS = 4
M_V_DIM = 256
M_QK_DIM = 128
M_WIDTH = M_HEADS * M_V_DIM
M_QK_WIDTH = M_HEADS * M_QK_DIM
MLSTM_BLOCK = 256
N_GATES = 4 * M_HEADS
N_EXPERTS = 16
EXPERT_FF = D_MODEL // 2
CAPACITY_FACTOR = 2
EPS = 1e-6
NEG = -1e30

COL_AQ = 0
COL_MV = ATT_WIDTH
COL_MO = COL_MV + M_WIDTH
COL_AK = COL_MO + M_WIDTH
COL_AV = COL_AK + ATT_KV_WIDTH
COL_MQ = COL_AV + ATT_KV_WIDTH
COL_MK = COL_MQ + M_QK_WIDTH
PROJ_WIDTH = COL_MK + M_QK_WIDTH
REF_COL_MV = ATT_WIDTH + 2 * ATT_KV_WIDTH + 2 * M_QK_WIDTH
IN_COL_BLOCKS = tuple(range(PROJ_WIDTH // 512))

VMEM_LIMIT_BYTES = 56 * 1024 * 1024

ADA_TN = 2048
INPROJ_TM = 1024
INPROJ_TN = 512
OUTPROJ_TM = 1024
OUTPROJ_TN = 512
FFN_UP_STEPS = 2
FFN_DOWN_STEPS = 2
COMBINE_TN = 512
COMBINE_BUILD_ROWS = 256
SELECT_BISECTIONS = 64
SELECT_MIN_UPPER = 1e-30


def _params(*sem):
    return pltpu.CompilerParams(dimension_semantics=sem, vmem_limit_bytes=VMEM_LIMIT_BYTES)


def _ada_kernel(c_ref, w_ref, b_ref, o_ref):
    c = c_ref[...]
    c_act = c * jax.nn.sigmoid(c)
    w = w_ref[0]
    w_hi = w.astype(BF16)
    w_lo = (w - w_hi.astype(F32)).astype(BF16)
    c_hi = c_act.astype(BF16)
    c_lo = (c_act - c_hi.astype(F32)).astype(BF16)
    acc = jnp.dot(c_hi, w_hi, preferred_element_type=F32)
    acc = acc + jnp.dot(c_lo, w_hi, preferred_element_type=F32)
    acc = acc + jnp.dot(c_hi, w_lo, preferred_element_type=F32)
    o_ref[0] = acc + b_ref[0]


def _ada(c_pad, w_ada, b_ada):
    depth, d, n = w_ada.shape
    rows = c_pad.shape[0]
    return pl.pallas_call(
        _ada_kernel,
        grid=(depth, n // ADA_TN),
        in_specs=[
            pl.BlockSpec((rows, d), lambda l, j: (0, 0)),
            pl.BlockSpec((1, d, ADA_TN), lambda l, j: (l, 0, j)),
            pl.BlockSpec((1, 1, ADA_TN), lambda l, j: (l, 0, j)),
        ],
        out_specs=pl.BlockSpec((1, rows, ADA_TN), lambda l, j: (l, 0, j)),
        out_shape=jax.ShapeDtypeStruct((depth, rows, n), F32),
        compiler_params=_params("arbitrary", "arbitrary"),
        name="ada_mod",
    )(c_pad, w_ada, b_ada.reshape(depth, 1, n))


def _inproj_kernel(perm_ref, x_ref, sc_ref, sh_ref, w_ref, wg_ref, cos_ref, sin_ref, qg_ref, kg_ref,
                   o_ref, g_ref, h_scr):
    del perm_ref
    j = pl.program_id(2)

    @pl.when(j == 0)
    def _():
        x = x_ref[0]
        r = lax.rsqrt(jnp.mean(x * x, axis=-1, keepdims=True) + EPS)
        h = (x * r) * (1.0 + sc_ref[0]) + sh_ref[0]
        hb = h.astype(BF16)
        h_scr[...] = hb
        g_ref[0] = jnp.dot(hb, wg_ref[...], preferred_element_type=F32)

    y = jnp.dot(h_scr[...], w_ref[...], preferred_element_type=F32)

    def norm_rope(t, gain):
        tn = t * lax.rsqrt(jnp.mean(t * t, axis=-1, keepdims=True) + EPS) * gain
        return tn * cos_ref[...] + pltpu.roll(tn, ATT_HEAD_DIM // 2, 1) * sin_ref[...]

    heads_per_tile = INPROJ_TN // ATT_HEAD_DIM
    q_tiles = ATT_WIDTH // INPROJ_TN
    kv_tile = COL_AK // INPROJ_TN
    scale = ATT_HEAD_DIM ** -0.5

    @pl.when(j < q_tiles)
    def _():
        for u in range(heads_per_tile):
            sl = slice(u * ATT_HEAD_DIM, (u + 1) * ATT_HEAD_DIM)
            o_ref[0, :, sl] = (norm_rope(y[:, sl], qg_ref[...]) * scale).astype(BF16)

    @pl.when(j == kv_tile)
    def _():
        for u in range(ATT_KV_HEADS):
            sl = slice(u * ATT_HEAD_DIM, (u + 1) * ATT_HEAD_DIM)
            o_ref[0, :, sl] = norm_rope(y[:, sl], kg_ref[...]).astype(BF16)
        o_ref[0, :, ATT_KV_WIDTH:] = y[:, ATT_KV_WIDTH:].astype(BF16)

    @pl.when(jnp.logical_and(j >= q_tiles, j != kv_tile))
    def _():
        o_ref[0] = y.astype(BF16)


def _inproj(x, mod6, w_in, layer, cos, sin_signed, q_gain, k_gain):
    b, s, d = x.shape
    tm, tn = INPROJ_TM, INPROJ_TN
    assert COL_AK % tn == 0 and ATT_WIDTH % tn == 0 and 2 * ATT_KV_WIDTH == tn
    w = w_in[layer]
    w_main = jnp.concatenate([w[:, :COL_MV], w[:, REF_COL_MV:PROJ_WIDTH], w[:, COL_MV:REF_COL_MV]],
                             axis=1).astype(BF16)
    w_gate = w[:, PROJ_WIDTH:].astype(BF16)
    grid_spec = pltpu.PrefetchScalarGridSpec(
        num_scalar_prefetch=1,
        grid=(b, s // tm, PROJ_WIDTH // tn),
        in_specs=[
            pl.BlockSpec((1, tm, d), lambda bi, i, j, perm: (bi, i, 0)),
            pl.BlockSpec((1, 1, d), lambda bi, i, j, perm: (bi * 6 + 1, 0, 0)),
            pl.BlockSpec((1, 1, d), lambda bi, i, j, perm: (bi * 6 + 0, 0, 0)),
            pl.BlockSpec((d, tn), lambda bi, i, j, perm: (0, perm[j])),
            pl.BlockSpec((d, N_GATES), lambda bi, i, j, perm: (0, 0)),
            pl.BlockSpec((tm, ATT_HEAD_DIM), lambda bi, i, j, perm: (i, 0)),
            pl.BlockSpec((tm, ATT_HEAD_DIM), lambda bi, i, j, perm: (i, 0)),
            pl.BlockSpec((1, ATT_HEAD_DIM), lambda bi, i, j, perm: (0, 0)),
            pl.BlockSpec((1, ATT_HEAD_DIM), lambda bi, i, j, perm: (0, 0)),
        ],
        out_specs=[
            pl.BlockSpec((1, tm, tn), lambda bi, i, j, perm: (bi, i, j)),
            pl.BlockSpec((1, tm, N_GATES), lambda bi, i, j, perm: (bi, i, 0)),
        ],
        scratch_shapes=[pltpu.VMEM((tm, d), BF16)],
    )
    return pl.pallas_call(
        _inproj_kernel,
        grid_spec=grid_spec,
        out_shape=[
            jax.ShapeDtypeStruct((b, s, PROJ_WIDTH), BF16),
            jax.ShapeDtypeStruct((b, s, N_GATES), F32),
        ],
        compiler_params=_params("arbitrary", "arbitrary", "arbitrary"),
        name="in_proj",
    )(jnp.asarray(IN_COL_BLOCKS, jnp.int32), x, mod6, mod6, w_main, w_gate, cos, sin_signed, q_gain, k_gain)


def _attn_kernel(sink_ref, q_ref, kl_ref, kc_ref, kr_ref, vl_ref, vc_ref, vr_ref, o_ref, *, n_steps):
    n = pl.program_id(1)
    L = ATT_BLOCK
    rows = ATT_GROUP * L
    i = lax.broadcasted_iota(jnp.int32, (rows, 3 * L), 0) & (L - 1)
    jj = lax.broadcasted_iota(jnp.int32, (rows, 3 * L), 1)
    in_window = jnp.abs(jj - L - i) <= WINDOW
    first_mask = in_window & ((jj >= L) | (n > 0))
    last_mask = in_window & ((jj < 2 * L) | (n < n_steps - 1))
    rgrp = lax.broadcasted_iota(jnp.int32, (rows, 1), 0) // L
    for kv in range(ATT_KV_HEADS):
        hs = slice(kv * ATT_HEAD_DIM, (kv + 1) * ATT_HEAD_DIM)
        kband = jnp.concatenate([kl_ref[0, :, hs], kc_ref[0, :, hs], kr_ref[0, :, hs]], axis=0)
        vband = jnp.concatenate([vl_ref[0, :, hs], vc_ref[0, :, hs], vr_ref[0, :, hs]], axis=0)
        sink = jnp.zeros((rows, 1), F32)
        for g in range(ATT_GROUP):
            sink = jnp.where(rgrp == g, sink_ref[kv * ATT_GROUP + g], sink)
        for blk in range(ATT_STEP_BLOCKS):
            qrows = slice(blk * L, (blk + 1) * L)
            q = jnp.concatenate(
                [q_ref[0, qrows, (kv * ATT_GROUP + g) * ATT_HEAD_DIM:(kv * ATT_GROUP + g + 1) * ATT_HEAD_DIM]
                 for g in range(ATT_GROUP)], axis=0)
            kb = kband[blk * L:(blk + 3) * L]
            vb = vband[blk * L:(blk + 3) * L]
            valid = first_mask if blk == 0 else (last_mask if blk == ATT_STEP_BLOCKS - 1 else in_window)
            s = lax.dot_general(q, kb, (((1,), (1,)), ((), ())), preferred_element_type=F32)
            s = jnp.where(valid, s, NEG)
            m = jnp.maximum(jnp.max(s, axis=-1, keepdims=True), sink)
            p = jnp.exp(s - m)
            denom = jnp.sum(p, axis=-1, keepdims=True) + jnp.exp(sink - m)
            o = jnp.dot(p.astype(BF16), vb, preferred_element_type=F32) / denom
            for g in range(ATT_GROUP):
                h = kv * ATT_GROUP + g
                o_ref[0, qrows, h * ATT_HEAD_DIM:(h + 1) * ATT_HEAD_DIM] = o[g * L:(g + 1) * L].astype(BF16)


def _attention(proj, sink):
    b, s, _ = proj.shape
    L = ATT_BLOCK
    nb = s // L
    sb = ATT_STEP_BLOCKS
    assert sb >= 2 and nb % sb == 0
    kblk = COL_AK // ATT_KV_WIDTH
    vblk = COL_AV // ATT_KV_WIDTH
    left = lambda col: pl.BlockSpec((1, L, ATT_KV_WIDTH), lambda bi, n: (bi, jnp.maximum(n * sb - 1, 0), col))
    right = lambda col: pl.BlockSpec((1, L, ATT_KV_WIDTH), lambda bi, n: (bi, jnp.minimum((n + 1) * sb, nb - 1), col))
    centre = lambda col: pl.BlockSpec((1, sb * L, ATT_KV_WIDTH), lambda bi, n: (bi, n, col))
    return pl.pallas_call(
        functools.partial(_attn_kernel, n_steps=nb // sb),
        grid=(b, nb // sb),
        in_specs=[
            pl.BlockSpec(memory_space=pltpu.SMEM),
            pl.BlockSpec((1, sb * L, ATT_WIDTH), lambda bi, n: (bi, n, COL_AQ // ATT_WIDTH)),
            left(kblk), centre(kblk), right(kblk),
            left(vblk), centre(vblk), right(vblk),
        ],
        out_specs=pl.BlockSpec((1, sb * L, ATT_WIDTH), lambda bi, n: (bi, n, 0)),
        out_shape=jax.ShapeDtypeStruct((b, s, ATT_WIDTH), BF16),
        compiler_params=_params("arbitrary", "arbitrary"),
        name="window_attn",
    )(sink, proj, proj, proj, proj, proj, proj, proj)


def _split3(x):
    hi = x.astype(BF16)
    rest = x - hi.astype(F32)
    mid = rest.astype(BF16)
    return hi, mid, (rest - mid.astype(F32)).astype(BF16)


def _mlstm_kernel(q_ref, k_ref, v_ref, mo_ref, gcol_ref, grow_ref, bcol_ref, brow_ref, gain_ref, o_ref,
                  bc_scr, kt_scr, ar_scr, wr_scr, bt_scr, c_scr, h_scr, *, seq):
    L = MLSTM_BLOCK
    nc = seq // L
    ns = 2 * M_HEADS
    scale = M_QK_DIM ** -0.5
    row = lax.broadcasted_iota(jnp.int32, (L, L), 0)
    col = lax.broadcasted_iota(jnp.int32, (L, L), 1)
    lower = col <= row
    upper = col >= row
    tril = jnp.where(lower, 1.0, 0.0).astype(BF16)
    triu = jnp.where(upper, 1.0, 0.0).astype(BF16)

    fwd_lane = lax.broadcasted_iota(jnp.int32, (1, N_GATES), 1) < ns + M_HEADS
    for c in range(nc):
        rows = slice(c * L, (c + 1) * L)
        lf = jax.nn.log_sigmoid(gcol_ref[0, rows, :] + bcol_ref[...])
        lf3 = _split3(lf)
        pre = sum(jnp.dot(tril, part, preferred_element_type=F32) for part in lf3)
        suf = sum(jnp.dot(triu, part, preferred_element_type=F32) for part in lf3)
        bcol = jnp.where(fwd_lane, pre, suf)
        for k in range(ns):
            bc_scr[k, rows, :] = jnp.broadcast_to(bcol[:, ns + k:ns + k + 1], (L, 128))
        kt_scr[:, rows] = k_ref[0, rows, :].astype(F32).T.astype(BF16)

    gr = grow_ref[0] + brow_ref[...]
    ig_r = gr[:ns].reshape(ns * nc, L)
    lf_r = jax.nn.log_sigmoid(gr[ns:]).reshape(ns * nc, L)
    lf_r3 = _split3(lf_r)
    pre_r = sum(jnp.dot(part, triu, preferred_element_type=F32) for part in lf_r3)
    suf_r = sum(jnp.dot(part, tril, preferred_element_type=F32) for part in lf_r3)
    fwd_rows = lax.broadcasted_iota(jnp.int32, (ns * nc, 1), 0) < M_HEADS * nc
    a_r = ig_r - jnp.where(fwd_rows, pre_r, suf_r)
    btot = jnp.sum(lf_r, axis=-1, keepdims=True)
    ar_scr[...] = a_r.reshape(ns, nc, L)
    wr_scr[...] = (btot + a_r).reshape(ns, nc, L)
    bt_scr[...] = jnp.broadcast_to(btot, (ns * nc, 128)).reshape(ns, nc, 128)

    c_scr[...] = jnp.zeros_like(c_scr)
    h_scr[...] = jnp.zeros_like(h_scr)
    ones_cols = jnp.ones((L, 128), BF16)

    twice = lambda t: jnp.concatenate([t, t], axis=1)

    def chunk_step(k, c, m_st):
        d, h = divmod(k, M_HEADS)
        r0 = pl.multiple_of(c * L, L)
        q = q_ref[0, pl.ds(r0, L), h * M_QK_DIM:(h + 1) * M_QK_DIM]
        kk = k_ref[0, pl.ds(r0, L), h * M_QK_DIM:(h + 1) * M_QK_DIM]
        v_ext = jnp.concatenate([v_ref[0, pl.ds(r0, L), h * M_V_DIM:(h + 1) * M_V_DIM], ones_cols], axis=1)
        bc = bc_scr[k, pl.ds(r0, L), :]
        ar = ar_scr[k, pl.ds(c, 1), :]
        wr = wr_scr[k, pl.ds(c, 1), :]
        bt = bt_scr[k, pl.ds(c, 1), :]
        dm = jnp.where(lower if d == 0 else upper, twice(bc) + ar, NEG)
        g_inter = bc + m_st
        m_t = jnp.maximum(jnp.broadcast_to(jnp.max(dm, axis=-1, keepdims=True), (L, 128)), g_inter)
        e_inter = jnp.exp(g_inter - m_t) * scale
        s_qk = lax.dot_general(q, kk, (((1,), (1,)), ((), ())), preferred_element_type=F32)
        p = s_qk * scale * jnp.exp(dm - twice(m_t))
        qc = jnp.dot(q, c_scr[k].astype(BF16), preferred_element_type=F32)
        pv = jnp.dot(p.astype(BF16), v_ext, preferred_element_type=F32)
        num = pv[:, :M_V_DIM] + twice(e_inter) * qc[:, :M_V_DIM]
        den = pv[:, M_V_DIM:] + e_inter * qc[:, M_V_DIM:]
        inv = 1.0 / jnp.maximum(jnp.abs(den), jnp.exp(-m_t))
        hsl = (pl.ds(r0, L), slice(h * M_V_DIM, (h + 1) * M_V_DIM))
        h_scr[hsl] = h_scr[hsl] + num * twice(inv)
        m_new = jnp.maximum(bt + m_st, jnp.broadcast_to(jnp.max(wr, axis=-1, keepdims=True), (1, 128)))
        a = jnp.exp(bt + m_st - m_new)
        kt = kt_scr[h * M_QK_DIM:(h + 1) * M_QK_DIM, pl.ds(r0, L)]
        ek_t = (kt.astype(F32) * jnp.exp(wr - twice(m_new))).astype(BF16)
        c_scr[k] = jnp.concatenate([a, a, a], axis=1) * c_scr[k] + jnp.dot(ek_t, v_ext, preferred_element_type=F32)
        return m_new

    def body(it, ms):
        out = []
        for k in range(ns):
            c = it if k < M_HEADS else nc - 1 - it
            out.append(chunk_step(k, c, ms[k]))
        return tuple(out)

    lax.fori_loop(0, nc, body, tuple(jnp.zeros((1, 128), F32) for _ in range(ns)))

    def fin(i, carry):
        r0 = pl.multiple_of(i * L, L)
        for h in range(M_HEADS):
            sl = slice(h * M_V_DIM, (h + 1) * M_V_DIM)
            x = h_scr[pl.ds(r0, L), sl]
            y = x * lax.rsqrt(jnp.mean(x * x, axis=-1, keepdims=True) + EPS) * gain_ref[:, sl]
            o_ref[0, pl.ds(r0, L), sl] = (y * jax.nn.sigmoid(mo_ref[0, pl.ds(r0, L), sl].astype(F32))).astype(BF16)
        return carry

    lax.fori_loop(0, nc, fin, 0)


def _mlstm(proj, gates, b_gates, m_gain):
    b, s, _ = proj.shape
    L = MLSTM_BLOCK
    nc = s // L
    ns = 2 * M_HEADS
    grow = jnp.transpose(gates, (0, 2, 1)).reshape(b, N_GATES, nc, L)
    col = lambda width, off: pl.BlockSpec((1, s, width), lambda bi: (bi, 0, off // width))
    return pl.pallas_call(
        functools.partial(_mlstm_kernel, seq=s),
        grid=(b,),
        in_specs=[
            col(M_QK_WIDTH, COL_MQ), col(M_QK_WIDTH, COL_MK), col(M_WIDTH, COL_MV),
            pl.BlockSpec((1, s, M_WIDTH), lambda bi: (bi, 0, COL_MO // M_WIDTH), pipeline_mode=pl.Buffered(1)),
            pl.BlockSpec((1, s, N_GATES), lambda bi: (bi, 0, 0)),
            pl.BlockSpec((1, N_GATES, nc, L), lambda bi: (bi, 0, 0, 0)),
            pl.BlockSpec((1, N_GATES), lambda bi: (0, 0)),
            pl.BlockSpec((N_GATES, 1, 1), lambda bi: (0, 0, 0)),
            pl.BlockSpec((1, M_WIDTH), lambda bi: (0, 0)),
        ],
        out_specs=pl.BlockSpec((1, s, M_WIDTH), lambda bi: (bi, 0, 0)),
        out_shape=jax.ShapeDtypeStruct((b, s, M_WIDTH), BF16),
        scratch_shapes=[
            pltpu.VMEM((ns, s, 128), F32),
            pltpu.VMEM((M_QK_WIDTH, s), BF16),
            pltpu.VMEM((ns, nc, L), F32), pltpu.VMEM((ns, nc, L), F32), pltpu.VMEM((ns, nc, 128), F32),
            pltpu.VMEM((ns, M_QK_DIM, M_V_DIM + 128), F32),
            pltpu.VMEM((s, M_WIDTH), F32),
        ],
        compiler_params=_params("arbitrary"),
        name="mlstm",
    )(proj, proj, proj, proj, gates, grow, b_gates.reshape(1, N_GATES), b_gates.reshape(N_GATES, 1, 1),
      m_gain.reshape(1, M_WIDTH))


def _outproj_kernel(att_ref, mo_ref, wa_ref, wm_ref, x_ref, g1_ref, sc_ref, sh_ref, wr_ref,
                    x1_ref, aff_ref):
    j = pl.program_id(2)
    nj = pl.num_programs(2)
    tn = OUTPROJ_TN
    mix = (jnp.dot(att_ref[0], wa_ref[...], preferred_element_type=F32)
           + jnp.dot(mo_ref[0], wm_ref[...], preferred_element_type=F32))
    x1t = x_ref[0] + g1_ref[0] * mix
    for jj in range(D_MODEL // tn):
        @pl.when(j == jj)
        def _(jj=jj):
            x1_ref[0, :, jj * tn:(jj + 1) * tn] = x1t

    @pl.when(j == nj - 1)
    def _():
        xf = x1_ref[0]
        r = lax.rsqrt(jnp.mean(xf * xf, axis=-1, keepdims=True) + EPS)
        h2 = (xf * r) * (1.0 + sc_ref[0]) + sh_ref[0]
        h2_hi = h2.astype(BF16)
        h2_lo = (h2 - h2_hi.astype(F32)).astype(BF16)
        hi_terms = jnp.dot(h2_hi, wr_ref[...], preferred_element_type=F32)
        lo_term = jnp.dot(h2_lo, wr_ref[:, :N_EXPERTS], preferred_element_type=F32)
        logits = hi_terms[:, :N_EXPERTS] + hi_terms[:, N_EXPERTS:] + lo_term
        e = jnp.exp(logits - jnp.max(logits, axis=-1, keepdims=True))
        aff_ref[0] = e / jnp.sum(e, axis=-1, keepdims=True)


def _outproj(att, m_out, w_out, x, mod6, w_router):
    b, s, d = x.shape
    tm, tn = OUTPROJ_TM, OUTPROJ_TN
    half = ATT_WIDTH
    assert w_out.shape[0] == 2 * half and M_WIDTH == half
    wr_hi = w_router.astype(BF16)
    wr_lo = (w_router - wr_hi.astype(F32)).astype(BF16)
    return pl.pallas_call(
        _outproj_kernel,
        grid=(b, s // tm, d // tn),
        in_specs=[
            pl.BlockSpec((1, tm, ATT_WIDTH), lambda bi, i, j: (bi, i, 0)),
            pl.BlockSpec((1, tm, M_WIDTH), lambda bi, i, j: (bi, i, 0)),
            pl.BlockSpec((half, tn), lambda bi, i, j: (0, j)),
            pl.BlockSpec((half, tn), lambda bi, i, j: (1, j)),
            pl.BlockSpec((1, tm, tn), lambda bi, i, j: (bi, i, j)),
            pl.BlockSpec((1, 1, tn), lambda bi, i, j: (bi * 6 + 2, 0, j)),
            pl.BlockSpec((1, 1, d), lambda bi, i, j: (bi * 6 + 4, 0, 0)),
            pl.BlockSpec((1, 1, d), lambda bi, i, j: (bi * 6 + 3, 0, 0)),
            pl.BlockSpec((d, 2 * N_EXPERTS), lambda bi, i, j: (0, 0)),
        ],
        out_specs=[
            pl.BlockSpec((1, tm, d), lambda bi, i, j: (bi, i, 0)),
            pl.BlockSpec((1, tm, N_EXPERTS), lambda bi, i, j: (bi, i, 0)),
        ],
        out_shape=[
            jax.ShapeDtypeStruct((b, s, d), F32),
            jax.ShapeDtypeStruct((b, s, N_EXPERTS), F32),
        ],
        compiler_params=_params("arbitrary", "arbitrary", "arbitrary"),
        name="out_proj",
    )(att, m_out, w_out, w_out, x, mod6, mod6, mod6, jnp.concatenate([wr_hi, wr_lo], axis=1))


def _prefix_sum_lanes(x, n):
    lane = lax.broadcasted_iota(jnp.int32, x.shape, 1)
    sh = 1
    while sh < n:
        x = x + jnp.where(lane >= sh, pltpu.roll(x, sh, 1), 0.0)
        sh *= 2
    return x


def _select_kernel(aff_ref, pos_ref, gate_ref, *, cap, seq):
    a = aff_ref[...]
    ne = a.shape[0]

    def body(_, lohi):
        lo, hi = lohi
        mid = jnp.where(lo > 0.0, 0.5 * (lo + hi), hi * (1.0 / 256.0))
        cnt = jnp.sum(jnp.where(a >= mid, 1.0, 0.0), axis=-1, keepdims=True)
        ok = cnt >= cap
        return jnp.where(ok, mid, lo), jnp.where(ok, hi, mid)

    lo0 = jnp.zeros((ne, 1), F32)
    hi0 = jnp.maximum(2.0 * jnp.max(a, axis=-1, keepdims=True), SELECT_MIN_UPPER)
    lo, hi = lax.fori_loop(0, SELECT_BISECTIONS, body, (lo0, hi0))
    above = a >= hi
    band = jnp.logical_and(a >= lo, jnp.logical_not(above))
    n_above = jnp.sum(jnp.where(above, 1.0, 0.0), axis=-1, keepdims=True)
    band_rank = _prefix_sum_lanes(jnp.where(band, 1.0, 0.0), seq)
    sel = jnp.where(above, 1.0, jnp.where(band, jnp.where(band_rank <= cap - n_above, 1.0, 0.0), 0.0))
    pos = _prefix_sum_lanes(sel, seq) - 1.0
    chosen = sel > 0.5
    pos_ref[...] = jnp.where(chosen, pos, -1.0)
    gate_ref[...] = jnp.where(chosen, a, 0.0)


def _select(aff_rows, cap):
    n, s = aff_rows.shape
    spec = pl.BlockSpec((n, s), lambda i: (0, 0))
    return pl.pallas_call(
        functools.partial(_select_kernel, cap=cap, seq=s),
        grid=(1,),
        in_specs=[spec],
        out_specs=[spec, spec],
        out_shape=[jax.ShapeDtypeStruct((n, s), F32), jax.ShapeDtypeStruct((n, s), F32)],
        compiler_params=_params("arbitrary"),
        name="expert_select",
    )(aff_rows)


def _compact_kernel(pos_ref, idx_ref, *, cap):
    seq, ne = pos_ref.shape[1], pos_ref.shape[2]
    rt = COMBINE_BUILD_ROWS
    slot = lax.broadcasted_iota(jnp.int32, (rt, cap), 1).astype(F32)
    tok0 = lax.broadcasted_iota(jnp.int32, (rt, 1), 0).astype(F32)

    def body(i, accs):
        r0 = pl.multiple_of(i * rt, rt)
        pos = pos_ref[0, pl.ds(r0, rt), :]
        tok = tok0 + r0.astype(F32)
        return tuple(
            accs[e] + jnp.sum(jnp.where(slot == pos[:, e:e + 1], tok, 0.0), axis=0, keepdims=True)
            for e in range(ne))

    accs = lax.fori_loop(0, seq // rt, body, tuple(jnp.zeros((1, cap), F32) for _ in range(ne)))
    for e in range(ne):
        idx_ref[0, e:e + 1, :] = accs[e].astype(jnp.int32)


def _compact(pos_cols, cap):
    b, s, ne = pos_cols.shape
    return pl.pallas_call(
        functools.partial(_compact_kernel, cap=cap),
        grid=(b,),
        in_specs=[pl.BlockSpec((1, s, ne), lambda bi: (bi, 0, 0))],
        out_specs=pl.BlockSpec((1, ne, cap), lambda bi: (bi, 0, 0)),
        out_shape=jax.ShapeDtypeStruct((b, ne, cap), jnp.int32),
        compiler_params=_params("arbitrary"),
        name="expert_compact",
    )(pos_cols)


def _ffn_kernel(idx_ref, x1_hbm, mod_ref, wg_ref, wu_ref, wd_ref, ye_ref, x_scr, hid_scr, sem, *, nb, cap):
    e = pl.program_id(0)
    f = pl.program_id(1)
    ne = pl.num_programs(0)
    nf = pl.num_programs(1)
    rows = nb * cap
    per_step = rows // FFN_UP_STEPS
    tf = wg_ref.shape[-1]
    slot = lax.rem(e, 2)
    nxt = jnp.minimum(e + 1, ne - 1)

    def row_copy(expert, r, b, sl):
        tok = idx_ref[expert, r]
        return pltpu.make_async_copy(x1_hbm.at[b, pl.ds(tok, 1), :], x_scr.at[sl, pl.ds(r, 1), :], sem.at[sl])

    def wait_rows(sl):
        pltpu.make_async_copy(x_scr.at[sl], x_scr.at[sl], sem.at[sl]).wait()

    def normalised(sl):
        parts = []
        for b in range(nb):
            x = x_scr[sl, b * cap:(b + 1) * cap, :]
            r = lax.rsqrt(jnp.mean(x * x, axis=-1, keepdims=True) + EPS)
            parts.append(((x * r) * (1.0 + mod_ref[b * 6 + 4]) + mod_ref[b * 6 + 3]).astype(BF16))
        return jnp.concatenate(parts, axis=0)

    @pl.when(jnp.logical_and(e == 0, f == 0))
    def _():
        def first(r, carry):
            row_copy(0, r, r // cap, 0).start()
            return carry
        lax.fori_loop(0, rows, first, 0)

    for ff in range(FFN_UP_STEPS):
        for cur in range(2):
            @pl.when(jnp.logical_and(f == ff, slot == cur))
            def _(ff=ff, cur=cur):
                if ff == 0:
                    wait_rows(cur)
                for r in range(ff * per_step, (ff + 1) * per_step):
                    row_copy(nxt, r, r // cap, 1 - cur).start(priority=r % 2)
                x = normalised(cur)
                g = jnp.dot(x, wg_ref[0, 0].astype(BF16), preferred_element_type=F32)
                u = jnp.dot(x, wu_ref[0, 0].astype(BF16), preferred_element_type=F32)
                hid_scr[:, ff * tf:(ff + 1) * tf] = ((g * jax.nn.sigmoid(g)) * u).astype(BF16)

    @pl.when(f >= FFN_UP_STEPS)
    def _():
        y = jnp.dot(hid_scr[...], wd_ref[0, 0].astype(BF16), preferred_element_type=F32)
        ye_ref[...] = y.reshape(nb, 1, cap, y.shape[-1]).astype(BF16)

    @pl.when(jnp.logical_and(e == ne - 1, f == nf - 1))
    def _():
        wait_rows(1 - slot)


def _ffn(idx, x1, mod6, w_gate, w_up, w_down, layer):
    b, s, d = x1.shape
    ne = idx.shape[0]
    cap = idx.shape[1] // b
    ff = w_gate.shape[-1]
    tf = ff // FFN_UP_STEPS
    dn = d // FFN_DOWN_STEPS
    up = lambda f: jnp.minimum(f, FFN_UP_STEPS - 1)
    down = lambda f: jnp.maximum(f - FFN_UP_STEPS, 0)
    grid_spec = pltpu.PrefetchScalarGridSpec(
        num_scalar_prefetch=1,
        grid=(ne, FFN_UP_STEPS + FFN_DOWN_STEPS),
        in_specs=[
            pl.BlockSpec(memory_space=pl.ANY),
            pl.BlockSpec(mod6.shape, lambda e, f, idx_ref: (0, 0, 0)),
            pl.BlockSpec((1, 1, d, tf), lambda e, f, idx_ref: (layer, e, 0, up(f))),
            pl.BlockSpec((1, 1, d, tf), lambda e, f, idx_ref: (layer, e, 0, up(f))),
            pl.BlockSpec((1, 1, ff, dn), lambda e, f, idx_ref: (layer, e, 0, down(f))),
        ],
        out_specs=pl.BlockSpec((b, 1, cap, dn), lambda e, f, idx_ref: (0, e, 0, down(f))),
        scratch_shapes=[
            pltpu.VMEM((2, b * cap, d), F32),
            pltpu.VMEM((b * cap, ff), BF16),
            pltpu.SemaphoreType.DMA((2,)),
        ],
    )
    return pl.pallas_call(
        functools.partial(_ffn_kernel, nb=b, cap=cap),
        grid_spec=grid_spec,
        out_shape=jax.ShapeDtypeStruct((b, ne, cap, d), BF16),
        compiler_params=_params("arbitrary", "arbitrary"),
        name="expert_ffn",
    )(idx, x1, mod6, w_gate, w_up, w_down)


def _combine_kernel(pos_ref, gate_ref, ye_ref, x1_ref, g2_ref, o_ref, scat_scr, *, cap):
    j = pl.program_id(1)
    seq, ne = pos_ref.shape[1], pos_ref.shape[2]
    rt = COMBINE_BUILD_ROWS

    @pl.when(j == 0)
    def _():
        slot = lax.broadcasted_iota(jnp.int32, (rt, cap), 1).astype(F32)

        def build(i, carry):
            r0 = pl.multiple_of(i * rt, rt)
            pos = pos_ref[0, pl.ds(r0, rt), :]
            gate = gate_ref[0, pl.ds(r0, rt), :]
            for e in range(ne):
                scat_scr[pl.ds(r0, rt), e * cap:(e + 1) * cap] = jnp.where(
                    slot == pos[:, e:e + 1], gate[:, e:e + 1], 0.0).astype(BF16)
            return carry

        lax.fori_loop(0, seq // rt, build, 0)

    acc = jnp.dot(scat_scr[...], ye_ref[0], preferred_element_type=F32)
    o_ref[0] = x1_ref[0] + g2_ref[0] * acc


def _combine(pos_cols, gate_cols, ye, x1, mod6):
    b, s, d = x1.shape
    ne, cap = ye.shape[1], ye.shape[2]
    tn = COMBINE_TN
    return pl.pallas_call(
        functools.partial(_combine_kernel, cap=cap),
        grid=(b, d // tn),
        in_specs=[
            pl.BlockSpec((1, s, ne), lambda bi, j: (bi, 0, 0)),
            pl.BlockSpec((1, s, ne), lambda bi, j: (bi, 0, 0)),
            pl.BlockSpec((1, ne * cap, tn), lambda bi, j: (bi, 0, j)),
            pl.BlockSpec((1, s, tn), lambda bi, j: (bi, 0, j)),
            pl.BlockSpec((1, 1, tn), lambda bi, j: (bi * 6 + 5, 0, j)),
        ],
        out_specs=pl.BlockSpec((1, s, tn), lambda bi, j: (bi, 0, j)),
        out_shape=jax.ShapeDtypeStruct((b, s, d), F32),
        scratch_shapes=[pltpu.VMEM((s, ne * cap), BF16)],
        compiler_params=_params("arbitrary", "arbitrary"),
        name="expert_combine",
    )(pos_cols, gate_cols, ye.reshape(b, ne * cap, d), x1, mod6)


def _rope_tables(seq):
    inv = 1.0 / (ROPE_THETA ** (jnp.arange(0, ATT_HEAD_DIM, 2, dtype=F32) / ATT_HEAD_DIM))
    ang = jnp.arange(seq, dtype=F32)[:, None] * inv[None, :]
    ang = jnp.concatenate([ang, ang], axis=-1)
    sign = jnp.where(jnp.arange(ATT_HEAD_DIM) < ATT_HEAD_DIM // 2, -1.0, 1.0).astype(F32)
    return jnp.cos(ang), jnp.sin(ang) * sign


def kernel(x, c, w_ada, b_ada, w_in, b_gates, q_gain, k_gain, sink, m_gain, w_out,
           w_router, w_gate, w_up, w_down):
    b, s, d = x.shape
    depth = w_ada.shape[0]
    cap = CAPACITY_FACTOR * s // N_EXPERTS
    cos, sin_signed = _rope_tables(s)
    c_pad = jnp.pad(c, ((0, 8 - b), (0, 0)))
    mod = _ada(c_pad, w_ada, b_ada)
    for l in range(depth):
        mod6 = mod[l, :b].reshape(b * 6, 1, d)
        proj, gates = _inproj(x, mod6, w_in, l, cos, sin_signed,
                              q_gain[l].reshape(1, -1), k_gain[l].reshape(1, -1))
        att = _attention(proj, sink[l])
        m_out = _mlstm(proj, gates, b_gates[l].reshape(-1), m_gain[l])
        x1, aff = _outproj(att, m_out, w_out[l].astype(BF16), x, mod6, w_router[l])
        pos_r, gate_r = _select(jnp.transpose(aff, (0, 2, 1)).reshape(b * N_EXPERTS, s), cap)
        to_cols = lambda t: jnp.transpose(t.reshape(b, N_EXPERTS, s), (0, 2, 1))
        pos_c = to_cols(pos_r)
        idx = jnp.transpose(_compact(pos_c, cap), (1, 0, 2)).reshape(N_EXPERTS, b * cap)
        ye = _ffn(idx, x1, mod6, w_gate, w_up, w_down, l)
        x = _combine(pos_c, to_cols(gate_r), ye, x1, mod6)
    return x
```

```python
import functools

import jax
import jax.numpy as jnp
from jax import lax
from jax.experimental import pallas as pl
from jax.experimental.pallas import tpu as pltpu

F32 = jnp.float32
BF16 = jnp.bfloat16
HIGHEST = lax.Precision.HIGHEST

D_MODEL = 2048
ATT_HEAD_DIM = 128
ATT_HEADS = 8
ATT_KV_HEADS = 2
ATT_GROUP = ATT_HEADS // ATT_KV_HEADS
ATT_WIDTH = ATT_HEADS * ATT_HEAD_DIM
ATT_KV_WIDTH = ATT_KV_HEADS * ATT_HEAD_DIM
WINDOW = 128
ATT_BLOCK = 128
ATT_STEP_BLOCKS = 4
ROPE_THETA = 10000.0
M_HEADS = 4
M_V_DIM = 256
M_QK_DIM = 128
M_WIDTH = M_HEADS * M_V_DIM
M_QK_WIDTH = M_HEADS * M_QK_DIM
MLSTM_BLOCK = 256
N_GATES = 4 * M_HEADS
N_EXPERTS = 16
EXPERT_FF = D_MODEL // 2
CAPACITY_FACTOR = 2
EPS = 1e-6
NEG = -1e30

COL_AQ = 0
COL_MV = ATT_WIDTH
COL_MO = COL_MV + M_WIDTH
COL_AK = COL_MO + M_WIDTH
COL_AV = COL_AK + ATT_KV_WIDTH
COL_MQ = COL_AV + ATT_KV_WIDTH
COL_MK = COL_MQ + M_QK_WIDTH
PROJ_WIDTH = COL_MK + M_QK_WIDTH
IN_COL_BLOCKS = (0, 1, 5, 6, 7, 8, 2, 3, 4)

VMEM_LIMIT_BYTES = 56 * 1024 * 1024

ADA_TN = 1024
INPROJ_TM = 1024
INPROJ_TN = 512
OUTPROJ_TM = 1024
OUTPROJ_TN = 512
FFN_UP_STEPS = 4
FFN_DOWN_STEPS = 2
COMBINE_TN = 512
COMBINE_BUILD_ROWS = 256
SELECT_BISECTIONS = 64
SELECT_MIN_UPPER = 1e-30


def _params(*sem):
    return pltpu.CompilerParams(dimension_semantics=sem, vmem_limit_bytes=VMEM_LIMIT_BYTES)


def _ada_kernel(c_ref, w_ref, b_ref, o_ref):
    c = c_ref[...]
    c_act = c * jax.nn.sigmoid(c)
    w = w_ref[0]
    w_hi = w.astype(BF16)
    w_lo = (w - w_hi.astype(F32)).astype(BF16)
    c_hi = c_act.astype(BF16)
    c_lo = (c_act - c_hi.astype(F32)).astype(BF16)
    acc = jnp.dot(c_hi, w_hi, preferred_element_type=F32)
    acc = acc + jnp.dot(c_lo, w_hi, preferred_element_type=F32)
    acc = acc + jnp.dot(c_hi, w_lo, preferred_element_type=F32)
    o_ref[0] = acc + b_ref[0]


def _ada(c_pad, w_ada, b_ada):
    depth, d, n = w_ada.shape
    rows = c_pad.shape[0]
    return pl.pallas_call(
        _ada_kernel,
        grid=(depth, n // ADA_TN),
        in_specs=[
            pl.BlockSpec((rows, d), lambda l, j: (0, 0)),
            pl.BlockSpec((1, d, ADA_TN), lambda l, j: (l, 0, j)),
            pl.BlockSpec((1, 1, ADA_TN), lambda l, j: (l, 0, j)),
        ],
        out_specs=pl.BlockSpec((1, rows, ADA_TN), lambda l, j: (l, 0, j)),
        out_shape=jax.ShapeDtypeStruct((depth, rows, n), F32),
        compiler_params=_params("arbitrary", "arbitrary"),
        name="ada_mod",
    )(c_pad, w_ada, b_ada.reshape(depth, 1, n))


def _inproj_kernel(perm_ref, x_ref, sc_ref, sh_ref, w_ref, wg_ref, cos_ref, sin_ref, qg_ref, kg_ref,
                   o_ref, g_ref, h_scr):
    del perm_ref
    j = pl.program_id(2)

    @pl.when(j == 0)
    def _():
        x = x_ref[0]
        r = lax.rsqrt(jnp.mean(x * x, axis=-1, keepdims=True) + EPS)
        h = (x * r) * (1.0 + sc_ref[0]) + sh_ref[0]
        hb = h.astype(BF16)
        h_scr[...] = hb
        g_ref[0] = jnp.dot(hb, wg_ref[...], preferred_element_type=F32)

    y = jnp.dot(h_scr[...], w_ref[0], preferred_element_type=F32)

    def norm_rope(t, gain):
        tn = t * lax.rsqrt(jnp.mean(t * t, axis=-1, keepdims=True) + EPS) * gain
        return tn * cos_ref[...] + pltpu.roll(tn, ATT_HEAD_DIM // 2, 1) * sin_ref[...]

    heads_per_tile = INPROJ_TN // ATT_HEAD_DIM
    q_tiles = ATT_WIDTH // INPROJ_TN
    kv_tile = COL_AK // INPROJ_TN
    scale = ATT_HEAD_DIM ** -0.5

    @pl.when(j < q_tiles)
    def _():
        for u in range(heads_per_tile):
            sl = slice(u * ATT_HEAD_DIM, (u + 1) * ATT_HEAD_DIM)
            o_ref[0, :, sl] = (norm_rope(y[:, sl], qg_ref[...]) * scale).astype(BF16)

    @pl.when(j == kv_tile)
    def _():
        for u in range(ATT_KV_HEADS):
            sl = slice(u * ATT_HEAD_DIM, (u + 1) * ATT_HEAD_DIM)
            o_ref[0, :, sl] = norm_rope(y[:, sl], kg_ref[...]).astype(BF16)
        o_ref[0, :, ATT_KV_WIDTH:] = y[:, ATT_KV_WIDTH:].astype(BF16)

    @pl.when(jnp.logical_and(j >= q_tiles, j != kv_tile))
    def _():
        o_ref[0] = y.astype(BF16)


def _inproj(x, mod6, w_bf, w_gate, layer, cos, sin_signed, q_gain, k_gain):
    b, s, d = x.shape
    tm, tn = INPROJ_TM, INPROJ_TN
    assert COL_AK % tn == 0 and ATT_WIDTH % tn == 0 and 2 * ATT_KV_WIDTH == tn
    grid_spec = pltpu.PrefetchScalarGridSpec(
        num_scalar_prefetch=1,
        grid=(b, s // tm, PROJ_WIDTH // tn),
        in_specs=[
            pl.BlockSpec((1, tm, d), lambda bi, i, j, perm: (bi, i, 0)),
            pl.BlockSpec((1, 1, d), lambda bi, i, j, perm: (bi * 6 + 1, 0, 0)),
            pl.BlockSpec((1, 1, d), lambda bi, i, j, perm: (bi * 6 + 0, 0, 0)),
            pl.BlockSpec((1, d, tn), lambda bi, i, j, perm: (layer, 0, perm[j])),
            pl.BlockSpec((d, N_GATES), lambda bi, i, j, perm: (0, 0)),
            pl.BlockSpec((tm, ATT_HEAD_DIM), lambda bi, i, j, perm: (i, 0)),
            pl.BlockSpec((tm, ATT_HEAD_DIM), lambda bi, i, j, perm: (i, 0)),
            pl.BlockSpec((1, ATT_HEAD_DIM), lambda bi, i, j, perm: (0, 0)),
            pl.BlockSpec((1, ATT_HEAD_DIM), lambda bi, i, j, perm: (0, 0)),
        ],
        out_specs=[
            pl.BlockSpec((1, tm, tn), lambda bi, i, j, perm: (bi, i, j)),
            pl.BlockSpec((1, tm, N_GATES), lambda bi, i, j, perm: (bi, i, 0)),
        ],
        scratch_shapes=[pltpu.VMEM((tm, d), BF16)],
    )
    return pl.pallas_call(
        _inproj_kernel,
        grid_spec=grid_spec,
        out_shape=[
            jax.ShapeDtypeStruct((b, s, PROJ_WIDTH), BF16),
            jax.ShapeDtypeStruct((b, s, N_GATES), F32),
        ],
        compiler_params=_params("arbitrary", "arbitrary", "arbitrary"),
        name="in_proj",
    )(jnp.asarray(IN_COL_BLOCKS, jnp.int32), x, mod6, mod6, w_bf, w_gate, cos, sin_signed, q_gain, k_gain)


def _attn_kernel(sink_ref, q_ref, kl_ref, kc_ref, kr_ref, vl_ref, vc_ref, vr_ref, o_ref, *, n_steps):
    n = pl.program_id(1)
    L = ATT_BLOCK
    rows = ATT_GROUP * L
    i = lax.broadcasted_iota(jnp.int32, (rows, 3 * L), 0) & (L - 1)
    jj = lax.broadcasted_iota(jnp.int32, (rows, 3 * L), 1)
    in_window = jnp.abs(jj - L - i) <= WINDOW
    first_mask = in_window & ((jj >= L) | (n > 0))
    last_mask = in_window & ((jj < 2 * L) | (n < n_steps - 1))
    rgrp = lax.broadcasted_iota(jnp.int32, (rows, 1), 0) // L
    for kv in range(ATT_KV_HEADS):
        hs = slice(kv * ATT_HEAD_DIM, (kv + 1) * ATT_HEAD_DIM)
        kband = jnp.concatenate([kl_ref[0, :, hs], kc_ref[0, :, hs], kr_ref[0, :, hs]], axis=0)
        vband = jnp.concatenate([vl_ref[0, :, hs], vc_ref[0, :, hs], vr_ref[0, :, hs]], axis=0)
        sink = jnp.zeros((rows, 1), F32)
        for g in range(ATT_GROUP):
            sink = jnp.where(rgrp == g, sink_ref[kv * ATT_GROUP + g], sink)
        for blk in range(ATT_STEP_BLOCKS):
            qrows = slice(blk * L, (blk + 1) * L)
            q = jnp.concatenate(
                [q_ref[0, qrows, (kv * ATT_GROUP + g) * ATT_HEAD_DIM:(kv * ATT_GROUP + g + 1) * ATT_HEAD_DIM]
                 for g in range(ATT_GROUP)], axis=0)
            kb = kband[blk * L:(blk + 3) * L]
            vb = vband[blk * L:(blk + 3) * L]
            valid = first_mask if blk == 0 else (last_mask if blk == ATT_STEP_BLOCKS - 1 else in_window)
            s = lax.dot_general(q, kb, (((1,), (1,)), ((), ())), preferred_element_type=F32)
            s = jnp.where(valid, s, NEG)
            m = jnp.maximum(jnp.max(s, axis=-1, keepdims=True), sink)
            p = jnp.exp(s - m)
            denom = jnp.sum(p, axis=-1, keepdims=True) + jnp.exp(sink - m)
            o = jnp.dot(p.astype(BF16), vb, preferred_element_type=F32) / denom
            for g in range(ATT_GROUP):
                h = kv * ATT_GROUP + g
                o_ref[0, qrows, h * ATT_HEAD_DIM:(h + 1) * ATT_HEAD_DIM] = o[g * L:(g + 1) * L].astype(BF16)


def _attention(proj, sink):
    b, s, _ = proj.shape
    L = ATT_BLOCK
    nb = s // L
    sb = ATT_STEP_BLOCKS
    assert sb >= 2 and nb % sb == 0
    kblk = COL_AK // ATT_KV_WIDTH
    vblk = COL_AV // ATT_KV_WIDTH
    left = lambda col: pl.BlockSpec((1, L, ATT_KV_WIDTH), lambda bi, n: (bi, jnp.maximum(n * sb - 1, 0), col))
    right = lambda col: pl.BlockSpec((1, L, ATT_KV_WIDTH), lambda bi, n: (bi, jnp.minimum((n + 1) * sb, nb - 1), col))
    centre = lambda col: pl.BlockSpec((1, sb * L, ATT_KV_WIDTH), lambda bi, n: (bi, n, col))
    return pl.pallas_call(
        functools.partial(_attn_kernel, n_steps=nb // sb),
        grid=(b, nb // sb),
        in_specs=[
            pl.BlockSpec(memory_space=pltpu.SMEM),
            pl.BlockSpec((1, sb * L, ATT_WIDTH), lambda bi, n: (bi, n, COL_AQ // ATT_WIDTH)),
            left(kblk), centre(kblk), right(kblk),
            left(vblk), centre(vblk), right(vblk),
        ],
        out_specs=pl.BlockSpec((1, sb * L, ATT_WIDTH), lambda bi, n: (bi, n, 0)),
        out_shape=jax.ShapeDtypeStruct((b, s, ATT_WIDTH), BF16),
        compiler_params=_params("arbitrary", "arbitrary"),
        name="window_attn",
    )(sink, proj, proj, proj, proj, proj, proj, proj)


def _split3(x):
    hi = x.astype(BF16)
    rest = x - hi.astype(F32)
    mid = rest.astype(BF16)
    return hi, mid, (rest - mid.astype(F32)).astype(BF16)


def _mlstm_kernel(q_ref, k_ref, v_ref, mo_ref, gcol_ref, grow_ref, bcol_ref, brow_ref, gain_ref, o_ref,
                  bc_scr, kt_scr, ar_scr, wr_scr, bt_scr, c_scr, h_scr, *, seq):
    L = MLSTM_BLOCK
    nc = seq // L
    ns = 2 * M_HEADS
    scale = M_QK_DIM ** -0.5
    row = lax.broadcasted_iota(jnp.int32, (L, L), 0)
    col = lax.broadcasted_iota(jnp.int32, (L, L), 1)
    lower = col <= row
    upper = col >= row
    tril = jnp.where(lower, 1.0, 0.0).astype(BF16)
    triu = jnp.where(upper, 1.0, 0.0).astype(BF16)

    fwd_lane = lax.broadcasted_iota(jnp.int32, (1, N_GATES), 1) < ns + M_HEADS
    for c in range(nc):
        rows = slice(c * L, (c + 1) * L)
        lf = jax.nn.log_sigmoid(gcol_ref[0, rows, :] + bcol_ref[...])
        lf3 = _split3(lf)
        pre = sum(jnp.dot(tril, part, preferred_element_type=F32) for part in lf3)
        suf = sum(jnp.dot(triu, part, preferred_element_type=F32) for part in lf3)
        bcol = jnp.where(fwd_lane, pre, suf)
        for k in range(ns):
            bc_scr[k, rows, :] = jnp.broadcast_to(bcol[:, ns + k:ns + k + 1], (L, 128))
        kt_scr[:, rows] = k_ref[0, rows, :].astype(F32).T.astype(BF16)

    gr = grow_ref[0] + brow_ref[...]
    ig_r = gr[:ns].reshape(ns * nc, L)
    lf_r = jax.nn.log_sigmoid(gr[ns:]).reshape(ns * nc, L)
    lf_r3 = _split3(lf_r)
    pre_r = sum(jnp.dot(part, triu, preferred_element_type=F32) for part in lf_r3)
    suf_r = sum(jnp.dot(part, tril, preferred_element_type=F32) for part in lf_r3)
    fwd_rows = lax.broadcasted_iota(jnp.int32, (ns * nc, 1), 0) < M_HEADS * nc
    a_r = ig_r - jnp.where(fwd_rows, pre_r, suf_r)
    btot = jnp.sum(lf_r, axis=-1, keepdims=True)
    ar_scr[...] = a_r.reshape(ns, nc, L)
    wr_scr[...] = (btot + a_r).reshape(ns, nc, L)
    bt_scr[...] = jnp.broadcast_to(btot, (ns * nc, 128)).reshape(ns, nc, 128)

    c_scr[...] = jnp.zeros_like(c_scr)
    h_scr[...] = jnp.zeros_like(h_scr)
    ones_cols = jnp.ones((L, 128), BF16)

    twice = lambda t: jnp.concatenate([t, t], axis=1)

    def chunk_step(k, c, m_st):
        d, h = divmod(k, M_HEADS)
        r0 = pl.multiple_of(c * L, L)
        q = q_ref[0, pl.ds(r0, L), h * M_QK_DIM:(h + 1) * M_QK_DIM]
        kk = k_ref[0, pl.ds(r0, L), h * M_QK_DIM:(h + 1) * M_QK_DIM]
        v_ext = jnp.concatenate([v_ref[0, pl.ds(r0, L), h * M_V_DIM:(h + 1) * M_V_DIM], ones_cols], axis=1)
        bc = bc_scr[k, pl.ds(r0, L), :]
        ar = ar_scr[k, pl.ds(c, 1), :]
        wr = wr_scr[k, pl.ds(c, 1), :]
        bt = bt_scr[k, pl.ds(c, 1), :]
        dm = jnp.where(lower if d == 0 else upper, twice(bc) + ar, NEG)
        g_inter = bc + m_st
        m_t = jnp.maximum(jnp.broadcast_to(jnp.max(dm, axis=-1, keepdims=True), (L, 128)), g_inter)
        e_inter = jnp.exp(g_inter - m_t) * scale
        s_qk = lax.dot_general(q, kk, (((1,), (1,)), ((), ())), preferred_element_type=F32)
        p = s_qk * scale * jnp.exp(dm - twice(m_t))
        qc = jnp.dot(q, c_scr[k].astype(BF16), preferred_element_type=F32)
        pv = jnp.dot(p.astype(BF16), v_ext, preferred_element_type=F32)
        num = pv[:, :M_V_DIM] + twice(e_inter) * qc[:, :M_V_DIM]
        den = pv[:, M_V_DIM:] + e_inter * qc[:, M_V_DIM:]
        inv = 1.0 / jnp.maximum(jnp.abs(den), jnp.exp(-m_t))
        hsl = (pl.ds(r0, L), slice(h * M_V_DIM, (h + 1) * M_V_DIM))
        h_scr[hsl] = h_scr[hsl] + num * twice(inv)
        m_new = jnp.maximum(bt + m_st, jnp.broadcast_to(jnp.max(wr, axis=-1, keepdims=True), (1, 128)))
        a = jnp.exp(bt + m_st - m_new)
        kt = kt_scr[h * M_QK_DIM:(h + 1) * M_QK_DIM, pl.ds(r0, L)]
        ek_t = (kt.astype(F32) * jnp.exp(wr - twice(m_new))).astype(BF16)
        c_scr[k] = jnp.concatenate([a, a, a], axis=1) * c_scr[k] + jnp.dot(ek_t, v_ext, preferred_element_type=F32)
        return m_new

    def body(it, ms):
        out = []
        for k in range(ns):
            c = it if k < M_HEADS else nc - 1 - it
            out.append(chunk_step(k, c, ms[k]))
        return tuple(out)

    lax.fori_loop(0, nc, body, tuple(jnp.zeros((1, 128), F32) for _ in range(ns)))

    def fin(i, carry):
        r0 = pl.multiple_of(i * L, L)
        for h in range(M_HEADS):
            sl = slice(h * M_V_DIM, (h + 1) * M_V_DIM)
            x = h_scr[pl.ds(r0, L), sl]
            y = x * lax.rsqrt(jnp.mean(x * x, axis=-1, keepdims=True) + EPS) * gain_ref[:, sl]
            o_ref[0, pl.ds(r0, L), sl] = (y * jax.nn.sigmoid(mo_ref[0, pl.ds(r0, L), sl].astype(F32))).astype(BF16)
        return carry

    lax.fori_loop(0, nc, fin, 0)


def _mlstm(proj, gates, b_gates, m_gain):
    b, s, _ = proj.shape
    L = MLSTM_BLOCK
    nc = s // L
    ns = 2 * M_HEADS
    grow = jnp.transpose(gates, (0, 2, 1)).reshape(b, N_GATES, nc, L)
    col = lambda width, off: pl.BlockSpec((1, s, width), lambda bi: (bi, 0, off // width))
    return pl.pallas_call(
        functools.partial(_mlstm_kernel, seq=s),
        grid=(b,),
        in_specs=[
            col(M_QK_WIDTH, COL_MQ), col(M_QK_WIDTH, COL_MK), col(M_WIDTH, COL_MV),
            pl.BlockSpec((1, s, M_WIDTH), lambda bi: (bi, 0, COL_MO // M_WIDTH), pipeline_mode=pl.Buffered(1)),
            pl.BlockSpec((1, s, N_GATES), lambda bi: (bi, 0, 0)),
            pl.BlockSpec((1, N_GATES, nc, L), lambda bi: (bi, 0, 0, 0)),
            pl.BlockSpec((1, N_GATES), lambda bi: (0, 0)),
            pl.BlockSpec((N_GATES, 1, 1), lambda bi: (0, 0, 0)),
            pl.BlockSpec((1, M_WIDTH), lambda bi: (0, 0)),
        ],
        out_specs=pl.BlockSpec((1, s, M_WIDTH), lambda bi: (bi, 0, 0)),
        out_shape=jax.ShapeDtypeStruct((b, s, M_WIDTH), BF16),
        scratch_shapes=[
            pltpu.VMEM((ns, s, 128), F32),
            pltpu.VMEM((M_QK_WIDTH, s), BF16),
            pltpu.VMEM((ns, nc, L), F32), pltpu.VMEM((ns, nc, L), F32), pltpu.VMEM((ns, nc, 128), F32),
            pltpu.VMEM((ns, M_QK_DIM, M_V_DIM + 128), F32),
            pltpu.VMEM((s, M_WIDTH), F32),
        ],
        compiler_params=_params("arbitrary"),
        name="mlstm",
    )(proj, proj, proj, proj, gates, grow, b_gates.reshape(1, N_GATES), b_gates.reshape(N_GATES, 1, 1),
      m_gain.reshape(1, M_WIDTH))


def _outproj_kernel(att_ref, mo_ref, wa_ref, wm_ref, x_ref, g1_ref, sc_ref, sh_ref, wr_ref,
                    x1_ref, h2_ref, aff_ref):
    j = pl.program_id(2)
    nj = pl.num_programs(2)
    tn = OUTPROJ_TN
    mix = (jnp.dot(att_ref[0], wa_ref[...], preferred_element_type=F32)
           + jnp.dot(mo_ref[0], wm_ref[...], preferred_element_type=F32))
    x1t = x_ref[0] + g1_ref[0] * mix
    for jj in range(D_MODEL // tn):
        @pl.when(j == jj)
        def _(jj=jj):
            x1_ref[0, :, jj * tn:(jj + 1) * tn] = x1t

    @pl.when(j == nj - 1)
    def _():
        xf = x1_ref[0]
        r = lax.rsqrt(jnp.mean(xf * xf, axis=-1, keepdims=True) + EPS)
        h2 = (xf * r) * (1.0 + sc_ref[0]) + sh_ref[0]
        h2_hi = h2.astype(BF16)
        h2_ref[0] = h2
        h2_lo = (h2 - h2_hi.astype(F32)).astype(BF16)
        hi_terms = jnp.dot(h2_hi, wr_ref[...], preferred_element_type=F32)
        lo_term = jnp.dot(h2_lo, wr_ref[:, :N_EXPERTS], preferred_element_type=F32)
        logits = hi_terms[:, :N_EXPERTS] + hi_terms[:, N_EXPERTS:] + lo_term
        e = jnp.exp(logits - jnp.max(logits, axis=-1, keepdims=True))
        aff_ref[0] = e / jnp.sum(e, axis=-1, keepdims=True)


def _outproj(att, m_out, w_out, x, mod6, w_router):
    b, s, d = x.shape
    tm, tn = OUTPROJ_TM, OUTPROJ_TN
    half = ATT_WIDTH
    assert w_out.shape[0] == 2 * half and M_WIDTH == half
    wr_hi = w_router.astype(BF16)
    wr_lo = (w_router - wr_hi.astype(F32)).astype(BF16)
    return pl.pallas_call(
        _outproj_kernel,
        grid=(b, s // tm, d // tn),
        in_specs=[
            pl.BlockSpec((1, tm, ATT_WIDTH), lambda bi, i, j: (bi, i, 0)),
            pl.BlockSpec((1, tm, M_WIDTH), lambda bi, i, j: (bi, i, 0)),
            pl.BlockSpec((half, tn), lambda bi, i, j: (0, j)),
            pl.BlockSpec((half, tn), lambda bi, i, j: (1, j)),
            pl.BlockSpec((1, tm, tn), lambda bi, i, j: (bi, i, j)),
            pl.BlockSpec((1, 1, tn), lambda bi, i, j: (bi * 6 + 2, 0, j)),
            pl.BlockSpec((1, 1, d), lambda bi, i, j: (bi * 6 + 4, 0, 0)),
            pl.BlockSpec((1, 1, d), lambda bi, i, j: (bi * 6 + 3, 0, 0)),
            pl.BlockSpec((d, 2 * N_EXPERTS), lambda bi, i, j: (0, 0)),
        ],
        out_specs=[
            pl.BlockSpec((1, tm, d), lambda bi, i, j: (bi, i, 0)),
            pl.BlockSpec((1, tm, d), lambda bi, i, j: (bi, i, 0)),
            pl.BlockSpec((1, tm, N_EXPERTS), lambda bi, i, j: (bi, i, 0)),
        ],
        out_shape=[
            jax.ShapeDtypeStruct((b, s, d), F32),
            jax.ShapeDtypeStruct((b, s, d), F32),
            jax.ShapeDtypeStruct((b, s, N_EXPERTS), F32),
        ],
        compiler_params=_params("arbitrary", "arbitrary", "arbitrary"),
        name="out_proj",
    )(att, m_out, w_out, w_out, x, mod6, mod6, mod6, jnp.concatenate([wr_hi, wr_lo], axis=1))


def _prefix_sum_lanes(x, n):
    lane = lax.broadcasted_iota(jnp.int32, x.shape, 1)
    sh = 1
    while sh < n:
        x = x + jnp.where(lane >= sh, pltpu.roll(x, sh, 1), 0.0)
        sh *= 2
    return x


def _select_kernel(aff_ref, pos_ref, gate_ref, *, cap, seq):
    a = aff_ref[...]
    ne = a.shape[0]

    def body(_, lohi):
        lo, hi = lohi
        mid = jnp.where(lo > 0.0, 0.5 * (lo + hi), hi * (1.0 / 256.0))
        cnt = jnp.sum(jnp.where(a >= mid, 1.0, 0.0), axis=-1, keepdims=True)
        ok = cnt >= cap
        return jnp.where(ok, mid, lo), jnp.where(ok, hi, mid)

    lo0 = jnp.zeros((ne, 1), F32)
    hi0 = jnp.maximum(2.0 * jnp.max(a, axis=-1, keepdims=True), SELECT_MIN_UPPER)
    lo, hi = lax.fori_loop(0, SELECT_BISECTIONS, body, (lo0, hi0))
    above = a >= hi
    band = jnp.logical_and(a >= lo, jnp.logical_not(above))
    n_above = jnp.sum(jnp.where(above, 1.0, 0.0), axis=-1, keepdims=True)
    band_rank = _prefix_sum_lanes(jnp.where(band, 1.0, 0.0), seq)
    sel = jnp.where(above, 1.0, jnp.where(band, jnp.where(band_rank <= cap - n_above, 1.0, 0.0), 0.0))
    pos = _prefix_sum_lanes(sel, seq) - 1.0
    chosen = sel > 0.5
    pos_ref[...] = jnp.where(chosen, pos, -1.0)
    gate_ref[...] = jnp.where(chosen, a, 0.0)


def _select(aff_rows, cap):
    n, s = aff_rows.shape
    spec = pl.BlockSpec((n, s), lambda i: (0, 0))
    return pl.pallas_call(
        functools.partial(_select_kernel, cap=cap, seq=s),
        grid=(1,),
        in_specs=[spec],
        out_specs=[spec, spec],
        out_shape=[jax.ShapeDtypeStruct((n, s), F32), jax.ShapeDtypeStruct((n, s), F32)],
        compiler_params=_params("arbitrary"),
        name="expert_select",
    )(aff_rows)


def _compact_kernel(pos_ref, idx_ref, *, cap):
    seq, ne = pos_ref.shape[1], pos_ref.shape[2]
    rt = COMBINE_BUILD_ROWS
    slot = lax.broadcasted_iota(jnp.int32, (rt, cap), 1).astype(F32)
    tok0 = lax.broadcasted_iota(jnp.int32, (rt, 1), 0).astype(F32)

    def body(i, accs):
        r0 = pl.multiple_of(i * rt, rt)
        pos = pos_ref[0, pl.ds(r0, rt), :]
        tok = tok0 + r0.astype(F32)
        return tuple(
            accs[e] + jnp.sum(jnp.where(slot == pos[:, e:e + 1], tok, 0.0), axis=0, keepdims=True)
            for e in range(ne))

    accs = lax.fori_loop(0, seq // rt, body, tuple(jnp.zeros((1, cap), F32) for _ in range(ne)))
    for e in range(ne):
        idx_ref[0, e:e + 1, :] = accs[e].astype(jnp.int32)


def _compact(pos_cols, cap):
    b, s, ne = pos_cols.shape
    return pl.pallas_call(
        functools.partial(_compact_kernel, cap=cap),
        grid=(b,),
        in_specs=[pl.BlockSpec((1, s, ne), lambda bi: (bi, 0, 0))],
        out_specs=pl.BlockSpec((1, ne, cap), lambda bi: (bi, 0, 0)),
        out_shape=jax.ShapeDtypeStruct((b, ne, cap), jnp.int32),
        compiler_params=_params("arbitrary"),
        name="expert_compact",
    )(pos_cols)


def _ffn_kernel(idx_ref, h2_hbm, wg_ref, wu_ref, wd_ref, ye_ref, x_scr, hid_scr, sem, *, nb, cap):
    e = pl.program_id(0)
    f = pl.program_id(1)
    ne = pl.num_programs(0)
    nf = pl.num_programs(1)
    rows = nb * cap
    per_step = rows // FFN_UP_STEPS
    tf = wg_ref.shape[-1]
    slot = lax.rem(e, 2)
    nslot = 1 - slot
    nxt = jnp.minimum(e + 1, ne - 1)

    def row_copy(expert, r, b, sl):
        tok = idx_ref[expert, r]
        return pltpu.make_async_copy(h2_hbm.at[b, pl.ds(tok, 1), :], x_scr.at[sl, pl.ds(r, 1), :], sem.at[sl])

    def wait_rows(sl):
        pltpu.make_async_copy(x_scr.at[sl], x_scr.at[sl], sem.at[sl]).wait()

    @pl.when(jnp.logical_and(e == 0, f == 0))
    def _():
        def first(r, carry):
            row_copy(0, r, r // cap, 0).start()
            return carry
        lax.fori_loop(0, rows, first, 0)

    for ff in range(FFN_UP_STEPS):
        for cur in range(2):
            @pl.when(jnp.logical_and(f == ff, slot == cur))
            def _(ff=ff, cur=cur):
                if ff == 0:
                    wait_rows(cur)
                for r in range(ff * per_step, (ff + 1) * per_step):
                    row_copy(nxt, r, r // cap, 1 - cur).start(priority=r % 2)
                x = x_scr[cur].astype(BF16)
                g = jnp.dot(x, wg_ref[0, 0].astype(BF16), preferred_element_type=F32)
                u = jnp.dot(x, wu_ref[0, 0].astype(BF16), preferred_element_type=F32)
                hid_scr[:, ff * tf:(ff + 1) * tf] = ((g * jax.nn.sigmoid(g)) * u).astype(BF16)

    @pl.when(f >= FFN_UP_STEPS)
    def _():
        y = jnp.dot(hid_scr[...], wd_ref[0, 0].astype(BF16), preferred_element_type=F32)
        ye_ref[...] = y.reshape(nb, 1, cap, y.shape[-1]).astype(BF16)

    @pl.when(jnp.logical_and(e == ne - 1, f == nf - 1))
    def _():
        wait_rows(nslot)


def _ffn(idx, h2, w_gate, w_up, w_down, layer):
    b, s, d = h2.shape
    ne = idx.shape[0]
    cap = idx.shape[1] // b
    ff = w_gate.shape[-1]
    tf = ff // FFN_UP_STEPS
    dn = d // FFN_DOWN_STEPS
    up = lambda f: jnp.minimum(f, FFN_UP_STEPS - 1)
    down = lambda f: jnp.maximum(f - FFN_UP_STEPS, 0)
    grid_spec = pltpu.PrefetchScalarGridSpec(
        num_scalar_prefetch=1,
        grid=(ne, FFN_UP_STEPS + FFN_DOWN_STEPS),
        in_specs=[
            pl.BlockSpec(memory_space=pl.ANY),
            pl.BlockSpec((1, 1, d, tf), lambda e, f, idx_ref: (layer, e, 0, up(f))),
            pl.BlockSpec((1, 1, d, tf), lambda e, f, idx_ref: (layer, e, 0, up(f))),
            pl.BlockSpec((1, 1, ff, dn), lambda e, f, idx_ref: (layer, e, 0, down(f))),
        ],
        out_specs=pl.BlockSpec((b, 1, cap, dn), lambda e, f, idx_ref: (0, e, 0, down(f))),
        scratch_shapes=[
            pltpu.VMEM((2, b * cap, d), F32),
            pltpu.VMEM((b * cap, ff), BF16),
            pltpu.SemaphoreType.DMA((2,)),
        ],
    )
    return pl.pallas_call(
        functools.partial(_ffn_kernel, nb=b, cap=cap),
        grid_spec=grid_spec,
        out_shape=jax.ShapeDtypeStruct((b, ne, cap, d), BF16),
        compiler_params=_params("arbitrary", "arbitrary"),
        name="expert_ffn",
    )(idx, h2, w_gate, w_up, w_down)


def _combine_kernel(pos_ref, gate_ref, ye_ref, x1_ref, g2_ref, o_ref, scat_scr, *, cap):
    j = pl.program_id(1)
    seq, ne = pos_ref.shape[1], pos_ref.shape[2]
    rt = COMBINE_BUILD_ROWS

    @pl.when(j == 0)
    def _():
        slot = lax.broadcasted_iota(jnp.int32, (rt, cap), 1).astype(F32)

        def build(i, carry):
            r0 = pl.multiple_of(i * rt, rt)
            pos = pos_ref[0, pl.ds(r0, rt), :]
            gate = gate_ref[0, pl.ds(r0, rt), :]
            for e in range(ne):
                scat_scr[pl.ds(r0, rt), e * cap:(e + 1) * cap] = jnp.where(
                    slot == pos[:, e:e + 1], gate[:, e:e + 1], 0.0).astype(BF16)
            return carry

        lax.fori_loop(0, seq // rt, build, 0)

    acc = jnp.dot(scat_scr[...], ye_ref[0], preferred_element_type=F32)
    o_ref[0] = x1_ref[0] + g2_ref[0] * acc


def _combine(pos_cols, gate_cols, ye, x1, mod6):
    b, s, d = x1.shape
    ne, cap = ye.shape[1], ye.shape[2]
    tn = COMBINE_TN
    return pl.pallas_call(
        functools.partial(_combine_kernel, cap=cap),
        grid=(b, d // tn),
        in_specs=[
            pl.BlockSpec((1, s, ne), lambda bi, j: (bi, 0, 0)),
            pl.BlockSpec((1, s, ne), lambda bi, j: (bi, 0, 0)),
            pl.BlockSpec((1, ne * cap, tn), lambda bi, j: (bi, 0, j)),
            pl.BlockSpec((1, s, tn), lambda bi, j: (bi, 0, j)),
            pl.BlockSpec((1, 1, tn), lambda bi, j: (bi * 6 + 5, 0, j)),
        ],
        out_specs=pl.BlockSpec((1, s, tn), lambda bi, j: (bi, 0, j)),
        out_shape=jax.ShapeDtypeStruct((b, s, d), F32),
        scratch_shapes=[pltpu.VMEM((s, ne * cap), BF16)],
        compiler_params=_params("arbitrary", "arbitrary"),
        name="expert_combine",
    )(pos_cols, gate_cols, ye.reshape(b, ne * cap, d), x1, mod6)


def _rope_tables(seq):
    inv = 1.0 / (ROPE_THETA ** (jnp.arange(0, ATT_HEAD_DIM, 2, dtype=F32) / ATT_HEAD_DIM))
    ang = jnp.arange(seq, dtype=F32)[:, None] * inv[None, :]
    ang = jnp.concatenate([ang, ang], axis=-1)
    sign = jnp.where(jnp.arange(ATT_HEAD_DIM) < ATT_HEAD_DIM // 2, -1.0, 1.0).astype(F32)
    return jnp.cos(ang), jnp.sin(ang) * sign


def kernel(x, c, w_ada, b_ada, w_in, b_gates, q_gain, k_gain, sink, m_gain, w_out,
           w_router, w_gate, w_up, w_down):
    b, s, d = x.shape
    depth = w_ada.shape[0]
    cap = CAPACITY_FACTOR * s // N_EXPERTS
    cos, sin_signed = _rope_tables(s)
    c_pad = jnp.pad(c, ((0, 8 - b), (0, 0)))
    mod = _ada(c_pad, w_ada, b_ada)
    w_in_bf = w_in.astype(BF16)
    for l in range(depth):
        mod6 = mod[l, :b].reshape(b * 6, 1, d)
        proj, gates = _inproj(x, mod6, w_in_bf, w_in[l, :, PROJ_WIDTH:].astype(BF16), l, cos, sin_signed,
                              q_gain[l].reshape(1, -1), k_gain[l].reshape(1, -1))
        att = _attention(proj, sink[l])
        m_out = _mlstm(proj, gates, b_gates[l].reshape(-1), m_gain[l])
        x1, h2, aff = _outproj(att, m_out, w_out[l].astype(BF16), x, mod6, w_router[l])
        pos_r, gate_r = _select(jnp.transpose(aff, (0, 2, 1)).reshape(b * N_EXPERTS, s), cap)
        to_cols = lambda t: jnp.transpose(t.reshape(b, N_EXPERTS, s), (0, 2, 1))
        pos_c = to_cols(pos_r)
        idx = jnp.transpose(_compact(pos_c, cap), (1, 0, 2)).reshape(N_EXPERTS, b * cap)
        ye = _ffn(idx, h2, w_gate, w_up, w_down, l)
        x = _combine(pos_c, to_cols(gate_r), ye, x1, mod6)
    return x
```

```python
import functools

import jax
import jax.numpy as jnp
from jax import lax
from jax.experimental import pallas as pl
from jax.experimental.pallas import tpu as pltpu

F32 = jnp.float32
BF16 = jnp.bfloat16
SUBLANES = 8
LANES = 128

D_MODEL = 2048
ATT_HEAD_DIM = 128
ATT_HEADS = 8
ATT_KV_HEADS = 2
ATT_GROUP = ATT_HEADS // ATT_KV_HEADS
ATT_WIDTH = ATT_HEADS * ATT_HEAD_DIM
ATT_KV_WIDTH = ATT_KV_HEADS * ATT_HEAD_DIM
WINDOW = 128
ATT_BLOCK = 128
ATT_STEP_BLOCKS = 4
ROPE_THETA = 10000.0
M_HEADS = 4
M_V_DIM = 256
M_QK_DIM = 128
M_WIDTH = M_HEADS * M_V_DIM
M_QK_WIDTH = M_HEADS * M_QK_DIM
MLSTM_BLOCK = 256
N_GATES = 4 * M_HEADS
N_EXPERTS = 16
EXPERT_FF = D_MODEL // 2
CAPACITY_FACTOR = 2
EPS = 1e-6
NEG = -1e30

COL_AQ = 0
COL_MV = ATT_WIDTH
COL_MO = COL_MV + M_WIDTH
COL_AK = COL_MO + M_WIDTH
COL_AV = COL_AK + ATT_KV_WIDTH
COL_MQ = COL_AV + ATT_KV_WIDTH
COL_MK = COL_MQ + M_QK_WIDTH
PROJ_WIDTH = COL_MK + M_QK_WIDTH
IN_COL_BLOCKS = (0, 1, 5, 6, 7, 8, 2, 3, 4)

VMEM_LIMIT_BYTES = 56 * 1024 * 1024

ADA_TN = 1024
INPROJ_TM = 1024
INPROJ_TN = 512
OUTPROJ_TM = 1024
OUTPROJ_TN = 512
FFN_UP_STEPS = 4
FFN_DOWN_STEPS = 2
COMBINE_TN = 512
COMBINE_BUILD_ROWS = 256
SELECT_BISECTIONS = 64
SELECT_MIN_UPPER = 1e-30


def _params(*sem):
    return pltpu.CompilerParams(dimension_semantics=sem, vmem_limit_bytes=VMEM_LIMIT_BYTES)


def _ada_kernel(c_ref, w_ref, b_ref, o_ref):
    c = c_ref[...]
    c_act = c * jax.nn.sigmoid(c)
    w = w_ref[0]
    w_hi = w.astype(BF16)
    w_lo = (w - w_hi.astype(F32)).astype(BF16)
    c_hi = c_act.astype(BF16)
    c_lo = (c_act - c_hi.astype(F32)).astype(BF16)
    acc = jnp.dot(c_hi, w_hi, preferred_element_type=F32)
    acc = acc + jnp.dot(c_lo, w_hi, preferred_element_type=F32)
    acc = acc + jnp.dot(c_hi, w_lo, preferred_element_type=F32)
    o_ref[0] = acc + b_ref[0]


def _ada(c_pad, w_ada, b_ada):
    depth, d, n = w_ada.shape
    rows = c_pad.shape[0]
    return pl.pallas_call(
        _ada_kernel,
        grid=(depth, n // ADA_TN),
        in_specs=[
            pl.BlockSpec((rows, d), lambda l, j: (0, 0)),
            pl.BlockSpec((1, d, ADA_TN), lambda l, j: (l, 0, j)),
            pl.BlockSpec((1, 1, ADA_TN), lambda l, j: (l, 0, j)),
        ],
        out_specs=pl.BlockSpec((1, rows, ADA_TN), lambda l, j: (l, 0, j)),
        out_shape=jax.ShapeDtypeStruct((depth, rows, n), F32),
        compiler_params=_params("arbitrary", "arbitrary"),
        name="ada_mod",
    )(c_pad, w_ada, b_ada.reshape(depth, 1, n))


def _inproj_kernel(perm_ref, x_ref, sc_ref, sh_ref, w_ref, wg_ref, cos_ref, sin_ref, qg_ref, kg_ref,
                   o_ref, g_ref, h_scr):
    del perm_ref
    j = pl.program_id(2)

    @pl.when(j == 0)
    def _():
        x = x_ref[0]
        r = lax.rsqrt(jnp.mean(x * x, axis=-1, keepdims=True) + EPS)
        h = (x * r) * (1.0 + sc_ref[0]) + sh_ref[0]
        hb = h.astype(BF16)
        h_scr[...] = hb
        g_ref[0] = jnp.dot(hb, wg_ref[...], preferred_element_type=F32)

    y = jnp.dot(h_scr[...], w_ref[0], preferred_element_type=F32)

    head_ones = jnp.ones((ATT_HEAD_DIM, ATT_HEAD_DIM), BF16)

    def norm_rope(t, gain):
        ss = jnp.dot((t * t).astype(BF16), head_ones, preferred_element_type=F32)
        tn = t * lax.rsqrt(ss * (1.0 / ATT_HEAD_DIM) + EPS) * gain
        return tn * cos_ref[...] + pltpu.roll(tn, ATT_HEAD_DIM // 2, 1) * sin_ref[...]

    heads_per_tile = INPROJ_TN // ATT_HEAD_DIM
    q_tiles = ATT_WIDTH // INPROJ_TN
    kv_tile = COL_AK // INPROJ_TN
    scale = ATT_HEAD_DIM ** -0.5

    @pl.when(j < q_tiles)
    def _():
        for u in range(heads_per_tile):
            sl = slice(u * ATT_HEAD_DIM, (u + 1) * ATT_HEAD_DIM)
            o_ref[0, :, sl] = (norm_rope(y[:, sl], qg_ref[...]) * scale).astype(BF16)

    @pl.when(j == kv_tile)
    def _():
        for u in range(ATT_KV_HEADS):
            sl = slice(u * ATT_HEAD_DIM, (u + 1) * ATT_HEAD_DIM)
            o_ref[0, :, sl] = norm_rope(y[:, sl], kg_ref[...]).astype(BF16)
        o_ref[0, :, ATT_KV_WIDTH:] = y[:, ATT_KV_WIDTH:].astype(BF16)

    @pl.when(jnp.logical_and(j >= q_tiles, j != kv_tile))
    def _():
        o_ref[0] = y.astype(BF16)


def _inproj(x, mod6, w_bf, w_gate, layer, cos, sin_signed, q_gain, k_gain):
    b, s, d = x.shape
    tm, tn = INPROJ_TM, INPROJ_TN
    assert COL_AK % tn == 0 and ATT_WIDTH % tn == 0 and 2 * ATT_KV_WIDTH == tn
    grid_spec = pltpu.PrefetchScalarGridSpec(
        num_scalar_prefetch=1,
        grid=(b, s // tm, PROJ_WIDTH // tn),
        in_specs=[
            pl.BlockSpec((1, tm, d), lambda bi, i, j, perm: (bi, i, 0)),
            pl.BlockSpec((1, 1, d), lambda bi, i, j, perm: (bi * 6 + 1, 0, 0)),
            pl.BlockSpec((1, 1, d), lambda bi, i, j, perm: (bi * 6 + 0, 0, 0)),
            pl.BlockSpec((1, d, tn), lambda bi, i, j, perm: (layer, 0, perm[j])),
            pl.BlockSpec((d, N_GATES), lambda bi, i, j, perm: (0, 0)),
            pl.BlockSpec((tm, ATT_HEAD_DIM), lambda bi, i, j, perm: (i, 0)),
            pl.BlockSpec((tm, ATT_HEAD_DIM), lambda bi, i, j, perm: (i, 0)),
            pl.BlockSpec((1, ATT_HEAD_DIM), lambda bi, i, j, perm: (0, 0)),
            pl.BlockSpec((1, ATT_HEAD_DIM), lambda bi, i, j, perm: (0, 0)),
        ],
        out_specs=[
            pl.BlockSpec((1, tm, tn), lambda bi, i, j, perm: (bi, i, j)),
            pl.BlockSpec((1, tm, N_GATES), lambda bi, i, j, perm: (bi, i, 0)),
        ],
        scratch_shapes=[pltpu.VMEM((tm, d), BF16)],
    )
    return pl.pallas_call(
        _inproj_kernel,
        grid_spec=grid_spec,
        out_shape=[
            jax.ShapeDtypeStruct((b, s, PROJ_WIDTH), BF16),
            jax.ShapeDtypeStruct((b, s, N_GATES), F32),
        ],
        compiler_params=_params("arbitrary", "arbitrary", "arbitrary"),
        name="in_proj",
    )(jnp.asarray(IN_COL_BLOCKS, jnp.int32), x, mod6, mod6, w_bf, w_gate, cos, sin_signed, q_gain, k_gain)


def _attn_kernel(sink_ref, q_ref, kl_ref, kc_ref, kr_ref, vl_ref, vc_ref, vr_ref, o_ref, *, n_steps):
    n = pl.program_id(1)
    L = ATT_BLOCK
    rows = ATT_GROUP * L
    i = lax.broadcasted_iota(jnp.int32, (rows, 3 * L), 0) & (L - 1)
    jj = lax.broadcasted_iota(jnp.int32, (rows, 3 * L), 1)
    in_window = jnp.abs(jj - L - i) <= WINDOW
    first_mask = in_window & ((jj >= L) | (n > 0))
    last_mask = in_window & ((jj < 2 * L) | (n < n_steps - 1))
    rgrp = lax.broadcasted_iota(jnp.int32, (rows, 1), 0) // L
    for kv in range(ATT_KV_HEADS):
        hs = slice(kv * ATT_HEAD_DIM, (kv + 1) * ATT_HEAD_DIM)
        kband = jnp.concatenate([kl_ref[0, :, hs], kc_ref[0, :, hs], kr_ref[0, :, hs]], axis=0)
        vband = jnp.concatenate([vl_ref[0, :, hs], vc_ref[0, :, hs], vr_ref[0, :, hs]], axis=0)
        sink = jnp.zeros((rows, 1), F32)
        for g in range(ATT_GROUP):
            sink = jnp.where(rgrp == g, sink_ref[kv * ATT_GROUP + g], sink)
        for blk in range(ATT_STEP_BLOCKS):
            qrows = slice(blk * L, (blk + 1) * L)
            q = jnp.concatenate(
                [q_ref[0, qrows, (kv * ATT_GROUP + g) * ATT_HEAD_DIM:(kv * ATT_GROUP + g + 1) * ATT_HEAD_DIM]
                 for g in range(ATT_GROUP)], axis=0)
            kb = kband[blk * L:(blk + 3) * L]
            vb = vband[blk * L:(blk + 3) * L]
            valid = first_mask if blk == 0 else (last_mask if blk == ATT_STEP_BLOCKS - 1 else in_window)
            s = lax.dot_general(q, kb, (((1,), (1,)), ((), ())), preferred_element_type=F32)
            s = jnp.where(valid, s, NEG)
            m = jnp.maximum(jnp.max(s, axis=-1, keepdims=True), sink)
            p = jnp.exp(s - m)
            denom = jnp.sum(p, axis=-1, keepdims=True) + jnp.exp(sink - m)
            o = jnp.dot(p.astype(BF16), vb, preferred_element_type=F32) / denom
            for g in range(ATT_GROUP):
                h = kv * ATT_GROUP + g
                o_ref[0, qrows, h * ATT_HEAD_DIM:(h + 1) * ATT_HEAD_DIM] = o[g * L:(g + 1) * L].astype(BF16)


def _attention(proj, sink):
    b, s, _ = proj.shape
    L = ATT_BLOCK
    nb = s // L
    sb = ATT_STEP_BLOCKS
    assert sb >= 2 and nb % sb == 0
    kblk = COL_AK // ATT_KV_WIDTH
    vblk = COL_AV // ATT_KV_WIDTH
    left = lambda col: pl.BlockSpec((1, L, ATT_KV_WIDTH), lambda bi, n: (bi, jnp.maximum(n * sb - 1, 0), col))
    right = lambda col: pl.BlockSpec((1, L, ATT_KV_WIDTH), lambda bi, n: (bi, jnp.minimum((n + 1) * sb, nb - 1), col))
    centre = lambda col: pl.BlockSpec((1, sb * L, ATT_KV_WIDTH), lambda bi, n: (bi, n, col))
    return pl.pallas_call(
        functools.partial(_attn_kernel, n_steps=nb // sb),
        grid=(b, nb // sb),
        in_specs=[
            pl.BlockSpec(memory_space=pltpu.SMEM),
            pl.BlockSpec((1, sb * L, ATT_WIDTH), lambda bi, n: (bi, n, COL_AQ // ATT_WIDTH)),
            left(kblk), centre(kblk), right(kblk),
            left(vblk), centre(vblk), right(vblk),
        ],
        out_specs=pl.BlockSpec((1, sb * L, ATT_WIDTH), lambda bi, n: (bi, n, 0)),
        out_shape=jax.ShapeDtypeStruct((b, s, ATT_WIDTH), BF16),
        compiler_params=_params("arbitrary", "arbitrary"),
        name="window_attn",
    )(sink, proj, proj, proj, proj, proj, proj, proj)


def _split3(x):
    hi = x.astype(BF16)
    rest = x - hi.astype(F32)
    mid = rest.astype(BF16)
    return hi, mid, (rest - mid.astype(F32)).astype(BF16)


def _mlstm_kernel(q_ref, k_ref, v_ref, mo_ref, gcol_ref, grow_ref, bcol_ref, brow_ref, gain_ref, o_ref,
                  bc_scr, kt_scr, ar_scr, wr_scr, bt_scr, c_scr, h_scr, *, seq):
    L = MLSTM_BLOCK
    nc = seq // L
    ns = 2 * M_HEADS
    scale = M_QK_DIM ** -0.5
    row = lax.broadcasted_iota(jnp.int32, (L, L), 0)
    col = lax.broadcasted_iota(jnp.int32, (L, L), 1)
    lower = col <= row
    upper = col >= row
    tril = jnp.where(lower, 1.0, 0.0).astype(BF16)
    triu = jnp.where(upper, 1.0, 0.0).astype(BF16)

    fwd_lane = lax.broadcasted_iota(jnp.int32, (1, N_GATES), 1) < ns + M_HEADS
    for c in range(nc):
        rows = slice(c * L, (c + 1) * L)
        lf = jax.nn.log_sigmoid(gcol_ref[0, rows, :] + bcol_ref[...])
        lf3 = _split3(lf)
        pre = sum(jnp.dot(tril, part, preferred_element_type=F32) for part in lf3)
        suf = sum(jnp.dot(triu, part, preferred_element_type=F32) for part in lf3)
        bcol = jnp.where(fwd_lane, pre, suf)
        for k in range(ns):
            bc_scr[k, rows, :] = jnp.broadcast_to(bcol[:, ns + k:ns + k + 1], (L, LANES))
        kt_scr[:, rows] = k_ref[0, rows, :].astype(F32).T.astype(BF16)

    gr = grow_ref[0] + brow_ref[...]
    ig_r = gr[:ns].reshape(ns * nc, L)
    lf_r = jax.nn.log_sigmoid(gr[ns:]).reshape(ns * nc, L)
    lf_r3 = _split3(lf_r)
    pre_r = sum(jnp.dot(part, triu, preferred_element_type=F32) for part in lf_r3)
    suf_r = sum(jnp.dot(part, tril, preferred_element_type=F32) for part in lf_r3)
    fwd_rows = lax.broadcasted_iota(jnp.int32, (ns * nc, 1), 0) < M_HEADS * nc
    a_r = ig_r - jnp.where(fwd_rows, pre_r, suf_r)
    btot = jnp.sum(lf_r, axis=-1, keepdims=True)
    ar_scr[...] = a_r.reshape(ns, nc, L)
    wr_scr[...] = (btot + a_r).reshape(ns, nc, L)
    bt_scr[...] = jnp.broadcast_to(btot, (ns * nc, LANES)).reshape(ns, nc, LANES)

    c_scr[...] = jnp.zeros_like(c_scr)
    h_scr[...] = jnp.zeros_like(h_scr)
    ones_cols = jnp.ones((L, LANES), BF16)

    twice = lambda t: jnp.concatenate([t, t], axis=1)

    def chunk_step(k, c, m_st):
        d, h = divmod(k, M_HEADS)
        r0 = pl.multiple_of(c * L, L)
        q = q_ref[0, pl.ds(r0, L), h * M_QK_DIM:(h + 1) * M_QK_DIM]
        kk = k_ref[0, pl.ds(r0, L), h * M_QK_DIM:(h + 1) * M_QK_DIM]
        v_ext = jnp.concatenate([v_ref[0, pl.ds(r0, L), h * M_V_DIM:(h + 1) * M_V_DIM], ones_cols], axis=1)
        bc = bc_scr[k, pl.ds(r0, L), :]
        ar = ar_scr[k, pl.ds(c, 1), :]
        wr = wr_scr[k, pl.ds(c, 1), :]
        bt = bt_scr[k, pl.ds(c, 1), :]
        dm = jnp.where(lower if d == 0 else upper, twice(bc) + ar, NEG)
        g_inter = bc + m_st
        m_t = jnp.maximum(jnp.broadcast_to(jnp.max(dm, axis=-1, keepdims=True), (L, LANES)), g_inter)
        e_inter = jnp.exp(g_inter - m_t) * scale
        s_qk = lax.dot_general(q, kk, (((1,), (1,)), ((), ())), preferred_element_type=F32)
        p = s_qk * scale * jnp.exp(dm - twice(m_t))
        qc = jnp.dot(q, c_scr[k].astype(BF16), preferred_element_type=F32)
        pv = jnp.dot(p.astype(BF16), v_ext, preferred_element_type=F32)
        num = pv[:, :M_V_DIM] + twice(e_inter) * qc[:, :M_V_DIM]
        den = pv[:, M_V_DIM:] + e_inter * qc[:, M_V_DIM:]
        inv = 1.0 / jnp.maximum(jnp.abs(den), jnp.exp(-m_t))
        hsl = (pl.ds(r0, L), slice(h * M_V_DIM, (h + 1) * M_V_DIM))
        h_scr[hsl] = h_scr[hsl] + num * twice(inv)
        m_new = jnp.maximum(bt + m_st, jnp.broadcast_to(jnp.max(wr, axis=-1, keepdims=True), (1, LANES)))
        a = jnp.exp(bt + m_st - m_new)
        kt = kt_scr[h * M_QK_DIM:(h + 1) * M_QK_DIM, pl.ds(r0, L)]
        ek_t = (kt.astype(F32) * jnp.exp(wr - twice(m_new))).astype(BF16)
        c_scr[k] = jnp.concatenate([a, a, a], axis=1) * c_scr[k] + jnp.dot(ek_t, v_ext, preferred_element_type=F32)
        return m_new

    def body(it, ms):
        out = []
        for k in range(ns):
            c = it if k < M_HEADS else nc - 1 - it
            out.append(chunk_step(k, c, ms[k]))
        return tuple(out)

    lax.fori_loop(0, nc, body, tuple(jnp.zeros((1, LANES), F32) for _ in range(ns)))

    def fin(i, carry):
        r0 = pl.multiple_of(i * L, L)
        for h in range(M_HEADS):
            sl = slice(h * M_V_DIM, (h + 1) * M_V_DIM)
            x = h_scr[pl.ds(r0, L), sl]
            y = x * lax.rsqrt(jnp.mean(x * x, axis=-1, keepdims=True) + EPS) * gain_ref[:, sl]
            o_ref[0, pl.ds(r0, L), sl] = (y * jax.nn.sigmoid(mo_ref[0, pl.ds(r0, L), sl].astype(F32))).astype(BF16)
        return carry

    lax.fori_loop(0, nc, fin, 0)


def _mlstm(proj, gates, b_gates, m_gain):
    b, s, _ = proj.shape
    L = MLSTM_BLOCK
    nc = s // L
    ns = 2 * M_HEADS
    grow = jnp.transpose(gates, (0, 2, 1)).reshape(b, N_GATES, nc, L)
    col = lambda width, off: pl.BlockSpec((1, s, width), lambda bi: (bi, 0, off // width))
    return pl.pallas_call(
        functools.partial(_mlstm_kernel, seq=s),
        grid=(b,),
        in_specs=[
            col(M_QK_WIDTH, COL_MQ), col(M_QK_WIDTH, COL_MK), col(M_WIDTH, COL_MV),
            pl.BlockSpec((1, s, M_WIDTH), lambda bi: (bi, 0, COL_MO // M_WIDTH), pipeline_mode=pl.Buffered(1)),
            pl.BlockSpec((1, s, N_GATES), lambda bi: (bi, 0, 0)),
            pl.BlockSpec((1, N_GATES, nc, L), lambda bi: (bi, 0, 0, 0)),
            pl.BlockSpec((1, N_GATES), lambda bi: (0, 0)),
            pl.BlockSpec((N_GATES, 1, 1), lambda bi: (0, 0, 0)),
            pl.BlockSpec((1, M_WIDTH), lambda bi: (0, 0)),
        ],
        out_specs=pl.BlockSpec((1, s, M_WIDTH), lambda bi: (bi, 0, 0)),
        out_shape=jax.ShapeDtypeStruct((b, s, M_WIDTH), BF16),
        scratch_shapes=[
            pltpu.VMEM((ns, s, LANES), F32),
            pltpu.VMEM((M_QK_WIDTH, s), BF16),
            pltpu.VMEM((ns, nc, L), F32), pltpu.VMEM((ns, nc, L), F32), pltpu.VMEM((ns, nc, LANES), F32),
            pltpu.VMEM((ns, M_QK_DIM, M_V_DIM + LANES), F32),
            pltpu.VMEM((s, M_WIDTH), F32),
        ],
        compiler_params=_params("arbitrary"),
        name="mlstm",
    )(proj, proj, proj, proj, gates, grow, b_gates.reshape(1, N_GATES), b_gates.reshape(N_GATES, 1, 1),
      m_gain.reshape(1, M_WIDTH))


def _outproj_kernel(att_ref, mo_ref, wa_ref, wm_ref, x_ref, g1_ref, sc_ref, sh_ref, wr_ref,
                    x1_ref, h2_ref, aff_ref):
    j = pl.program_id(2)
    nj = pl.num_programs(2)
    tn = OUTPROJ_TN
    mix = (jnp.dot(att_ref[0], wa_ref[...], preferred_element_type=F32)
           + jnp.dot(mo_ref[0], wm_ref[...], preferred_element_type=F32))
    x1t = x_ref[0] + g1_ref[0] * mix
    for jj in range(D_MODEL // tn):
        @pl.when(j == jj)
        def _(jj=jj):
            x1_ref[0, :, jj * tn:(jj + 1) * tn] = x1t

    @pl.when(j == nj - 1)
    def _():
        xf = x1_ref[0]
        r = lax.rsqrt(jnp.mean(xf * xf, axis=-1, keepdims=True) + EPS)
        h2 = (xf * r) * (1.0 + sc_ref[0]) + sh_ref[0]
        h2_hi = h2.astype(BF16)
        h2_ref[0] = h2
        h2_lo = (h2 - h2_hi.astype(F32)).astype(BF16)
        hi_terms = jnp.dot(h2_hi, wr_ref[...], preferred_element_type=F32)
        lo_term = jnp.dot(h2_lo, wr_ref[:, :N_EXPERTS], preferred_element_type=F32)
        logits = hi_terms[:, :N_EXPERTS] + hi_terms[:, N_EXPERTS:] + lo_term
        e = jnp.exp(logits - jnp.max(logits, axis=-1, keepdims=True))
        aff_ref[0] = e / jnp.sum(e, axis=-1, keepdims=True)


def _outproj(att, m_out, w_out, x, mod6, w_router):
    b, s, d = x.shape
    tm, tn = OUTPROJ_TM, OUTPROJ_TN
    half = ATT_WIDTH
    assert w_out.shape[0] == 2 * half and M_WIDTH == half
    wr_hi = w_router.astype(BF16)
    wr_lo = (w_router - wr_hi.astype(F32)).astype(BF16)
    return pl.pallas_call(
        _outproj_kernel,
        grid=(b, s // tm, d // tn),
        in_specs=[
            pl.BlockSpec((1, tm, ATT_WIDTH), lambda bi, i, j: (bi, i, 0)),
            pl.BlockSpec((1, tm, M_WIDTH), lambda bi, i, j: (bi, i, 0)),
            pl.BlockSpec((half, tn), lambda bi, i, j: (0, j)),
            pl.BlockSpec((half, tn), lambda bi, i, j: (1, j)),
            pl.BlockSpec((1, tm, tn), lambda bi, i, j: (bi, i, j)),
            pl.BlockSpec((1, 1, tn), lambda bi, i, j: (bi * 6 + 2, 0, j)),
            pl.BlockSpec((1, 1, d), lambda bi, i, j: (bi * 6 + 4, 0, 0)),
            pl.BlockSpec((1, 1, d), lambda bi, i, j: (bi * 6 + 3, 0, 0)),
            pl.BlockSpec((d, 2 * N_EXPERTS), lambda bi, i, j: (0, 0)),
        ],
        out_specs=[
            pl.BlockSpec((1, tm, d), lambda bi, i, j: (bi, i, 0)),
            pl.BlockSpec((1, tm, d), lambda bi, i, j: (bi, i, 0)),
            pl.BlockSpec((1, tm, N_EXPERTS), lambda bi, i, j: (bi, i, 0)),
        ],
        out_shape=[
            jax.ShapeDtypeStruct((b, s, d), F32),
            jax.ShapeDtypeStruct((b, s, d), F32),
            jax.ShapeDtypeStruct((b, s, N_EXPERTS), F32),
        ],
        compiler_params=_params("arbitrary", "arbitrary", "arbitrary"),
        name="out_proj",
    )(att, m_out, w_out, w_out, x, mod6, mod6, mod6, jnp.concatenate([wr_hi, wr_lo], axis=1))


def _prefix_sum_lanes(x, n):
    lane = lax.broadcasted_iota(jnp.int32, x.shape, 1)
    sh = 1
    while sh < n:
        x = x + jnp.where(lane >= sh, pltpu.roll(x, sh, 1), 0.0)
        sh *= 2
    return x


def _select_kernel(aff_ref, pos_ref, gate_ref, *, cap, seq):
    a = aff_ref[...]
    ne = a.shape[0]

    def body(_, lohi):
        lo, hi = lohi
        mid = jnp.where(lo > 0.0, 0.5 * (lo + hi), hi * (1.0 / 256.0))
        cnt = jnp.sum(jnp.where(a >= mid, 1.0, 0.0), axis=-1, keepdims=True)
        ok = cnt >= cap
        return jnp.where(ok, mid, lo), jnp.where(ok, hi, mid)

    lo0 = jnp.zeros((ne, 1), F32)
    hi0 = jnp.maximum(2.0 * jnp.max(a, axis=-1, keepdims=True), SELECT_MIN_UPPER)
    lo, hi = lax.fori_loop(0, SELECT_BISECTIONS, body, (lo0, hi0))
    above = a >= hi
    band = jnp.logical_and(a >= lo, jnp.logical_not(above))
    n_above = jnp.sum(jnp.where(above, 1.0, 0.0), axis=-1, keepdims=True)
    band_rank = _prefix_sum_lanes(jnp.where(band, 1.0, 0.0), seq)
    sel = jnp.where(above, 1.0, jnp.where(band, jnp.where(band_rank <= cap - n_above, 1.0, 0.0), 0.0))
    pos = _prefix_sum_lanes(sel, seq) - 1.0
    chosen = sel > 0.5
    pos_ref[...] = jnp.where(chosen, pos, -1.0)
    gate_ref[...] = jnp.where(chosen, a, 0.0)


def _select(aff_rows, cap):
    n, s = aff_rows.shape
    spec = pl.BlockSpec((n, s), lambda i: (0, 0))
    return pl.pallas_call(
        functools.partial(_select_kernel, cap=cap, seq=s),
        grid=(1,),
        in_specs=[spec],
        out_specs=[spec, spec],
        out_shape=[jax.ShapeDtypeStruct((n, s), F32), jax.ShapeDtypeStruct((n, s), F32)],
        compiler_params=_params("arbitrary"),
        name="expert_select",
    )(aff_rows)


def _lane_spread(ne):
    blk = lax.broadcasted_iota(jnp.int32, (ne, ne * LANES), 1) // LANES
    return jnp.where(blk == lax.broadcasted_iota(jnp.int32, (ne, ne * LANES), 0), 1.0, 0.0).astype(BF16)


def _compact_kernel(pos_ref, idx_ref, *, cap):
    seq, ne = pos_ref.shape[1], pos_ref.shape[2]
    rt = COMBINE_BUILD_ROWS
    halves = cap // LANES
    spread = _lane_spread(ne)
    lane = lax.broadcasted_iota(jnp.int32, (rt, LANES), 1).astype(F32)
    tok0 = lax.broadcasted_iota(jnp.int32, (rt, LANES), 0).astype(F32)

    def body(i, accs):
        r0 = pl.multiple_of(i * rt, rt)
        pos = jnp.dot(pos_ref[0, pl.ds(r0, rt), :].astype(BF16), spread, preferred_element_type=F32)
        tok = tok0 + r0.astype(F32)
        out = []
        for e in range(ne):
            p = pos[:, e * LANES:(e + 1) * LANES]
            for h in range(halves):
                hit = lane + float(h * LANES) == p
                out.append(accs[e * halves + h] + jnp.sum(jnp.where(hit, tok, 0.0), axis=0, keepdims=True))
        return tuple(out)

    accs = lax.fori_loop(0, seq // rt, body, tuple(jnp.zeros((1, LANES), F32) for _ in range(ne * halves)))
    for e in range(ne):
        for h in range(halves):
            idx_ref[0, e:e + 1, h * LANES:(h + 1) * LANES] = accs[e * halves + h].astype(jnp.int32)


def _compact(pos_cols, cap):
    b, s, ne = pos_cols.shape
    return pl.pallas_call(
        functools.partial(_compact_kernel, cap=cap),
        grid=(b,),
        in_specs=[pl.BlockSpec((1, s, ne), lambda bi: (bi, 0, 0))],
        out_specs=pl.BlockSpec((1, ne, cap), lambda bi: (bi, 0, 0)),
        out_shape=jax.ShapeDtypeStruct((b, ne, cap), jnp.int32),
        compiler_params=_params("arbitrary"),
        name="expert_compact",
    )(pos_cols)


def _ffn_kernel(idx_ref, h2_hbm, wg_ref, wu_ref, wd_ref, ye_ref, x_scr, hid_scr, sem, *, nb, cap):
    e = pl.program_id(0)
    f = pl.program_id(1)
    ne = pl.num_programs(0)
    nf = pl.num_programs(1)
    rows = nb * cap
    per_step = rows // FFN_UP_STEPS
    tf = wg_ref.shape[-1]
    slot = lax.rem(e, 2)
    nslot = 1 - slot
    nxt = jnp.minimum(e + 1, ne - 1)

    def row_copy(expert, r, b, sl):
        tok = idx_ref[expert, r]
        return pltpu.make_async_copy(h2_hbm.at[b, pl.ds(tok, 1), :], x_scr.at[sl, pl.ds(r, 1), :], sem.at[sl])

    def wait_rows(sl):
        pltpu.make_async_copy(x_scr.at[sl], x_scr.at[sl], sem.at[sl]).wait()

    @pl.when(jnp.logical_and(e == 0, f == 0))
    def _():
        def first(r, carry):
            row_copy(0, r, r // cap, 0).start()
            return carry
        lax.fori_loop(0, rows, first, 0)

    for ff in range(FFN_UP_STEPS):
        for cur in range(2):
            @pl.when(jnp.logical_and(f == ff, slot == cur))
            def _(ff=ff, cur=cur):
                if ff == 0:
                    wait_rows(cur)
                for r in range(ff * per_step, (ff + 1) * per_step):
                    row_copy(nxt, r, r // cap, 1 - cur).start(priority=r % 2)
                x = x_scr[cur].astype(BF16)
                g = jnp.dot(x, wg_ref[0, 0].astype(BF16), preferred_element_type=F32)
                u = jnp.dot(x, wu_ref[0, 0].astype(BF16), preferred_element_type=F32)
                hid_scr[:, ff * tf:(ff + 1) * tf] = ((g * jax.nn.sigmoid(g)) * u).astype(BF16)

    @pl.when(f >= FFN_UP_STEPS)
    def _():
        y = jnp.dot(hid_scr[...], wd_ref[0, 0].astype(BF16), preferred_element_type=F32)
        ye_ref[...] = y.reshape(nb, 1, cap, y.shape[-1]).astype(BF16)

    @pl.when(jnp.logical_and(e == ne - 1, f == nf - 1))
    def _():
        wait_rows(nslot)


def _ffn(idx, h2, w_gate, w_up, w_down, layer):
    b, s, d = h2.shape
    ne = idx.shape[0]
    cap = idx.shape[1] // b
    ff = w_gate.shape[-1]
    tf = ff // FFN_UP_STEPS
    dn = d // FFN_DOWN_STEPS
    up = lambda f: jnp.minimum(f, FFN_UP_STEPS - 1)
    down = lambda f: jnp.maximum(f - FFN_UP_STEPS, 0)
    grid_spec = pltpu.PrefetchScalarGridSpec(
        num_scalar_prefetch=1,
        grid=(ne, FFN_UP_STEPS + FFN_DOWN_STEPS),
        in_specs=[
            pl.BlockSpec(memory_space=pl.ANY),
            pl.BlockSpec((1, 1, d, tf), lambda e, f, idx_ref: (layer, e, 0, up(f))),
            pl.BlockSpec((1, 1, d, tf), lambda e, f, idx_ref: (layer, e, 0, up(f))),
            pl.BlockSpec((1, 1, ff, dn), lambda e, f, idx_ref: (layer, e, 0, down(f))),
        ],
        out_specs=pl.BlockSpec((b, 1, cap, dn), lambda e, f, idx_ref: (0, e, 0, down(f))),
        scratch_shapes=[
            pltpu.VMEM((2, b * cap, d), F32),
            pltpu.VMEM((b * cap, ff), BF16),
            pltpu.SemaphoreType.DMA((2,)),
        ],
    )
    return pl.pallas_call(
        functools.partial(_ffn_kernel, nb=b, cap=cap),
        grid_spec=grid_spec,
        out_shape=jax.ShapeDtypeStruct((b, ne, cap, d), BF16),
        compiler_params=_params("arbitrary", "arbitrary"),
        name="expert_ffn",
    )(idx, h2, w_gate, w_up, w_down)


def _combine_kernel(pos_ref, gate_ref, ye_ref, x1_ref, g2_ref, o_ref, scat_scr, *, cap):
    j = pl.program_id(1)
    seq, ne = pos_ref.shape[1], pos_ref.shape[2]
    rt = COMBINE_BUILD_ROWS

    @pl.when(j == 0)
    def _():
        spread = _lane_spread(ne)
        lane = lax.broadcasted_iota(jnp.int32, (rt, LANES), 1).astype(F32)

        def build(i, carry):
            r0 = pl.multiple_of(i * rt, rt)
            pos = jnp.dot(pos_ref[0, pl.ds(r0, rt), :].astype(BF16), spread, preferred_element_type=F32)
            gate = jnp.dot(gate_ref[0, pl.ds(r0, rt), :].astype(BF16), spread, preferred_element_type=F32)
            for e in range(ne):
                p = pos[:, e * LANES:(e + 1) * LANES]
                g = gate[:, e * LANES:(e + 1) * LANES]
                for h in range(cap // LANES):
                    cols = slice(e * cap + h * LANES, e * cap + (h + 1) * LANES)
                    scat_scr[pl.ds(r0, rt), cols] = jnp.where(lane + float(h * LANES) == p, g, 0.0).astype(BF16)
            return carry

        lax.fori_loop(0, seq // rt, build, 0)

    acc = jnp.dot(scat_scr[...], ye_ref[0], preferred_element_type=F32)
    o_ref[0] = x1_ref[0] + g2_ref[0] * acc


def _combine(pos_cols, gate_cols, ye, x1, mod6):
    b, s, d = x1.shape
    ne, cap = ye.shape[1], ye.shape[2]
    tn = COMBINE_TN
    return pl.pallas_call(
        functools.partial(_combine_kernel, cap=cap),
        grid=(b, d // tn),
        in_specs=[
            pl.BlockSpec((1, s, ne), lambda bi, j: (bi, 0, 0)),
            pl.BlockSpec((1, s, ne), lambda bi, j: (bi, 0, 0)),
            pl.BlockSpec((1, ne * cap, tn), lambda bi, j: (bi, 0, j)),
            pl.BlockSpec((1, s, tn), lambda bi, j: (bi, 0, j)),
            pl.BlockSpec((1, 1, tn), lambda bi, j: (bi * 6 + 5, 0, j)),
        ],
        out_specs=pl.BlockSpec((1, s, tn), lambda bi, j: (bi, 0, j)),
        out_shape=jax.ShapeDtypeStruct((b, s, d), F32),
        scratch_shapes=[pltpu.VMEM((s, ne * cap), BF16)],
        compiler_params=_params("arbitrary", "arbitrary"),
        name="expert_combine",
    )(pos_cols, gate_cols, ye.reshape(b, ne * cap, d), x1, mod6)


def _rope_tables(seq):
    inv = 1.0 / (ROPE_THETA ** (jnp.arange(0, ATT_HEAD_DIM, 2, dtype=F32) / ATT_HEAD_DIM))
    ang = jnp.arange(seq, dtype=F32)[:, None] * inv[None, :]
    ang = jnp.concatenate([ang, ang], axis=-1)
    sign = jnp.where(jnp.arange(ATT_HEAD_DIM) < ATT_HEAD_DIM // 2, -1.0, 1.0).astype(F32)
    return jnp.cos(ang), jnp.sin(ang) * sign


def kernel(x, c, w_ada, b_ada, w_in, b_gates, q_gain, k_gain, sink, m_gain, w_out,
           w_router, w_gate, w_up, w_down):
    b, s, d = x.shape
    depth = w_ada.shape[0]
    cap = CAPACITY_FACTOR * s // N_EXPERTS
    cos, sin_signed = _rope_tables(s)
    c_pad = jnp.pad(c, ((0, SUBLANES - b), (0, 0)))
    mod = _ada(c_pad, w_ada, b_ada)
    w_in_bf = w_in.astype(BF16)
    for l in range(depth):
        mod6 = mod[l, :b].reshape(b * 6, 1, d)
        proj, gates = _inproj(x, mod6, w_in_bf, w_in[l, :, PROJ_WIDTH:].astype(BF16), l, cos, sin_signed,
                              q_gain[l].reshape(1, -1), k_gain[l].reshape(1, -1))
        att = _attention(proj, sink[l])
        m_out = _mlstm(proj, gates, b_gates[l].reshape(-1), m_gain[l])
        x1, h2, aff = _outproj(att, m_out, w_out[l].astype(BF16), x, mod6, w_router[l])
        pos_r, gate_r = _select(jnp.transpose(aff, (0, 2, 1)).reshape(b * N_EXPERTS, s), cap)
        to_cols = lambda t: jnp.transpose(t.reshape(b, N_EXPERTS, s), (0, 2, 1))
        pos_c = to_cols(pos_r)
        idx = jnp.transpose(_compact(pos_c, cap), (1, 0, 2)).reshape(N_EXPERTS, b * cap)
        ye = _ffn(idx, h2, w_gate, w_up, w_down, l)
        x = _combine(pos_c, to_cols(gate_r), ye, x1, mod6)
    return x
```

```python
import functools

import jax
import jax.numpy as jnp
from jax import lax
from jax.experimental import pallas as pl
from jax.experimental.pallas import tpu as pltpu

F32 = jnp.float32
BF16 = jnp.bfloat16
SUBLANES = 8
LANES = 128

D_MODEL = 2048
ATT_HEAD_DIM = 128
ATT_HEADS = 8
ATT_KV_HEADS = 2
ATT_GROUP = ATT_HEADS // ATT_KV_HEADS
ATT_WIDTH = ATT_HEADS * ATT_HEAD_DIM
ATT_KV_WIDTH = ATT_KV_HEADS * ATT_HEAD_DIM
WINDOW = 128
ATT_BLOCK = 128
ATT_STEP_BLOCKS = 4
ROPE_THETA = 10000.0
M_HEADS = 4
M_V_DIM = 256
M_QK_DIM = 128
M_WIDTH = M_HEADS * M_V_DIM
M_QK_WIDTH = M_HEADS * M_QK_DIM
MLSTM_BLOCK = 256
N_GATES = 4 * M_HEADS
N_EXPERTS = 16
EXPERT_FF = D_MODEL // 2
CAPACITY_FACTOR = 2
EPS = 1e-6
NEG = -1e30

COL_AQ = 0
COL_MV = ATT_WIDTH
COL_MO = COL_MV + M_WIDTH
COL_AK = COL_MO + M_WIDTH
COL_AV = COL_AK + ATT_KV_WIDTH
COL_MQ = COL_AV + ATT_KV_WIDTH
COL_MK = COL_MQ + M_QK_WIDTH
PROJ_WIDTH = COL_MK + M_QK_WIDTH
IN_COL_BLOCKS = (0, 1, 5, 6, 7, 8, 2, 3, 4)

VMEM_LIMIT_BYTES = 56 * 1024 * 1024

ADA_TN = 1024
ADA_K_SPLITS = 4
INPROJ_TM = 1024
INPROJ_TN = 512
OUTPROJ_TM = 1024
OUTPROJ_TN = 512
FFN_UP_STEPS = 4
FFN_DOWN_STEPS = 2
COMBINE_TN = 512
COMBINE_BUILD_ROWS = 256
SELECT_BISECTIONS = 64
SELECT_MIN_UPPER = 1e-30


def _params(*sem):
    return pltpu.CompilerParams(dimension_semantics=sem, vmem_limit_bytes=VMEM_LIMIT_BYTES)


def _ada_kernel(c_ref, *refs):
    w_refs, b_ref, o_ref = refs[:ADA_K_SPLITS], refs[ADA_K_SPLITS], refs[ADA_K_SPLITS + 1]
    c = c_ref[...]
    c_act = c * jax.nn.sigmoid(c)
    c_hi = c_act.astype(BF16)
    c_lo = (c_act - c_hi.astype(F32)).astype(BF16)
    kb = c.shape[1] // ADA_K_SPLITS
    acc = b_ref[0]
    for s, w_ref in enumerate(w_refs):
        w = w_ref[0]
        w_hi = w.astype(BF16)
        w_lo = (w - w_hi.astype(F32)).astype(BF16)
        ks = slice(s * kb, (s + 1) * kb)
        acc = acc + jnp.dot(c_hi[:, ks], w_hi, preferred_element_type=F32)
        acc = acc + jnp.dot(c_lo[:, ks], w_hi, preferred_element_type=F32)
        acc = acc + jnp.dot(c_hi[:, ks], w_lo, preferred_element_type=F32)
    o_ref[0] = acc


def _ada(c_pad, w_ada, b_ada):
    depth, d, n = w_ada.shape
    rows = c_pad.shape[0]
    return pl.pallas_call(
        _ada_kernel,
        grid=(depth, n // ADA_TN),
        in_specs=(
            [pl.BlockSpec((rows, d), lambda l, j: (0, 0))]
            + [pl.BlockSpec((1, d // ADA_K_SPLITS, ADA_TN), functools.partial(lambda l, j, s: (l, s, j), s=s))
               for s in range(ADA_K_SPLITS)]
            + [pl.BlockSpec((1, 1, ADA_TN), lambda l, j: (l, 0, j))]),
        out_specs=pl.BlockSpec((1, rows, ADA_TN), lambda l, j: (l, 0, j)),
        out_shape=jax.ShapeDtypeStruct((depth, rows, n), F32),
        compiler_params=_params("arbitrary", "arbitrary"),
        name="ada_mod",
    )(c_pad, *([w_ada] * ADA_K_SPLITS), b_ada.reshape(depth, 1, n))


def _inproj_kernel(perm_ref, x_ref, sc_ref, sh_ref, w_ref, wg_ref, cos_ref, sin_ref, qg_ref, kg_ref,
                   o_ref, g_ref, h_scr):
    del perm_ref
    j = pl.program_id(2)

    @pl.when(j == 0)
    def _():
        x = x_ref[0]
        r = lax.rsqrt(jnp.mean(x * x, axis=-1, keepdims=True) + EPS)
        h = (x * r) * (1.0 + sc_ref[0]) + sh_ref[0]
        hb = h.astype(BF16)
        h_scr[...] = hb
        g_ref[0] = jnp.dot(hb, wg_ref[...], preferred_element_type=F32)

    y = jnp.dot(h_scr[...], w_ref[0], preferred_element_type=F32)

    head_ones = jnp.ones((ATT_HEAD_DIM, ATT_HEAD_DIM), BF16)

    def norm_rope(t, gain):
        ss = jnp.dot((t * t).astype(BF16), head_ones, preferred_element_type=F32)
        tn = t * lax.rsqrt(ss * (1.0 / ATT_HEAD_DIM) + EPS) * gain
        return tn * cos_ref[...] + pltpu.roll(tn, ATT_HEAD_DIM // 2, 1) * sin_ref[...]

    heads_per_tile = INPROJ_TN // ATT_HEAD_DIM
    q_tiles = ATT_WIDTH // INPROJ_TN
    kv_tile = COL_AK // INPROJ_TN
    scale = ATT_HEAD_DIM ** -0.5

    @pl.when(j < q_tiles)
    def _():
        for u in range(heads_per_tile):
            sl = slice(u * ATT_HEAD_DIM, (u + 1) * ATT_HEAD_DIM)
            o_ref[0, :, sl] = (norm_rope(y[:, sl], qg_ref[...]) * scale).astype(BF16)

    @pl.when(j == kv_tile)
    def _():
        for u in range(ATT_KV_HEADS):
            sl = slice(u * ATT_HEAD_DIM, (u + 1) * ATT_HEAD_DIM)
            o_ref[0, :, sl] = norm_rope(y[:, sl], kg_ref[...]).astype(BF16)
        o_ref[0, :, ATT_KV_WIDTH:] = y[:, ATT_KV_WIDTH:].astype(BF16)

    @pl.when(jnp.logical_and(j >= q_tiles, j != kv_tile))
    def _():
        o_ref[0] = y.astype(BF16)


def _inproj(x, mod6, w_bf, w_gate, layer, cos, sin_signed, q_gain, k_gain):
    b, s, d = x.shape
    tm, tn = INPROJ_TM, INPROJ_TN
    assert COL_AK % tn == 0 and ATT_WIDTH % tn == 0 and 2 * ATT_KV_WIDTH == tn
    grid_spec = pltpu.PrefetchScalarGridSpec(
        num_scalar_prefetch=1,
        grid=(b, s // tm, PROJ_WIDTH // tn),
        in_specs=[
            pl.BlockSpec((1, tm, d), lambda bi, i, j, perm: (bi, i, 0)),
            pl.BlockSpec((1, 1, d), lambda bi, i, j, perm: (bi * 6 + 1, 0, 0)),
            pl.BlockSpec((1, 1, d), lambda bi, i, j, perm: (bi * 6 + 0, 0, 0)),
            pl.BlockSpec((1, d, tn), lambda bi, i, j, perm: (layer, 0, perm[j])),
            pl.BlockSpec((d, N_GATES), lambda bi, i, j, perm: (0, 0)),
            pl.BlockSpec((tm, ATT_HEAD_DIM), lambda bi, i, j, perm: (i, 0)),
            pl.BlockSpec((tm, ATT_HEAD_DIM), lambda bi, i, j, perm: (i, 0)),
            pl.BlockSpec((1, ATT_HEAD_DIM), lambda bi, i, j, perm: (0, 0)),
            pl.BlockSpec((1, ATT_HEAD_DIM), lambda bi, i, j, perm: (0, 0)),
        ],
        out_specs=[
            pl.BlockSpec((1, tm, tn), lambda bi, i, j, perm: (bi, i, j)),
            pl.BlockSpec((1, tm, N_GATES), lambda bi, i, j, perm: (bi, i, 0)),
        ],
        scratch_shapes=[pltpu.VMEM((tm, d), BF16)],
    )
    return pl.pallas_call(
        _inproj_kernel,
        grid_spec=grid_spec,
        out_shape=[
            jax.ShapeDtypeStruct((b, s, PROJ_WIDTH), BF16),
            jax.ShapeDtypeStruct((b, s, N_GATES), F32),
        ],
        compiler_params=_params("arbitrary", "arbitrary", "arbitrary"),
        name="in_proj",
    )(jnp.asarray(IN_COL_BLOCKS, jnp.int32), x, mod6, mod6, w_bf, w_gate, cos, sin_signed, q_gain, k_gain)


def _attn_kernel(sink_ref, q_ref, kl_ref, kc_ref, kr_ref, vl_ref, vc_ref, vr_ref, o_ref, *, n_steps):
    n = pl.program_id(1)
    L = ATT_BLOCK
    rows = ATT_GROUP * L
    i = lax.broadcasted_iota(jnp.int32, (rows, 3 * L), 0) & (L - 1)
    jj = lax.broadcasted_iota(jnp.int32, (rows, 3 * L), 1)
    in_window = jnp.abs(jj - L - i) <= WINDOW
    first_mask = in_window & ((jj >= L) | (n > 0))
    last_mask = in_window & ((jj < 2 * L) | (n < n_steps - 1))
    rgrp = lax.broadcasted_iota(jnp.int32, (rows, 1), 0) // L
    for kv in range(ATT_KV_HEADS):
        hs = slice(kv * ATT_HEAD_DIM, (kv + 1) * ATT_HEAD_DIM)
        kband = jnp.concatenate([kl_ref[0, :, hs], kc_ref[0, :, hs], kr_ref[0, :, hs]], axis=0)
        vband = jnp.concatenate([vl_ref[0, :, hs], vc_ref[0, :, hs], vr_ref[0, :, hs]], axis=0)
        sink = jnp.zeros((rows, 1), F32)
        for g in range(ATT_GROUP):
            sink = jnp.where(rgrp == g, sink_ref[kv * ATT_GROUP + g], sink)
        for blk in range(ATT_STEP_BLOCKS):
            qrows = slice(blk * L, (blk + 1) * L)
            q = jnp.concatenate(
                [q_ref[0, qrows, (kv * ATT_GROUP + g) * ATT_HEAD_DIM:(kv * ATT_GROUP + g + 1) * ATT_HEAD_DIM]
                 for g in range(ATT_GROUP)], axis=0)
            kb = kband[blk * L:(blk + 3) * L]
            vb = vband[blk * L:(blk + 3) * L]
            valid = first_mask if blk == 0 else (last_mask if blk == ATT_STEP_BLOCKS - 1 else in_window)
            s = lax.dot_general(q, kb, (((1,), (1,)), ((), ())), preferred_element_type=F32)
            s = jnp.where(valid, s, NEG)
            m = jnp.maximum(jnp.max(s, axis=-1, keepdims=True), sink)
            p = jnp.exp(s - m)
            denom = jnp.sum(p, axis=-1, keepdims=True) + jnp.exp(sink - m)
            o = jnp.dot(p.astype(BF16), vb, preferred_element_type=F32) / denom
            for g in range(ATT_GROUP):
                h = kv * ATT_GROUP + g
                o_ref[0, qrows, h * ATT_HEAD_DIM:(h + 1) * ATT_HEAD_DIM] = o[g * L:(g + 1) * L].astype(BF16)


def _attention(proj, sink):
    b, s, _ = proj.shape
    L = ATT_BLOCK
    nb = s // L
    sb = ATT_STEP_BLOCKS
    assert sb >= 2 and nb % sb == 0
    kblk = COL_AK // ATT_KV_WIDTH
    vblk = COL_AV // ATT_KV_WIDTH
    left = lambda col: pl.BlockSpec((1, L, ATT_KV_WIDTH), lambda bi, n: (bi, jnp.maximum(n * sb - 1, 0), col))
    right = lambda col: pl.BlockSpec((1, L, ATT_KV_WIDTH), lambda bi, n: (bi, jnp.minimum((n + 1) * sb, nb - 1), col))
    centre = lambda col: pl.BlockSpec((1, sb * L, ATT_KV_WIDTH), lambda bi, n: (bi, n, col))
    return pl.pallas_call(
        functools.partial(_attn_kernel, n_steps=nb // sb),
        grid=(b, nb // sb),
        in_specs=[
            pl.BlockSpec(memory_space=pltpu.SMEM),
            pl.BlockSpec((1, sb * L, ATT_WIDTH), lambda bi, n: (bi, n, COL_AQ // ATT_WIDTH)),
            left(kblk), centre(kblk), right(kblk),
            left(vblk), centre(vblk), right(vblk),
        ],
        out_specs=pl.BlockSpec((1, sb * L, ATT_WIDTH), lambda bi, n: (bi, n, 0)),
        out_shape=jax.ShapeDtypeStruct((b, s, ATT_WIDTH), BF16),
        compiler_params=_params("arbitrary", "arbitrary"),
        name="window_attn",
    )(sink, proj, proj, proj, proj, proj, proj, proj)


def _split3(x):
    hi = x.astype(BF16)
    rest = x - hi.astype(F32)
    mid = rest.astype(BF16)
    return hi, mid, (rest - mid.astype(F32)).astype(BF16)


def _mlstm_kernel(q_ref, k_ref, v_ref, mo_ref, gcol_ref, grow_ref, bcol_ref, brow_ref, gain_ref, o_ref,
                  bc_scr, kt_scr, ar_scr, wr_scr, bt_scr, c_scr, h_scr, *, seq):
    L = MLSTM_BLOCK
    nc = seq // L
    ns = 2 * M_HEADS
    scale = M_QK_DIM ** -0.5
    row = lax.broadcasted_iota(jnp.int32, (L, L), 0)
    col = lax.broadcasted_iota(jnp.int32, (L, L), 1)
    lower = col <= row
    upper = col >= row
    tril = jnp.where(lower, 1.0, 0.0).astype(BF16)
    triu = jnp.where(upper, 1.0, 0.0).astype(BF16)

    fwd_lane = lax.broadcasted_iota(jnp.int32, (1, N_GATES), 1) < ns + M_HEADS
    for c in range(nc):
        rows = slice(c * L, (c + 1) * L)
        lf = jax.nn.log_sigmoid(gcol_ref[0, rows, :] + bcol_ref[...])
        lf3 = _split3(lf)
        pre = sum(jnp.dot(tril, part, preferred_element_type=F32) for part in lf3)
        suf = sum(jnp.dot(triu, part, preferred_element_type=F32) for part in lf3)
        bcol = jnp.where(fwd_lane, pre, suf)
        for k in range(ns):
            bc_scr[k, rows, :] = jnp.broadcast_to(bcol[:, ns + k:ns + k + 1], (L, LANES))
        kt_scr[:, rows] = k_ref[0, rows, :].astype(F32).T.astype(BF16)

    gr = grow_ref[0] + brow_ref[...]
    ig_r = gr[:ns].reshape(ns * nc, L)
    lf_r = jax.nn.log_sigmoid(gr[ns:]).reshape(ns * nc, L)
    lf_r3 = _split3(lf_r)
    pre_r = sum(jnp.dot(part, triu, preferred_element_type=F32) for part in lf_r3)
    suf_r = sum(jnp.dot(part, tril, preferred_element_type=F32) for part in lf_r3)
    fwd_rows = lax.broadcasted_iota(jnp.int32, (ns * nc, 1), 0) < M_HEADS * nc
    a_r = ig_r - jnp.where(fwd_rows, pre_r, suf_r)
    btot = jnp.sum(lf_r, axis=-1, keepdims=True)
    ar_scr[...] = a_r.reshape(ns, nc, L)
    wr_scr[...] = (btot + a_r).reshape(ns, nc, L)
    bt_scr[...] = jnp.broadcast_to(btot, (ns * nc, LANES)).reshape(ns, nc, LANES)

    c_scr[...] = jnp.zeros_like(c_scr)
    h_scr[...] = jnp.zeros_like(h_scr)
    ones_cols = jnp.ones((L, LANES), BF16)

    twice = lambda t: jnp.concatenate([t, t], axis=1)

    def chunk_step(k, c, m_st):
        d, h = divmod(k, M_HEADS)
        r0 = pl.multiple_of(c * L, L)
        q = q_ref[0, pl.ds(r0, L), h * M_QK_DIM:(h + 1) * M_QK_DIM]
        kk = k_ref[0, pl.ds(r0, L), h * M_QK_DIM:(h + 1) * M_QK_DIM]
        v_ext = jnp.concatenate([v_ref[0, pl.ds(r0, L), h * M_V_DIM:(h + 1) * M_V_DIM], ones_cols], axis=1)
        bc = bc_scr[k, pl.ds(r0, L), :]
        ar = ar_scr[k, pl.ds(c, 1), :]
        wr = wr_scr[k, pl.ds(c, 1), :]
        bt = bt_scr[k, pl.ds(c, 1), :]
        dm = jnp.where(lower if d == 0 else upper, twice(bc) + ar, NEG)
        g_inter = bc + m_st
        m_t = jnp.maximum(jnp.broadcast_to(jnp.max(dm, axis=-1, keepdims=True), (L, LANES)), g_inter)
        e_inter = jnp.exp(g_inter - m_t) * scale
        s_qk = lax.dot_general(q, kk, (((1,), (1,)), ((), ())), preferred_element_type=F32)
        p = s_qk * scale * jnp.exp(dm - twice(m_t))
        qc = jnp.dot(q, c_scr[k].astype(BF16), preferred_element_type=F32)
        pv = jnp.dot(p.astype(BF16), v_ext, preferred_element_type=F32)
        num = pv[:, :M_V_DIM] + twice(e_inter) * qc[:, :M_V_DIM]
        den = pv[:, M_V_DIM:] + e_inter * qc[:, M_V_DIM:]
        inv = 1.0 / jnp.maximum(jnp.abs(den), jnp.exp(-m_t))
        hsl = (pl.ds(r0, L), slice(h * M_V_DIM, (h + 1) * M_V_DIM))
        h_scr[hsl] = h_scr[hsl] + num * twice(inv)
        m_new = jnp.maximum(bt + m_st, jnp.broadcast_to(jnp.max(wr, axis=-1, keepdims=True), (1, LANES)))
        a = jnp.exp(bt + m_st - m_new)
        kt = kt_scr[h * M_QK_DIM:(h + 1) * M_QK_DIM, pl.ds(r0, L)]
        ek_t = (kt.astype(F32) * jnp.exp(wr - twice(m_new))).astype(BF16)
        c_scr[k] = jnp.concatenate([a, a, a], axis=1) * c_scr[k] + jnp.dot(ek_t, v_ext, preferred_element_type=F32)
        return m_new

    def body(it, ms):
        out = []
        for k in range(ns):
            c = it if k < M_HEADS else nc - 1 - it
            out.append(chunk_step(k, c, ms[k]))
        return tuple(out)

    lax.fori_loop(0, nc, body, tuple(jnp.zeros((1, LANES), F32) for _ in range(ns)))

    def fin(i, carry):
        r0 = pl.multiple_of(i * L, L)
        for h in range(M_HEADS):
            sl = slice(h * M_V_DIM, (h + 1) * M_V_DIM)
            x = h_scr[pl.ds(r0, L), sl]
            y = x * lax.rsqrt(jnp.mean(x * x, axis=-1, keepdims=True) + EPS) * gain_ref[:, sl]
            o_ref[0, pl.ds(r0, L), sl] = (y * jax.nn.sigmoid(mo_ref[0, pl.ds(r0, L), sl].astype(F32))).astype(BF16)
        return carry

    lax.fori_loop(0, nc, fin, 0)


def _mlstm(proj, gates, b_gates, m_gain):
    b, s, _ = proj.shape
    L = MLSTM_BLOCK
    nc = s // L
    ns = 2 * M_HEADS
    grow = jnp.transpose(gates, (0, 2, 1)).reshape(b, N_GATES, nc, L)
    col = lambda width, off: pl.BlockSpec((1, s, width), lambda bi: (bi, 0, off // width))
    return pl.pallas_call(
        functools.partial(_mlstm_kernel, seq=s),
        grid=(b,),
        in_specs=[
            col(M_QK_WIDTH, COL_MQ), col(M_QK_WIDTH, COL_MK), col(M_WIDTH, COL_MV),
            pl.BlockSpec((1, s, M_WIDTH), lambda bi: (bi, 0, COL_MO // M_WIDTH), pipeline_mode=pl.Buffered(1)),
            pl.BlockSpec((1, s, N_GATES), lambda bi: (bi, 0, 0)),
            pl.BlockSpec((1, N_GATES, nc, L), lambda bi: (bi, 0, 0, 0)),
            pl.BlockSpec((1, N_GATES), lambda bi: (0, 0)),
            pl.BlockSpec((N_GATES, 1, 1), lambda bi: (0, 0, 0)),
            pl.BlockSpec((1, M_WIDTH), lambda bi: (0, 0)),
        ],
        out_specs=pl.BlockSpec((1, s, M_WIDTH), lambda bi: (bi, 0, 0)),
        out_shape=jax.ShapeDtypeStruct((b, s, M_WIDTH), BF16),
        scratch_shapes=[
            pltpu.VMEM((ns, s, LANES), F32),
            pltpu.VMEM((M_QK_WIDTH, s), BF16),
            pltpu.VMEM((ns, nc, L), F32), pltpu.VMEM((ns, nc, L), F32), pltpu.VMEM((ns, nc, LANES), F32),
            pltpu.VMEM((ns, M_QK_DIM, M_V_DIM + LANES), F32),
            pltpu.VMEM((s, M_WIDTH), F32),
        ],
        compiler_params=_params("arbitrary"),
        name="mlstm",
    )(proj, proj, proj, proj, gates, grow, b_gates.reshape(1, N_GATES), b_gates.reshape(N_GATES, 1, 1),
      m_gain.reshape(1, M_WIDTH))


def _outproj_kernel(att_ref, mo_ref, wa_ref, wm_ref, x_ref, g1_ref, sc_ref, sh_ref, wr_ref,
                    x1_ref, h2_ref, aff_ref):
    j = pl.program_id(2)
    nj = pl.num_programs(2)
    tn = OUTPROJ_TN
    mix = (jnp.dot(att_ref[0], wa_ref[...], preferred_element_type=F32)
           + jnp.dot(mo_ref[0], wm_ref[...], preferred_element_type=F32))
    x1t = x_ref[0] + g1_ref[0] * mix
    for jj in range(D_MODEL // tn):
        @pl.when(j == jj)
        def _(jj=jj):
            x1_ref[0, :, jj * tn:(jj + 1) * tn] = x1t

    @pl.when(j == nj - 1)
    def _():
        xf = x1_ref[0]
        r = lax.rsqrt(jnp.mean(xf * xf, axis=-1, keepdims=True) + EPS)
        h2 = (xf * r) * (1.0 + sc_ref[0]) + sh_ref[0]
        h2_hi = h2.astype(BF16)
        h2_ref[0] = h2
        h2_lo = (h2 - h2_hi.astype(F32)).astype(BF16)
        hi_terms = jnp.dot(h2_hi, wr_ref[...], preferred_element_type=F32)
        lo_term = jnp.dot(h2_lo, wr_ref[:, :N_EXPERTS], preferred_element_type=F32)
        logits = hi_terms[:, :N_EXPERTS] + hi_terms[:, N_EXPERTS:] + lo_term
        e = jnp.exp(logits - jnp.max(logits, axis=-1, keepdims=True))
        aff_ref[0] = e / jnp.sum(e, axis=-1, keepdims=True)


def _outproj(att, m_out, w_out, x, mod6, w_router):
    b, s, d = x.shape
    tm, tn = OUTPROJ_TM, OUTPROJ_TN
    half = ATT_WIDTH
    assert w_out.shape[0] == 2 * half and M_WIDTH == half
    wr_hi = w_router.astype(BF16)
    wr_lo = (w_router - wr_hi.astype(F32)).astype(BF16)
    return pl.pallas_call(
        _outproj_kernel,
        grid=(b, s // tm, d // tn),
        in_specs=[
            pl.BlockSpec((1, tm, ATT_WIDTH), lambda bi, i, j: (bi, i, 0)),
            pl.BlockSpec((1, tm, M_WIDTH), lambda bi, i, j: (bi, i, 0)),
            pl.BlockSpec((half, tn), lambda bi, i, j: (0, j)),
            pl.BlockSpec((half, tn), lambda bi, i, j: (1, j)),
            pl.BlockSpec((1, tm, tn), lambda bi, i, j: (bi, i, j)),
            pl.BlockSpec((1, 1, tn), lambda bi, i, j: (bi * 6 + 2, 0, j)),
            pl.BlockSpec((1, 1, d), lambda bi, i, j: (bi * 6 + 4, 0, 0)),
            pl.BlockSpec((1, 1, d), lambda bi, i, j: (bi * 6 + 3, 0, 0)),
            pl.BlockSpec((d, 2 * N_EXPERTS), lambda bi, i, j: (0, 0)),
        ],
        out_specs=[
            pl.BlockSpec((1, tm, d), lambda bi, i, j: (bi, i, 0)),
            pl.BlockSpec((1, tm, d), lambda bi, i, j: (bi, i, 0)),
            pl.BlockSpec((1, tm, N_EXPERTS), lambda bi, i, j: (bi, i, 0)),
        ],
        out_shape=[
            jax.ShapeDtypeStruct((b, s, d), F32),
            jax.ShapeDtypeStruct((b, s, d), F32),
            jax.ShapeDtypeStruct((b, s, N_EXPERTS), F32),
        ],
        compiler_params=_params("arbitrary", "arbitrary", "arbitrary"),
        name="out_proj",
    )(att, m_out, w_out, w_out, x, mod6, mod6, mod6, jnp.concatenate([wr_hi, wr_lo], axis=1))


def _prefix_sum_lanes(x, n):
    lane = lax.broadcasted_iota(jnp.int32, x.shape, 1)
    sh = 1
    while sh < n:
        x = x + jnp.where(lane >= sh, pltpu.roll(x, sh, 1), 0.0)
        sh *= 2
    return x


def _select_kernel(aff_ref, pos_ref, gate_ref, *, cap, seq):
    a = aff_ref[...]
    ne = a.shape[0]

    def body(_, lohi):
        lo, hi = lohi
        mid = jnp.where(lo > 0.0, 0.5 * (lo + hi), hi * (1.0 / 256.0))
        cnt = jnp.sum(jnp.where(a >= mid, 1.0, 0.0), axis=-1, keepdims=True)
        ok = cnt >= cap
        return jnp.where(ok, mid, lo), jnp.where(ok, hi, mid)

    lo0 = jnp.zeros((ne, 1), F32)
    hi0 = jnp.maximum(2.0 * jnp.max(a, axis=-1, keepdims=True), SELECT_MIN_UPPER)
    lo, hi = lax.fori_loop(0, SELECT_BISECTIONS, body, (lo0, hi0))
    above = a >= hi
    band = jnp.logical_and(a >= lo, jnp.logical_not(above))
    n_above = jnp.sum(jnp.where(above, 1.0, 0.0), axis=-1, keepdims=True)
    band_rank = _prefix_sum_lanes(jnp.where(band, 1.0, 0.0), seq)
    sel = jnp.where(above, 1.0, jnp.where(band, jnp.where(band_rank <= cap - n_above, 1.0, 0.0), 0.0))
    pos = _prefix_sum_lanes(sel, seq) - 1.0
    chosen = sel > 0.5
    pos_ref[...] = jnp.where(chosen, pos, -1.0)
    gate_ref[...] = jnp.where(chosen, a, 0.0)


def _select(aff_rows, cap):
    n, s = aff_rows.shape
    spec = pl.BlockSpec((n, s), lambda i: (0, 0))
    return pl.pallas_call(
        functools.partial(_select_kernel, cap=cap, seq=s),
        grid=(1,),
        in_specs=[spec],
        out_specs=[spec, spec],
        out_shape=[jax.ShapeDtypeStruct((n, s), F32), jax.ShapeDtypeStruct((n, s), F32)],
        compiler_params=_params("arbitrary"),
        name="expert_select",
    )(aff_rows)


def _lane_spread(ne):
    blk = lax.broadcasted_iota(jnp.int32, (ne, ne * LANES), 1) // LANES
    return jnp.where(blk == lax.broadcasted_iota(jnp.int32, (ne, ne * LANES), 0), 1.0, 0.0).astype(BF16)


def _compact_kernel(pos_ref, idx_ref, *, cap):
    seq, ne = pos_ref.shape[1], pos_ref.shape[2]
    rt = COMBINE_BUILD_ROWS
    halves = cap // LANES
    spread = _lane_spread(ne)
    lane = lax.broadcasted_iota(jnp.int32, (rt, LANES), 1).astype(F32)
    tok0 = lax.broadcasted_iota(jnp.int32, (rt, LANES), 0).astype(F32)

    def body(i, accs):
        r0 = pl.multiple_of(i * rt, rt)
        pos = jnp.dot(pos_ref[0, pl.ds(r0, rt), :].astype(BF16), spread, preferred_element_type=F32)
        tok = tok0 + r0.astype(F32)
        out = []
        for e in range(ne):
            p = pos[:, e * LANES:(e + 1) * LANES]
            for h in range(halves):
                hit = lane + float(h * LANES) == p
                out.append(accs[e * halves + h] + jnp.sum(jnp.where(hit, tok, 0.0), axis=0, keepdims=True))
        return tuple(out)

    accs = lax.fori_loop(0, seq // rt, body, tuple(jnp.zeros((1, LANES), F32) for _ in range(ne * halves)))
    for e in range(ne):
        for h in range(halves):
            idx_ref[0, e:e + 1, h * LANES:(h + 1) * LANES] = accs[e * halves + h].astype(jnp.int32)


def _compact(pos_cols, cap):
    b, s, ne = pos_cols.shape
    return pl.pallas_call(
        functools.partial(_compact_kernel, cap=cap),
        grid=(b,),
        in_specs=[pl.BlockSpec((1, s, ne), lambda bi: (bi, 0, 0))],
        out_specs=pl.BlockSpec((1, ne, cap), lambda bi: (bi, 0, 0)),
        out_shape=jax.ShapeDtypeStruct((b, ne, cap), jnp.int32),
        compiler_params=_params("arbitrary"),
        name="expert_compact",
    )(pos_cols)


def _ffn_kernel(idx_ref, h2_hbm, wg_ref, wu_ref, wd_ref, ye_ref, x_scr, hid_scr, sem, *, nb, cap):
    e = pl.program_id(0)
    f = pl.program_id(1)
    ne = pl.num_programs(0)
    nf = pl.num_programs(1)
    rows = nb * cap
    per_step = rows // FFN_UP_STEPS
    tf = wg_ref.shape[-1]
    slot = lax.rem(e, 2)
    nslot = 1 - slot
    nxt = jnp.minimum(e + 1, ne - 1)

    def row_copy(expert, r, b, sl):
        tok = idx_ref[expert, r]
        return pltpu.make_async_copy(h2_hbm.at[b, pl.ds(tok, 1), :], x_scr.at[sl, pl.ds(r, 1), :], sem.at[sl])

    def wait_rows(sl):
        pltpu.make_async_copy(x_scr.at[sl], x_scr.at[sl], sem.at[sl]).wait()

    @pl.when(jnp.logical_and(e == 0, f == 0))
    def _():
        def first(r, carry):
            row_copy(0, r, r // cap, 0).start()
            return carry
        lax.fori_loop(0, rows, first, 0)

    for ff in range(FFN_UP_STEPS):
        for cur in range(2):
            @pl.when(jnp.logical_and(f == ff, slot == cur))
            def _(ff=ff, cur=cur):
                if ff == 0:
                    wait_rows(cur)
                for r in range(ff * per_step, (ff + 1) * per_step):
                    row_copy(nxt, r, r // cap, 1 - cur).start(priority=r % 2)
                x = x_scr[cur].astype(BF16)
                g = jnp.dot(x, wg_ref[0, 0].astype(BF16), preferred_element_type=F32)
                u = jnp.dot(x, wu_ref[0, 0].astype(BF16), preferred_element_type=F32)
                hid_scr[:, ff * tf:(ff + 1) * tf] = ((g * jax.nn.sigmoid(g)) * u).astype(BF16)

    @pl.when(f >= FFN_UP_STEPS)
    def _():
        y = jnp.dot(hid_scr[...], wd_ref[0, 0].astype(BF16), preferred_element_type=F32)
        ye_ref[...] = y.reshape(nb, 1, cap, y.shape[-1]).astype(BF16)

    @pl.when(jnp.logical_and(e == ne - 1, f == nf - 1))
    def _():
        wait_rows(nslot)


def _ffn(idx, h2, w_gate, w_up, w_down, layer):
    b, s, d = h2.shape
    ne = idx.shape[0]
    cap = idx.shape[1] // b
    ff = w_gate.shape[-1]
    tf = ff // FFN_UP_STEPS
    dn = d // FFN_DOWN_STEPS
    up = lambda f: jnp.minimum(f, FFN_UP_STEPS - 1)
    down = lambda f: jnp.maximum(f - FFN_UP_STEPS, 0)
    grid_spec = pltpu.PrefetchScalarGridSpec(
        num_scalar_prefetch=1,
        grid=(ne, FFN_UP_STEPS + FFN_DOWN_STEPS),
        in_specs=[
            pl.BlockSpec(memory_space=pl.ANY),
            pl.BlockSpec((1, 1, d, tf), lambda e, f, idx_ref: (layer, e, 0, up(f))),
            pl.BlockSpec((1, 1, d, tf), lambda e, f, idx_ref: (layer, e, 0, up(f))),
            pl.BlockSpec((1, 1, ff, dn), lambda e, f, idx_ref: (layer, e, 0, down(f))),
        ],
        out_specs=pl.BlockSpec((b, 1, cap, dn), lambda e, f, idx_ref: (0, e, 0, down(f))),
        scratch_shapes=[
            pltpu.VMEM((2, b * cap, d), F32),
            pltpu.VMEM((b * cap, ff), BF16),
            pltpu.SemaphoreType.DMA((2,)),
        ],
    )
    return pl.pallas_call(
        functools.partial(_ffn_kernel, nb=b, cap=cap),
        grid_spec=grid_spec,
        out_shape=jax.ShapeDtypeStruct((b, ne, cap, d), BF16),
        compiler_params=_params("arbitrary", "arbitrary"),
        name="expert_ffn",
    )(idx, h2, w_gate, w_up, w_down)


def _combine_kernel(pos_ref, gate_ref, ye_ref, x1_ref, g2_ref, o_ref, scat_scr, *, cap):
    j = pl.program_id(1)
    seq, ne = pos_ref.shape[1], pos_ref.shape[2]
    rt = COMBINE_BUILD_ROWS

    @pl.when(j == 0)
    def _():
        spread = _lane_spread(ne)
        lane = lax.broadcasted_iota(jnp.int32, (rt, LANES), 1).astype(F32)

        def build(i, carry):
            r0 = pl.multiple_of(i * rt, rt)
            pos = jnp.dot(pos_ref[0, pl.ds(r0, rt), :].astype(BF16), spread, preferred_element_type=F32)
            gate = jnp.dot(gate_ref[0, pl.ds(r0, rt), :].astype(BF16), spread, preferred_element_type=F32)
            for e in range(ne):
                p = pos[:, e * LANES:(e + 1) * LANES]
                g = gate[:, e * LANES:(e + 1) * LANES]
                for h in range(cap // LANES):
                    cols = slice(e * cap + h * LANES, e * cap + (h + 1) * LANES)
                    scat_scr[pl.ds(r0, rt), cols] = jnp.where(lane + float(h * LANES) == p, g, 0.0).astype(BF16)
            return carry

        lax.fori_loop(0, seq // rt, build, 0)

    acc = jnp.dot(scat_scr[...], ye_ref[0], preferred_element_type=F32)
    o_ref[0] = x1_ref[0] + g2_ref[0] * acc


def _combine(pos_cols, gate_cols, ye, x1, mod6):
    b, s, d = x1.shape
    ne, cap = ye.shape[1], ye.shape[2]
    tn = COMBINE_TN
    return pl.pallas_call(
        functools.partial(_combine_kernel, cap=cap),
        grid=(b, d // tn),
        in_specs=[
            pl.BlockSpec((1, s, ne), lambda bi, j: (bi, 0, 0)),
            pl.BlockSpec((1, s, ne), lambda bi, j: (bi, 0, 0)),
            pl.BlockSpec((1, ne * cap, tn), lambda bi, j: (bi, 0, j)),
            pl.BlockSpec((1, s, tn), lambda bi, j: (bi, 0, j)),
            pl.BlockSpec((1, 1, tn), lambda bi, j: (bi * 6 + 5, 0, j)),
        ],
        out_specs=pl.BlockSpec((1, s, tn), lambda bi, j: (bi, 0, j)),
        out_shape=jax.ShapeDtypeStruct((b, s, d), F32),
        scratch_shapes=[pltpu.VMEM((s, ne * cap), BF16)],
        compiler_params=_params("arbitrary", "arbitrary"),
        name="expert_combine",
    )(pos_cols, gate_cols, ye.reshape(b, ne * cap, d), x1, mod6)


def _rope_tables(seq):
    inv = 1.0 / (ROPE_THETA ** (jnp.arange(0, ATT_HEAD_DIM, 2, dtype=F32) / ATT_HEAD_DIM))
    ang = jnp.arange(seq, dtype=F32)[:, None] * inv[None, :]
    ang = jnp.concatenate([ang, ang], axis=-1)
    sign = jnp.where(jnp.arange(ATT_HEAD_DIM) < ATT_HEAD_DIM // 2, -1.0, 1.0).astype(F32)
    return jnp.cos(ang), jnp.sin(ang) * sign


def kernel(x, c, w_ada, b_ada, w_in, b_gates, q_gain, k_gain, sink, m_gain, w_out,
           w_router, w_gate, w_up, w_down):
    b, s, d = x.shape
    depth = w_ada.shape[0]
    cap = CAPACITY_FACTOR * s // N_EXPERTS
    cos, sin_signed = _rope_tables(s)
    c_pad = jnp.pad(c, ((0, SUBLANES - b), (0, 0)))
    mod = _ada(c_pad, w_ada, b_ada)
    w_in_bf = w_in.astype(BF16)
    for l in range(depth):
        mod6 = mod[l, :b].reshape(b * 6, 1, d)
        proj, gates = _inproj(x, mod6, w_in_bf, w_in[l, :, PROJ_WIDTH:].astype(BF16), l, cos, sin_signed,
                              q_gain[l].reshape(1, -1), k_gain[l].reshape(1, -1))
        att = _attention(proj, sink[l])
        m_out = _mlstm(proj, gates, b_gates[l].reshape(-1), m_gain[l])
        x1, h2, aff = _outproj(att, m_out, w_out[l].astype(BF16), x, mod6, w_router[l])
        pos_r, gate_r = _select(jnp.transpose(aff, (0, 2, 1)).reshape(b * N_EXPERTS, s), cap)
        to_cols = lambda t: jnp.transpose(t.reshape(b, N_EXPERTS, s), (0, 2, 1))
        pos_c = to_cols(pos_r)
        idx = jnp.transpose(_compact(pos_c, cap), (1, 0, 2)).reshape(N_EXPERTS, b * cap)
        ye = _ffn(idx, h2, w_gate, w_up, w_down, l)
        x = _combine(pos_c, to_cols(gate_r), ye, x1, mod6)
    return x
```

```python
import functools

import jax
import jax.numpy as jnp
from jax import lax
from jax.experimental import pallas as pl
from jax.experimental.pallas import tpu as pltpu

F32 = jnp.float32
BF16 = jnp.bfloat16
SUBLANES = 8
LANES = 128

D_MODEL = 2048
ATT_HEAD_DIM = 128
ATT_HEADS = 8
ATT_KV_HEADS = 2
ATT_GROUP = ATT_HEADS // ATT_KV_HEADS
ATT_WIDTH = ATT_HEADS * ATT_HEAD_DIM
ATT_KV_WIDTH = ATT_KV_HEADS * ATT_HEAD_DIM
WINDOW = 128
ATT_BLOCK = 128
ATT_STEP_BLOCKS = 4
ROPE_THETA = 10000.0
M_HEADS = 4
M_V_DIM = 256
M_QK_DIM = 128
M_WIDTH = M_HEADS * M_V_DIM
M_QK_WIDTH = M_HEADS * M_QK_DIM
MLSTM_BLOCK = 256
N_GATES = 4 * M_HEADS
N_EXPERTS = 16
EXPERT_FF = D_MODEL // 2
CAPACITY_FACTOR = 2
EPS = 1e-6
NEG = -1e30

COL_AQ = 0
COL_MV = ATT_WIDTH
COL_MO = COL_MV + M_WIDTH
COL_AK = COL_MO + M_WIDTH
COL_AV = COL_AK + ATT_KV_WIDTH
COL_MQ = COL_AV + ATT_KV_WIDTH
COL_MK = COL_MQ + M_QK_WIDTH
PROJ_WIDTH = COL_MK + M_QK_WIDTH
IN_COL_BLOCKS = (0, 1, 5, 6, 7, 8, 2, 3, 4)

VMEM_LIMIT_BYTES = 56 * 1024 * 1024

ADA_TN = 1024
ADA_K_SPLITS = 8
INPROJ_TM = 1024
INPROJ_TN = 512
OUTPROJ_TM = 1024
OUTPROJ_TN = 512
FFN_UP_STEPS = 4
FFN_DOWN_STEPS = 2
COMBINE_TN = 512
COMBINE_BUILD_ROWS = 256
SELECT_BISECTIONS = 64
SELECT_MIN_UPPER = 1e-30


def _params(*sem):
    return pltpu.CompilerParams(dimension_semantics=sem, vmem_limit_bytes=VMEM_LIMIT_BYTES)


def _ada_kernel(c_ref, *refs):
    w_refs, b_ref, o_ref = refs[:ADA_K_SPLITS], refs[ADA_K_SPLITS], refs[ADA_K_SPLITS + 1]
    c = c_ref[...]
    c_act = c * jax.nn.sigmoid(c)
    c_hi = c_act.astype(BF16)
    c_lo = (c_act - c_hi.astype(F32)).astype(BF16)
    kb = c.shape[1] // ADA_K_SPLITS
    acc = b_ref[0]
    for s, w_ref in enumerate(w_refs):
        w = w_ref[0]
        w_hi = w.astype(BF16)
        w_lo = (w - w_hi.astype(F32)).astype(BF16)
        ks = slice(s * kb, (s + 1) * kb)
        acc = acc + jnp.dot(c_hi[:, ks], w_hi, preferred_element_type=F32)
        acc = acc + jnp.dot(c_lo[:, ks], w_hi, preferred_element_type=F32)
        acc = acc + jnp.dot(c_hi[:, ks], w_lo, preferred_element_type=F32)
    o_ref[0] = acc


def _ada(c_pad, w_ada, b_ada):
    depth, d, n = w_ada.shape
    rows = c_pad.shape[0]
    return pl.pallas_call(
        _ada_kernel,
        grid=(depth, n // ADA_TN),
        in_specs=(
            [pl.BlockSpec((rows, d), lambda l, j: (0, 0))]
            + [pl.BlockSpec((1, d // ADA_K_SPLITS, ADA_TN), functools.partial(lambda l, j, s: (l, s, j), s=s))
               for s in range(ADA_K_SPLITS)]
            + [pl.BlockSpec((1, 1, ADA_TN), lambda l, j: (l, 0, j))]),
        out_specs=pl.BlockSpec((1, rows, ADA_TN), lambda l, j: (l, 0, j)),
        out_shape=jax.ShapeDtypeStruct((depth, rows, n), F32),
        compiler_params=_params("arbitrary", "arbitrary"),
        name="ada_mod",
    )(c_pad, *([w_ada] * ADA_K_SPLITS), b_ada.reshape(depth, 1, n))


def _inproj_kernel(perm_ref, x_ref, sc_ref, sh_ref, w_ref, wg_ref, cos_ref, sin_ref, qg_ref, kg_ref,
                   o_ref, g_ref, h_scr):
    del perm_ref
    j = pl.program_id(2)

    @pl.when(j == 0)
    def _():
        x = x_ref[0]
        r = lax.rsqrt(jnp.mean(x * x, axis=-1, keepdims=True) + EPS)
        h = (x * r) * (1.0 + sc_ref[0]) + sh_ref[0]
        hb = h.astype(BF16)
        h_scr[...] = hb
        g_ref[0] = jnp.dot(hb, wg_ref[...], preferred_element_type=F32)

    y = jnp.dot(h_scr[...], w_ref[0], preferred_element_type=F32)

    head_ones = jnp.ones((ATT_HEAD_DIM, ATT_HEAD_DIM), BF16)

    def norm_rope(t, gain):
        ss = jnp.dot((t * t).astype(BF16), head_ones, preferred_element_type=F32)
        tn = t * lax.rsqrt(ss * (1.0 / ATT_HEAD_DIM) + EPS) * gain
        return tn * cos_ref[...] + pltpu.roll(tn, ATT_HEAD_DIM // 2, 1) * sin_ref[...]

    heads_per_tile = INPROJ_TN // ATT_HEAD_DIM
    q_tiles = ATT_WIDTH // INPROJ_TN
    kv_tile = COL_AK // INPROJ_TN
    scale = ATT_HEAD_DIM ** -0.5

    @pl.when(j < q_tiles)
    def _():
        for u in range(heads_per_tile):
            sl = slice(u * ATT_HEAD_DIM, (u + 1) * ATT_HEAD_DIM)
            o_ref[0, :, sl] = (norm_rope(y[:, sl], qg_ref[...]) * scale).astype(BF16)

    @pl.when(j == kv_tile)
    def _():
        for u in range(ATT_KV_HEADS):
            sl = slice(u * ATT_HEAD_DIM, (u + 1) * ATT_HEAD_DIM)
            o_ref[0, :, sl] = norm_rope(y[:, sl], kg_ref[...]).astype(BF16)
        o_ref[0, :, ATT_KV_WIDTH:] = y[:, ATT_KV_WIDTH:].astype(BF16)

    @pl.when(jnp.logical_and(j >= q_tiles, j != kv_tile))
    def _():
        o_ref[0] = y.astype(BF16)


def _inproj(x, mod6, w_bf, w_gate, layer, cos, sin_signed, q_gain, k_gain):
    b, s, d = x.shape
    tm, tn = INPROJ_TM, INPROJ_TN
    assert COL_AK % tn == 0 and ATT_WIDTH % tn == 0 and 2 * ATT_KV_WIDTH == tn
    grid_spec = pltpu.PrefetchScalarGridSpec(
        num_scalar_prefetch=1,
        grid=(b, s // tm, PROJ_WIDTH // tn),
        in_specs=[
            pl.BlockSpec((1, tm, d), lambda bi, i, j, perm: (bi, i, 0)),
            pl.BlockSpec((1, 1, d), lambda bi, i, j, perm: (bi * 6 + 1, 0, 0)),
            pl.BlockSpec((1, 1, d), lambda bi, i, j, perm: (bi * 6 + 0, 0, 0)),
            pl.BlockSpec((1, d, tn), lambda bi, i, j, perm: (layer, 0, perm[j])),
            pl.BlockSpec((d, N_GATES), lambda bi, i, j, perm: (0, 0)),
            pl.BlockSpec((tm, ATT_HEAD_DIM), lambda bi, i, j, perm: (i, 0)),
            pl.BlockSpec((tm, ATT_HEAD_DIM), lambda bi, i, j, perm: (i, 0)),
            pl.BlockSpec((1, ATT_HEAD_DIM), lambda bi, i, j, perm: (0, 0)),
            pl.BlockSpec((1, ATT_HEAD_DIM), lambda bi, i, j, perm: (0, 0)),
        ],
        out_specs=[
            pl.BlockSpec((1, tm, tn), lambda bi, i, j, perm: (bi, i, j)),
            pl.BlockSpec((1, tm, N_GATES), lambda bi, i, j, perm: (bi, i, 0)),
        ],
        scratch_shapes=[pltpu.VMEM((tm, d), BF16)],
    )
    return pl.pallas_call(
        _inproj_kernel,
        grid_spec=grid_spec,
        out_shape=[
            jax.ShapeDtypeStruct((b, s, PROJ_WIDTH), BF16),
            jax.ShapeDtypeStruct((b, s, N_GATES), F32),
        ],
        compiler_params=_params("arbitrary", "arbitrary", "arbitrary"),
        name="in_proj",
    )(jnp.asarray(IN_COL_BLOCKS, jnp.int32), x, mod6, mod6, w_bf, w_gate, cos, sin_signed, q_gain, k_gain)


def _attn_kernel(sink_ref, q_ref, kl_ref, kc_ref, kr_ref, vl_ref, vc_ref, vr_ref, o_ref, *, n_steps):
    n = pl.program_id(1)
    L = ATT_BLOCK
    rows = ATT_GROUP * L
    i = lax.broadcasted_iota(jnp.int32, (rows, 3 * L), 0) & (L - 1)
    jj = lax.broadcasted_iota(jnp.int32, (rows, 3 * L), 1)
    in_window = jnp.abs(jj - L - i) <= WINDOW
    first_mask = in_window & ((jj >= L) | (n > 0))
    last_mask = in_window & ((jj < 2 * L) | (n < n_steps - 1))
    rgrp = lax.broadcasted_iota(jnp.int32, (rows, 1), 0) // L
    for kv in range(ATT_KV_HEADS):
        hs = slice(kv * ATT_HEAD_DIM, (kv + 1) * ATT_HEAD_DIM)
        kband = jnp.concatenate([kl_ref[0, :, hs], kc_ref[0, :, hs], kr_ref[0, :, hs]], axis=0)
        vband = jnp.concatenate([vl_ref[0, :, hs], vc_ref[0, :, hs], vr_ref[0, :, hs]], axis=0)
        sink = jnp.zeros((rows, 1), F32)
        for g in range(ATT_GROUP):
            sink = jnp.where(rgrp == g, sink_ref[kv * ATT_GROUP + g], sink)
        for blk in range(ATT_STEP_BLOCKS):
            qrows = slice(blk * L, (blk + 1) * L)
            q = jnp.concatenate(
                [q_ref[0, qrows, (kv * ATT_GROUP + g) * ATT_HEAD_DIM:(kv * ATT_GROUP + g + 1) * ATT_HEAD_DIM]
                 for g in range(ATT_GROUP)], axis=0)
            kb = kband[blk * L:(blk + 3) * L]
            vb = vband[blk * L:(blk + 3) * L]
            valid = first_mask if blk == 0 else (last_mask if blk == ATT_STEP_BLOCKS - 1 else in_window)
            s = lax.dot_general(q, kb, (((1,), (1,)), ((), ())), preferred_element_type=F32)
            s = jnp.where(valid, s, NEG)
            m = jnp.maximum(jnp.max(s, axis=-1, keepdims=True), sink)
            p = jnp.exp(s - m)
            denom = jnp.sum(p, axis=-1, keepdims=True) + jnp.exp(sink - m)
            o = jnp.dot(p.astype(BF16), vb, preferred_element_type=F32) / denom
            for g in range(ATT_GROUP):
                h = kv * ATT_GROUP + g
                o_ref[0, qrows, h * ATT_HEAD_DIM:(h + 1) * ATT_HEAD_DIM] = o[g * L:(g + 1) * L].astype(BF16)


def _attention(proj, sink):
    b, s, _ = proj.shape
    L = ATT_BLOCK
    nb = s // L
    sb = ATT_STEP_BLOCKS
    assert sb >= 2 and nb % sb == 0
    kblk = COL_AK // ATT_KV_WIDTH
    vblk = COL_AV // ATT_KV_WIDTH
    left = lambda col: pl.BlockSpec((1, L, ATT_KV_WIDTH), lambda bi, n: (bi, jnp.maximum(n * sb - 1, 0), col))
    right = lambda col: pl.BlockSpec((1, L, ATT_KV_WIDTH), lambda bi, n: (bi, jnp.minimum((n + 1) * sb, nb - 1), col))
    centre = lambda col: pl.BlockSpec((1, sb * L, ATT_KV_WIDTH), lambda bi, n: (bi, n, col))
    return pl.pallas_call(
        functools.partial(_attn_kernel, n_steps=nb // sb),
        grid=(b, nb // sb),
        in_specs=[
            pl.BlockSpec(memory_space=pltpu.SMEM),
            pl.BlockSpec((1, sb * L, ATT_WIDTH), lambda bi, n: (bi, n, COL_AQ // ATT_WIDTH)),
            left(kblk), centre(kblk), right(kblk),
            left(vblk), centre(vblk), right(vblk),
        ],
        out_specs=pl.BlockSpec((1, sb * L, ATT_WIDTH), lambda bi, n: (bi, n, 0)),
        out_shape=jax.ShapeDtypeStruct((b, s, ATT_WIDTH), BF16),
        compiler_params=_params("arbitrary", "arbitrary"),
        name="window_attn",
    )(sink, proj, proj, proj, proj, proj, proj, proj)


def _split3(x):
    hi = x.astype(BF16)
    rest = x - hi.astype(F32)
    mid = rest.astype(BF16)
    return hi, mid, (rest - mid.astype(F32)).astype(BF16)


def _mlstm_kernel(q_ref, k_ref, v_ref, mo_ref, gcol_ref, grow_ref, bcol_ref, brow_ref, gain_ref, o_ref,
                  bc_scr, kt_scr, ar_scr, wr_scr, bt_scr, c_scr, h_scr, *, seq):
    L = MLSTM_BLOCK
    nc = seq // L
    ns = 2 * M_HEADS
    scale = M_QK_DIM ** -0.5
    row = lax.broadcasted_iota(jnp.int32, (L, L), 0)
    col = lax.broadcasted_iota(jnp.int32, (L, L), 1)
    lower = col <= row
    upper = col >= row
    tril = jnp.where(lower, 1.0, 0.0).astype(BF16)
    triu = jnp.where(upper, 1.0, 0.0).astype(BF16)

    fwd_lane = lax.broadcasted_iota(jnp.int32, (1, N_GATES), 1) < ns + M_HEADS
    for c in range(nc):
        rows = slice(c * L, (c + 1) * L)
        lf = jax.nn.log_sigmoid(gcol_ref[0, rows, :] + bcol_ref[...])
        lf3 = _split3(lf)
        pre = sum(jnp.dot(tril, part, preferred_element_type=F32) for part in lf3)
        suf = sum(jnp.dot(triu, part, preferred_element_type=F32) for part in lf3)
        bcol = jnp.where(fwd_lane, pre, suf)
        for k in range(ns):
            bc_scr[k, rows, :] = jnp.broadcast_to(bcol[:, ns + k:ns + k + 1], (L, LANES))
        kt_scr[:, rows] = k_ref[0, rows, :].astype(F32).T.astype(BF16)

    gr = grow_ref[0] + brow_ref[...]
    ig_r = gr[:ns].reshape(ns * nc, L)
    lf_r = jax.nn.log_sigmoid(gr[ns:]).reshape(ns * nc, L)
    lf_r3 = _split3(lf_r)
    pre_r = sum(jnp.dot(part, triu, preferred_element_type=F32) for part in lf_r3)
    suf_r = sum(jnp.dot(part, tril, preferred_element_type=F32) for part in lf_r3)
    fwd_rows = lax.broadcasted_iota(jnp.int32, (ns * nc, 1), 0) < M_HEADS * nc
    a_r = ig_r - jnp.where(fwd_rows, pre_r, suf_r)
    btot = jnp.sum(lf_r, axis=-1, keepdims=True)
    ar_scr[...] = a_r.reshape(ns, nc, L)
    wr_scr[...] = (btot + a_r).reshape(ns, nc, L)
    bt_scr[...] = jnp.broadcast_to(btot, (ns * nc, LANES)).reshape(ns, nc, LANES)

    c_scr[...] = jnp.zeros_like(c_scr)
    h_scr[...] = jnp.zeros_like(h_scr)
    ones_cols = jnp.ones((L, LANES), BF16)

    twice = lambda t: jnp.concatenate([t, t], axis=1)

    def chunk_step(k, c, m_st):
        d, h = divmod(k, M_HEADS)
        r0 = pl.multiple_of(c * L, L)
        q = q_ref[0, pl.ds(r0, L), h * M_QK_DIM:(h + 1) * M_QK_DIM]
        kk = k_ref[0, pl.ds(r0, L), h * M_QK_DIM:(h + 1) * M_QK_DIM]
        v_ext = jnp.concatenate([v_ref[0, pl.ds(r0, L), h * M_V_DIM:(h + 1) * M_V_DIM], ones_cols], axis=1)
        bc = bc_scr[k, pl.ds(r0, L), :]
        ar = ar_scr[k, pl.ds(c, 1), :]
        wr = wr_scr[k, pl.ds(c, 1), :]
        bt = bt_scr[k, pl.ds(c, 1), :]
        dm = jnp.where(lower if d == 0 else upper, twice(bc) + ar, NEG)
        g_inter = bc + m_st
        m_t = jnp.maximum(jnp.broadcast_to(jnp.max(dm, axis=-1, keepdims=True), (L, LANES)), g_inter)
        e_inter = jnp.exp(g_inter - m_t) * scale
        s_qk = lax.dot_general(q, kk, (((1,), (1,)), ((), ())), preferred_element_type=F32)
        p = s_qk * scale * jnp.exp(dm - twice(m_t))
        qc = jnp.dot(q, c_scr[k].astype(BF16), preferred_element_type=F32)
        pv = jnp.dot(p.astype(BF16), v_ext, preferred_element_type=F32)
        num = pv[:, :M_V_DIM] + twice(e_inter) * qc[:, :M_V_DIM]
        den = pv[:, M_V_DIM:] + e_inter * qc[:, M_V_DIM:]
        inv = 1.0 / jnp.maximum(jnp.abs(den), jnp.exp(-m_t))
        hsl = (pl.ds(r0, L), slice(h * M_V_DIM, (h + 1) * M_V_DIM))
        h_scr[hsl] = h_scr[hsl] + num * twice(inv)
        m_new = jnp.maximum(bt + m_st, jnp.broadcast_to(jnp.max(wr, axis=-1, keepdims=True), (1, LANES)))
        a = jnp.exp(bt + m_st - m_new)
        kt = kt_scr[h * M_QK_DIM:(h + 1) * M_QK_DIM, pl.ds(r0, L)]
        ek_t = (kt.astype(F32) * jnp.exp(wr - twice(m_new))).astype(BF16)
        c_scr[k] = jnp.concatenate([a, a, a], axis=1) * c_scr[k] + jnp.dot(ek_t, v_ext, preferred_element_type=F32)
        return m_new

    def body(it, ms):
        out = []
        for k in range(ns):
            c = it if k < M_HEADS else nc - 1 - it
            out.append(chunk_step(k, c, ms[k]))
        return tuple(out)

    lax.fori_loop(0, nc, body, tuple(jnp.zeros((1, LANES), F32) for _ in range(ns)))

    def fin(i, carry):
        r0 = pl.multiple_of(i * L, L)
        for h in range(M_HEADS):
            sl = slice(h * M_V_DIM, (h + 1) * M_V_DIM)
            x = h_scr[pl.ds(r0, L), sl]
            y = x * lax.rsqrt(jnp.mean(x * x, axis=-1, keepdims=True) + EPS) * gain_ref[:, sl]
            o_ref[0, pl.ds(r0, L), sl] = (y * jax.nn.sigmoid(mo_ref[0, pl.ds(r0, L), sl].astype(F32))).astype(BF16)
        return carry

    lax.fori_loop(0, nc, fin, 0)


def _mlstm(proj, gates, b_gates, m_gain):
    b, s, _ = proj.shape
    L = MLSTM_BLOCK
    nc = s // L
    ns = 2 * M_HEADS
    grow = jnp.transpose(gates, (0, 2, 1)).reshape(b, N_GATES, nc, L)
    col = lambda width, off: pl.BlockSpec((1, s, width), lambda bi: (bi, 0, off // width))
    return pl.pallas_call(
        functools.partial(_mlstm_kernel, seq=s),
        grid=(b,),
        in_specs=[
            col(M_QK_WIDTH, COL_MQ), col(M_QK_WIDTH, COL_MK), col(M_WIDTH, COL_MV),
            pl.BlockSpec((1, s, M_WIDTH), lambda bi: (bi, 0, COL_MO // M_WIDTH), pipeline_mode=pl.Buffered(1)),
            pl.BlockSpec((1, s, N_GATES), lambda bi: (bi, 0, 0)),
            pl.BlockSpec((1, N_GATES, nc, L), lambda bi: (bi, 0, 0, 0)),
            pl.BlockSpec((1, N_GATES), lambda bi: (0, 0)),
            pl.BlockSpec((N_GATES, 1, 1), lambda bi: (0, 0, 0)),
            pl.BlockSpec((1, M_WIDTH), lambda bi: (0, 0)),
        ],
        out_specs=pl.BlockSpec((1, s, M_WIDTH), lambda bi: (bi, 0, 0)),
        out_shape=jax.ShapeDtypeStruct((b, s, M_WIDTH), BF16),
        scratch_shapes=[
            pltpu.VMEM((ns, s, LANES), F32),
            pltpu.VMEM((M_QK_WIDTH, s), BF16),
            pltpu.VMEM((ns, nc, L), F32), pltpu.VMEM((ns, nc, L), F32), pltpu.VMEM((ns, nc, LANES), F32),
            pltpu.VMEM((ns, M_QK_DIM, M_V_DIM + LANES), F32),
            pltpu.VMEM((s, M_WIDTH), F32),
        ],
        compiler_params=_params("arbitrary"),
        name="mlstm",
    )(proj, proj, proj, proj, gates, grow, b_gates.reshape(1, N_GATES), b_gates.reshape(N_GATES, 1, 1),
      m_gain.reshape(1, M_WIDTH))


def _outproj_kernel(att_ref, mo_ref, wa_ref, wm_ref, x_ref, g1_ref, sc_ref, sh_ref, wr_ref,
                    x1_ref, h2_ref, aff_ref):
    j = pl.program_id(2)
    nj = pl.num_programs(2)
    tn = OUTPROJ_TN
    mix = (jnp.dot(att_ref[0], wa_ref[...], preferred_element_type=F32)
           + jnp.dot(mo_ref[0], wm_ref[...], preferred_element_type=F32))
    x1t = x_ref[0] + g1_ref[0] * mix
    for jj in range(D_MODEL // tn):
        @pl.when(j == jj)
        def _(jj=jj):
            x1_ref[0, :, jj * tn:(jj + 1) * tn] = x1t

    @pl.when(j == nj - 1)
    def _():
        xf = x1_ref[0]
        r = lax.rsqrt(jnp.mean(xf * xf, axis=-1, keepdims=True) + EPS)
        h2 = (xf * r) * (1.0 + sc_ref[0]) + sh_ref[0]
        h2_hi = h2.astype(BF16)
        h2_ref[0] = h2
        h2_lo = (h2 - h2_hi.astype(F32)).astype(BF16)
        hi_terms = jnp.dot(h2_hi, wr_ref[...], preferred_element_type=F32)
        lo_term = jnp.dot(h2_lo, wr_ref[:, :N_EXPERTS], preferred_element_type=F32)
        logits = hi_terms[:, :N_EXPERTS] + hi_terms[:, N_EXPERTS:] + lo_term
        e = jnp.exp(logits - jnp.max(logits, axis=-1, keepdims=True))
        aff_ref[0] = e / jnp.sum(e, axis=-1, keepdims=True)


def _outproj(att, m_out, w_out, x, mod6, w_router):
    b, s, d = x.shape
    tm, tn = OUTPROJ_TM, OUTPROJ_TN
    half = ATT_WIDTH
    assert w_out.shape[0] == 2 * half and M_WIDTH == half
    wr_hi = w_router.astype(BF16)
    wr_lo = (w_router - wr_hi.astype(F32)).astype(BF16)
    return pl.pallas_call(
        _outproj_kernel,
        grid=(b, s // tm, d // tn),
        in_specs=[
            pl.BlockSpec((1, tm, ATT_WIDTH), lambda bi, i, j: (bi, i, 0)),
            pl.BlockSpec((1, tm, M_WIDTH), lambda bi, i, j: (bi, i, 0)),
            pl.BlockSpec((half, tn), lambda bi, i, j: (0, j)),
            pl.BlockSpec((half, tn), lambda bi, i, j: (1, j)),
            pl.BlockSpec((1, tm, tn), lambda bi, i, j: (bi, i, j)),
            pl.BlockSpec((1, 1, tn), lambda bi, i, j: (bi * 6 + 2, 0, j)),
            pl.BlockSpec((1, 1, d), lambda bi, i, j: (bi * 6 + 4, 0, 0)),
            pl.BlockSpec((1, 1, d), lambda bi, i, j: (bi * 6 + 3, 0, 0)),
            pl.BlockSpec((d, 2 * N_EXPERTS), lambda bi, i, j: (0, 0)),
        ],
        out_specs=[
            pl.BlockSpec((1, tm, d), lambda bi, i, j: (bi, i, 0)),
            pl.BlockSpec((1, tm, d), lambda bi, i, j: (bi, i, 0)),
            pl.BlockSpec((1, tm, N_EXPERTS), lambda bi, i, j: (bi, i, 0)),
        ],
        out_shape=[
            jax.ShapeDtypeStruct((b, s, d), F32),
            jax.ShapeDtypeStruct((b, s, d), F32),
            jax.ShapeDtypeStruct((b, s, N_EXPERTS), F32),
        ],
        compiler_params=_params("arbitrary", "arbitrary", "arbitrary"),
        name="out_proj",
    )(att, m_out, w_out, w_out, x, mod6, mod6, mod6, jnp.concatenate([wr_hi, wr_lo], axis=1))


def _prefix_sum_lanes(x, n):
    lane = lax.broadcasted_iota(jnp.int32, x.shape, 1)
    sh = 1
    while sh < n:
        x = x + jnp.where(lane >= sh, pltpu.roll(x, sh, 1), 0.0)
        sh *= 2
    return x


def _select_kernel(aff_ref, pos_ref, gate_ref, *, cap, seq):
    a = aff_ref[...]
    ne = a.shape[0]

    def body(_, lohi):
        lo, hi = lohi
        mid = jnp.where(lo > 0.0, 0.5 * (lo + hi), hi * (1.0 / 256.0))
        cnt = jnp.sum(jnp.where(a >= mid, 1.0, 0.0), axis=-1, keepdims=True)
        ok = cnt >= cap
        return jnp.where(ok, mid, lo), jnp.where(ok, hi, mid)

    lo0 = jnp.zeros((ne, 1), F32)
    hi0 = jnp.maximum(2.0 * jnp.max(a, axis=-1, keepdims=True), SELECT_MIN_UPPER)
    lo, hi = lax.fori_loop(0, SELECT_BISECTIONS, body, (lo0, hi0))
    above = a >= hi
    band = jnp.logical_and(a >= lo, jnp.logical_not(above))
    n_above = jnp.sum(jnp.where(above, 1.0, 0.0), axis=-1, keepdims=True)
    band_rank = _prefix_sum_lanes(jnp.where(band, 1.0, 0.0), seq)
    sel = jnp.where(above, 1.0, jnp.where(band, jnp.where(band_rank <= cap - n_above, 1.0, 0.0), 0.0))
    pos = _prefix_sum_lanes(sel, seq) - 1.0
    chosen = sel > 0.5
    pos_ref[...] = jnp.where(chosen, pos, -1.0)
    gate_ref[...] = jnp.where(chosen, a, 0.0)


def _select(aff_rows, cap):
    n, s = aff_rows.shape
    spec = pl.BlockSpec((n, s), lambda i: (0, 0))
    return pl.pallas_call(
        functools.partial(_select_kernel, cap=cap, seq=s),
        grid=(1,),
        in_specs=[spec],
        out_specs=[spec, spec],
        out_shape=[jax.ShapeDtypeStruct((n, s), F32), jax.ShapeDtypeStruct((n, s), F32)],
        compiler_params=_params("arbitrary"),
        name="expert_select",
    )(aff_rows)


def _lane_spread(ne):
    blk = lax.broadcasted_iota(jnp.int32, (ne, ne * LANES), 1) // LANES
    return jnp.where(blk == lax.broadcasted_iota(jnp.int32, (ne, ne * LANES), 0), 1.0, 0.0).astype(BF16)


def _compact_kernel(pos_ref, idx_ref, *, cap):
    seq, ne = pos_ref.shape[1], pos_ref.shape[2]
    rt = COMBINE_BUILD_ROWS
    halves = cap // LANES
    spread = _lane_spread(ne)
    lane = lax.broadcasted_iota(jnp.int32, (rt, LANES), 1).astype(F32)
    tok0 = lax.broadcasted_iota(jnp.int32, (rt, LANES), 0).astype(F32)

    def body(i, accs):
        r0 = pl.multiple_of(i * rt, rt)
        pos = jnp.dot(pos_ref[0, pl.ds(r0, rt), :].astype(BF16), spread, preferred_element_type=F32)
        tok = tok0 + r0.astype(F32)
        out = []
        for e in range(ne):
            p = pos[:, e * LANES:(e + 1) * LANES]
            for h in range(halves):
                hit = lane + float(h * LANES) == p
                out.append(accs[e * halves + h] + jnp.sum(jnp.where(hit, tok, 0.0), axis=0, keepdims=True))
        return tuple(out)

    accs = lax.fori_loop(0, seq // rt, body, tuple(jnp.zeros((1, LANES), F32) for _ in range(ne * halves)))
    for e in range(ne):
        for h in range(halves):
            idx_ref[0, e:e + 1, h * LANES:(h + 1) * LANES] = accs[e * halves + h].astype(jnp.int32)


def _compact(pos_cols, cap):
    b, s, ne = pos_cols.shape
    return pl.pallas_call(
        functools.partial(_compact_kernel, cap=cap),
        grid=(b,),
        in_specs=[pl.BlockSpec((1, s, ne), lambda bi: (bi, 0, 0))],
        out_specs=pl.BlockSpec((1, ne, cap), lambda bi: (bi, 0, 0)),
        out_shape=jax.ShapeDtypeStruct((b, ne, cap), jnp.int32),
        compiler_params=_params("arbitrary"),
        name="expert_compact",
    )(pos_cols)


def _ffn_kernel(idx_ref, h2_hbm, wg_ref, wu_ref, wd0_ref, wd1_ref, ye_ref, x_scr, hid_scr, sem, *, nb, cap):
    e = pl.program_id(0)
    f = pl.program_id(1)
    ne = pl.num_programs(0)
    nf = pl.num_programs(1)
    rows = nb * cap
    per_step = rows // FFN_UP_STEPS
    tf = wg_ref.shape[-1]
    slot = lax.rem(e, 2)
    nslot = 1 - slot
    nxt = jnp.minimum(e + 1, ne - 1)

    def row_copy(expert, r, b, sl):
        tok = idx_ref[expert, r]
        return pltpu.make_async_copy(h2_hbm.at[b, pl.ds(tok, 1), :], x_scr.at[sl, pl.ds(r, 1), :], sem.at[sl])

    def wait_rows(sl):
        pltpu.make_async_copy(x_scr.at[sl], x_scr.at[sl], sem.at[sl]).wait()

    @pl.when(jnp.logical_and(e == 0, f == 0))
    def _():
        def first(r, carry):
            row_copy(0, r, r // cap, 0).start()
            return carry
        lax.fori_loop(0, rows, first, 0)

    for ff in range(FFN_UP_STEPS):
        for cur in range(2):
            @pl.when(jnp.logical_and(f == ff, slot == cur))
            def _(ff=ff, cur=cur):
                if ff == 0:
                    wait_rows(cur)
                for r in range(ff * per_step, (ff + 1) * per_step):
                    row_copy(nxt, r, r // cap, 1 - cur).start(priority=r % 2)
                x = x_scr[cur].astype(BF16)
                g = jnp.dot(x, wg_ref[0, 0].astype(BF16), preferred_element_type=F32)
                u = jnp.dot(x, wu_ref[0, 0].astype(BF16), preferred_element_type=F32)
                hid_scr[:, ff * tf:(ff + 1) * tf] = ((g * jax.nn.sigmoid(g)) * u).astype(BF16)

    @pl.when(f >= FFN_UP_STEPS)
    def _():
        kh = hid_scr.shape[1] // 2
        y = (jnp.dot(hid_scr[:, :kh], wd0_ref[0, 0].astype(BF16), preferred_element_type=F32)
             + jnp.dot(hid_scr[:, kh:], wd1_ref[0, 0].astype(BF16), preferred_element_type=F32))
        ye_ref[...] = y.reshape(nb, 1, cap, y.shape[-1]).astype(BF16)

    @pl.when(jnp.logical_and(e == ne - 1, f == nf - 1))
    def _():
        wait_rows(nslot)


def _ffn(idx, h2, w_gate, w_up, w_down, layer):
    b, s, d = h2.shape
    ne = idx.shape[0]
    cap = idx.shape[1] // b
    ff = w_gate.shape[-1]
    tf = ff // FFN_UP_STEPS
    dn = d // FFN_DOWN_STEPS
    up = lambda f: jnp.minimum(f, FFN_UP_STEPS - 1)
    down = lambda f: jnp.maximum(f - FFN_UP_STEPS, 0)
    grid_spec = pltpu.PrefetchScalarGridSpec(
        num_scalar_prefetch=1,
        grid=(ne, FFN_UP_STEPS + FFN_DOWN_STEPS),
        in_specs=[
            pl.BlockSpec(memory_space=pl.ANY),
            pl.BlockSpec((1, 1, d, tf), lambda e, f, idx_ref: (layer, e, 0, up(f))),
            pl.BlockSpec((1, 1, d, tf), lambda e, f, idx_ref: (layer, e, 0, up(f))),
            pl.BlockSpec((1, 1, ff // 2, dn), lambda e, f, idx_ref: (layer, e, 0, down(f))),
            pl.BlockSpec((1, 1, ff // 2, dn), lambda e, f, idx_ref: (layer, e, 1, down(f))),
        ],
        out_specs=pl.BlockSpec((b, 1, cap, dn), lambda e, f, idx_ref: (0, e, 0, down(f))),
        scratch_shapes=[
            pltpu.VMEM((2, b * cap, d), F32),
            pltpu.VMEM((b * cap, ff), BF16),
            pltpu.SemaphoreType.DMA((2,)),
        ],
    )
    return pl.pallas_call(
        functools.partial(_ffn_kernel, nb=b, cap=cap),
        grid_spec=grid_spec,
        out_shape=jax.ShapeDtypeStruct((b, ne, cap, d), BF16),
        compiler_params=_params("arbitrary", "arbitrary"),
        name="expert_ffn",
    )(idx, h2, w_gate, w_up, w_down, w_down)


def _combine_kernel(pos_ref, gate_ref, ye_ref, x1_ref, g2_ref, o_ref, scat_scr, *, cap):
    j = pl.program_id(1)
    seq, ne = pos_ref.shape[1], pos_ref.shape[2]
    rt = COMBINE_BUILD_ROWS

    @pl.when(j == 0)
    def _():
        spread = _lane_spread(ne)
        lane = lax.broadcasted_iota(jnp.int32, (rt, LANES), 1).astype(F32)

        def build(i, carry):
            r0 = pl.multiple_of(i * rt, rt)
            pos = jnp.dot(pos_ref[0, pl.ds(r0, rt), :].astype(BF16), spread, preferred_element_type=F32)
            gate = jnp.dot(gate_ref[0, pl.ds(r0, rt), :].astype(BF16), spread, preferred_element_type=F32)
            for e in range(ne):
                p = pos[:, e * LANES:(e + 1) * LANES]
                g = gate[:, e * LANES:(e + 1) * LANES]
                for h in range(cap // LANES):
                    cols = slice(e * cap + h * LANES, e * cap + (h + 1) * LANES)
                    scat_scr[pl.ds(r0, rt), cols] = jnp.where(lane + float(h * LANES) == p, g, 0.0).astype(BF16)
            return carry

        lax.fori_loop(0, seq // rt, build, 0)

    acc = jnp.dot(scat_scr[...], ye_ref[0], preferred_element_type=F32)
    o_ref[0] = x1_ref[0] + g2_ref[0] * acc


def _combine(pos_cols, gate_cols, ye, x1, mod6):
    b, s, d = x1.shape
    ne, cap = ye.shape[1], ye.shape[2]
    tn = COMBINE_TN
    return pl.pallas_call(
        functools.partial(_combine_kernel, cap=cap),
        grid=(b, d // tn),
        in_specs=[
            pl.BlockSpec((1, s, ne), lambda bi, j: (bi, 0, 0)),
            pl.BlockSpec((1, s, ne), lambda bi, j: (bi, 0, 0)),
            pl.BlockSpec((1, ne * cap, tn), lambda bi, j: (bi, 0, j)),
            pl.BlockSpec((1, s, tn), lambda bi, j: (bi, 0, j)),
            pl.BlockSpec((1, 1, tn), lambda bi, j: (bi * 6 + 5, 0, j)),
        ],
        out_specs=pl.BlockSpec((1, s, tn), lambda bi, j: (bi, 0, j)),
        out_shape=jax.ShapeDtypeStruct((b, s, d), F32),
        scratch_shapes=[pltpu.VMEM((s, ne * cap), BF16)],
        compiler_params=_params("arbitrary", "arbitrary"),
        name="expert_combine",
    )(pos_cols, gate_cols, ye.reshape(b, ne * cap, d), x1, mod6)


def _rope_tables(seq):
    inv = 1.0 / (ROPE_THETA ** (jnp.arange(0, ATT_HEAD_DIM, 2, dtype=F32) / ATT_HEAD_DIM))
    ang = jnp.arange(seq, dtype=F32)[:, None] * inv[None, :]
    ang = jnp.concatenate([ang, ang], axis=-1)
    sign = jnp.where(jnp.arange(ATT_HEAD_DIM) < ATT_HEAD_DIM // 2, -1.0, 1.0).astype(F32)
    return jnp.cos(ang), jnp.sin(ang) * sign


def kernel(x, c, w_ada, b_ada, w_in, b_gates, q_gain, k_gain, sink, m_gain, w_out,
           w_router, w_gate, w_up, w_down):
    b, s, d = x.shape
    depth = w_ada.shape[0]
    cap = CAPACITY_FACTOR * s // N_EXPERTS
    cos, sin_signed = _rope_tables(s)
    c_pad = jnp.pad(c, ((0, SUBLANES - b), (0, 0)))
    mod = _ada(c_pad, w_ada, b_ada)
    w_in_bf = w_in.astype(BF16)
    for l in range(depth):
        mod6 = mod[l, :b].reshape(b * 6, 1, d)
        proj, gates = _inproj(x, mod6, w_in_bf, w_in[l, :, PROJ_WIDTH:].astype(BF16), l, cos, sin_signed,
                              q_gain[l].reshape(1, -1), k_gain[l].reshape(1, -1))
        att = _attention(proj, sink[l])
        m_out = _mlstm(proj, gates, b_gates[l].reshape(-1), m_gain[l])
        x1, h2, aff = _outproj(att, m_out, w_out[l].astype(BF16), x, mod6, w_router[l])
        pos_r, gate_r = _select(jnp.transpose(aff, (0, 2, 1)).reshape(b * N_EXPERTS, s), cap)
        to_cols = lambda t: jnp.transpose(t.reshape(b, N_EXPERTS, s), (0, 2, 1))
        pos_c = to_cols(pos_r)
        idx = jnp.transpose(_compact(pos_c, cap), (1, 0, 2)).reshape(N_EXPERTS, b * cap)
        ye = _ffn(idx, h2, w_gate, w_up, w_down, l)
        x = _combine(pos_c, to_cols(gate_r), ye, x1, mod6)
    return x
```

```python
import functools

import jax
import jax.numpy as jnp
from jax import lax
from jax.experimental import pallas as pl
from jax.experimental.pallas import tpu as pltpu

F32 = jnp.float32
BF16 = jnp.bfloat16
SUBLANES = 8
LANES = 128

D_MODEL = 2048
ATT_HEAD_DIM = 128
ATT_HEADS = 8
ATT_KV_HEADS = 2
ATT_GROUP = ATT_HEADS // ATT_KV_HEADS
ATT_WIDTH = ATT_HEADS * ATT_HEAD_DIM
ATT_KV_WIDTH = ATT_KV_HEADS * ATT_HEAD_DIM
WINDOW = 128
ATT_BLOCK = 128
ATT_STEP_BLOCKS = 4
ROPE_THETA = 10000.0
M_HEADS = 4
M_V_DIM = 256
M_QK_DIM = 128
M_WIDTH = M_HEADS * M_V_DIM
M_QK_WIDTH = M_HEADS * M_QK_DIM
MLSTM_BLOCK = 256
N_GATES = 4 * M_HEADS
N_EXPERTS = 16
EXPERT_FF = D_MODEL // 2
CAPACITY_FACTOR = 2
EPS = 1e-6
NEG = -1e30

COL_AQ = 0
COL_MV = ATT_WIDTH
COL_MO = COL_MV + M_WIDTH
COL_AK = COL_MO + M_WIDTH
COL_AV = COL_AK + ATT_KV_WIDTH
COL_MQ = COL_AV + ATT_KV_WIDTH
COL_MK = COL_MQ + M_QK_WIDTH
PROJ_WIDTH = COL_MK + M_QK_WIDTH
IN_COL_BLOCKS = (0, 1, 5, 6, 7, 8, 2, 3, 4)

VMEM_LIMIT_BYTES = 56 * 1024 * 1024

ADA_TN = 1024
ADA_K_SPLITS = 4
INPROJ_TM = 1024
INPROJ_TN = 512
OUTPROJ_TM = 1024
OUTPROJ_TN = 512
FFN_UP_STEPS = 4
FFN_DOWN_STEPS = 2
COMBINE_TN = 512
COMBINE_BUILD_ROWS = 256
SELECT_BISECTIONS = 56
SELECT_MIN_UPPER = 1e-30


def _params(*sem):
    return pltpu.CompilerParams(dimension_semantics=sem, vmem_limit_bytes=VMEM_LIMIT_BYTES)


def _ada_kernel(c_ref, *refs):
    w_refs, b_ref, o_ref = refs[:ADA_K_SPLITS], refs[ADA_K_SPLITS], refs[ADA_K_SPLITS + 1]
    c = c_ref[...]
    c_act = c * jax.nn.sigmoid(c)
    c_hi = c_act.astype(BF16)
    c_lo = (c_act - c_hi.astype(F32)).astype(BF16)
    kb = c.shape[1] // ADA_K_SPLITS
    acc = b_ref[0]
    for s, w_ref in enumerate(w_refs):
        w = w_ref[0]
        w_hi = w.astype(BF16)
        w_lo = (w - w_hi.astype(F32)).astype(BF16)
        ks = slice(s * kb, (s + 1) * kb)
        acc = acc + jnp.dot(c_hi[:, ks], w_hi, preferred_element_type=F32)
        acc = acc + jnp.dot(c_lo[:, ks], w_hi, preferred_element_type=F32)
        acc = acc + jnp.dot(c_hi[:, ks], w_lo, preferred_element_type=F32)
    o_ref[0] = acc


def _ada(c_pad, w_ada, b_ada):
    depth, d, n = w_ada.shape
    rows = c_pad.shape[0]
    return pl.pallas_call(
        _ada_kernel,
        grid=(depth, n // ADA_TN),
        in_specs=(
            [pl.BlockSpec((rows, d), lambda l, j: (0, 0))]
            + [pl.BlockSpec((1, d // ADA_K_SPLITS, ADA_TN), functools.partial(lambda l, j, s: (l, s, j), s=s))
               for s in range(ADA_K_SPLITS)]
            + [pl.BlockSpec((1, 1, ADA_TN), lambda l, j: (l, 0, j))]),
        out_specs=pl.BlockSpec((1, rows, ADA_TN), lambda l, j: (l, 0, j)),
        out_shape=jax.ShapeDtypeStruct((depth, rows, n), F32),
        compiler_params=_params("arbitrary", "arbitrary"),
        name="ada_mod",
    )(c_pad, *([w_ada] * ADA_K_SPLITS), b_ada.reshape(depth, 1, n))


def _inproj_kernel(perm_ref, x_ref, sc_ref, sh_ref, w_ref, wg_ref, cos_ref, sin_ref, qg_ref, kg_ref,
                   o_ref, g_ref, h_scr):
    del perm_ref
    j = pl.program_id(2)

    @pl.when(j == 0)
    def _():
        x = x_ref[0]
        r = lax.rsqrt(jnp.mean(x * x, axis=-1, keepdims=True) + EPS)
        h = (x * r) * (1.0 + sc_ref[0]) + sh_ref[0]
        hb = h.astype(BF16)
        h_scr[...] = hb
        g_ref[0] = jnp.dot(hb, wg_ref[...], preferred_element_type=F32)

    y = jnp.dot(h_scr[...], w_ref[0], preferred_element_type=F32)

    head_ones = jnp.ones((ATT_HEAD_DIM, ATT_HEAD_DIM), BF16)

    def norm_rope(t, gain):
        ss = jnp.dot((t * t).astype(BF16), head_ones, preferred_element_type=F32)
        tn = t * lax.rsqrt(ss * (1.0 / ATT_HEAD_DIM) + EPS) * gain
        return tn * cos_ref[...] + pltpu.roll(tn, ATT_HEAD_DIM // 2, 1) * sin_ref[...]

    heads_per_tile = INPROJ_TN // ATT_HEAD_DIM
    q_tiles = ATT_WIDTH // INPROJ_TN
    kv_tile = COL_AK // INPROJ_TN
    scale = ATT_HEAD_DIM ** -0.5

    @pl.when(j < q_tiles)
    def _():
        for u in range(heads_per_tile):
            sl = slice(u * ATT_HEAD_DIM, (u + 1) * ATT_HEAD_DIM)
            o_ref[0, :, sl] = (norm_rope(y[:, sl], qg_ref[...]) * scale).astype(BF16)

    @pl.when(j == kv_tile)
    def _():
        for u in range(ATT_KV_HEADS):
            sl = slice(u * ATT_HEAD_DIM, (u + 1) * ATT_HEAD_DIM)
            o_ref[0, :, sl] = norm_rope(y[:, sl], kg_ref[...]).astype(BF16)
        o_ref[0, :, ATT_KV_WIDTH:] = y[:, ATT_KV_WIDTH:].astype(BF16)

    @pl.when(jnp.logical_and(j >= q_tiles, j != kv_tile))
    def _():
        o_ref[0] = y.astype(BF16)


def _inproj(x, mod6, w_bf, w_gate, layer, cos, sin_signed, q_gain, k_gain):
    b, s, d = x.shape
    tm, tn = INPROJ_TM, INPROJ_TN
    assert COL_AK % tn == 0 and ATT_WIDTH % tn == 0 and 2 * ATT_KV_WIDTH == tn
    grid_spec = pltpu.PrefetchScalarGridSpec(
        num_scalar_prefetch=1,
        grid=(b, s // tm, PROJ_WIDTH // tn),
        in_specs=[
            pl.BlockSpec((1, tm, d), lambda bi, i, j, perm: (bi, i, 0)),
            pl.BlockSpec((1, 1, d), lambda bi, i, j, perm: (bi * 6 + 1, 0, 0)),
            pl.BlockSpec((1, 1, d), lambda bi, i, j, perm: (bi * 6 + 0, 0, 0)),
            pl.BlockSpec((1, d, tn), lambda bi, i, j, perm: (layer, 0, perm[j])),
            pl.BlockSpec((d, N_GATES), lambda bi, i, j, perm: (0, 0)),
            pl.BlockSpec((tm, ATT_HEAD_DIM), lambda bi, i, j, perm: (i, 0)),
            pl.BlockSpec((tm, ATT_HEAD_DIM), lambda bi, i, j, perm: (i, 0)),
            pl.BlockSpec((1, ATT_HEAD_DIM), lambda bi, i, j, perm: (0, 0)),
            pl.BlockSpec((1, ATT_HEAD_DIM), lambda bi, i, j, perm: (0, 0)),
        ],
        out_specs=[
            pl.BlockSpec((1, tm, tn), lambda bi, i, j, perm: (bi, i, j)),
            pl.BlockSpec((1, tm, N_GATES), lambda bi, i, j, perm: (bi, i, 0)),
        ],
        scratch_shapes=[pltpu.VMEM((tm, d), BF16)],
    )
    return pl.pallas_call(
        _inproj_kernel,
        grid_spec=grid_spec,
        out_shape=[
            jax.ShapeDtypeStruct((b, s, PROJ_WIDTH), BF16),
            jax.ShapeDtypeStruct((b, s, N_GATES), F32),
        ],
        compiler_params=_params("arbitrary", "arbitrary", "arbitrary"),
        name="in_proj",
    )(jnp.asarray(IN_COL_BLOCKS, jnp.int32), x, mod6, mod6, w_bf, w_gate, cos, sin_signed, q_gain, k_gain)


def _attn_kernel(sink_ref, q_ref, kl_ref, kc_ref, kr_ref, vl_ref, vc_ref, vr_ref, o_ref, *, n_steps):
    n = pl.program_id(1)
    L = ATT_BLOCK
    rows = ATT_GROUP * L
    i = lax.broadcasted_iota(jnp.int32, (rows, 3 * L), 0) & (L - 1)
    jj = lax.broadcasted_iota(jnp.int32, (rows, 3 * L), 1)
    in_window = jnp.abs(jj - L - i) <= WINDOW
    first_mask = in_window & ((jj >= L) | (n > 0))
    last_mask = in_window & ((jj < 2 * L) | (n < n_steps - 1))
    rgrp = lax.broadcasted_iota(jnp.int32, (rows, 1), 0) // L
    for kv in range(ATT_KV_HEADS):
        hs = slice(kv * ATT_HEAD_DIM, (kv + 1) * ATT_HEAD_DIM)
        kband = jnp.concatenate([kl_ref[0, :, hs], kc_ref[0, :, hs], kr_ref[0, :, hs]], axis=0)
        vband = jnp.concatenate([vl_ref[0, :, hs], vc_ref[0, :, hs], vr_ref[0, :, hs]], axis=0)
        sink = jnp.zeros((rows, 1), F32)
        for g in range(ATT_GROUP):
            sink = jnp.where(rgrp == g, sink_ref[kv * ATT_GROUP + g], sink)
        for blk in range(ATT_STEP_BLOCKS):
            qrows = slice(blk * L, (blk + 1) * L)
            q = jnp.concatenate(
                [q_ref[0, qrows, (kv * ATT_GROUP + g) * ATT_HEAD_DIM:(kv * ATT_GROUP + g + 1) * ATT_HEAD_DIM]
                 for g in range(ATT_GROUP)], axis=0)
            kb = kband[blk * L:(blk + 3) * L]
            vb = vband[blk * L:(blk + 3) * L]
            valid = first_mask if blk == 0 else (last_mask if blk == ATT_STEP_BLOCKS - 1 else in_window)
            s = lax.dot_general(q, kb, (((1,), (1,)), ((), ())), preferred_element_type=F32)
            s = jnp.where(valid, s, NEG)
            m = jnp.maximum(jnp.max(s, axis=-1, keepdims=True), sink)
            p = jnp.exp(s - m)
            denom = jnp.sum(p, axis=-1, keepdims=True) + jnp.exp(sink - m)
            o = jnp.dot(p.astype(BF16), vb, preferred_element_type=F32) / denom
            for g in range(ATT_GROUP):
                h = kv * ATT_GROUP + g
                o_ref[0, qrows, h * ATT_HEAD_DIM:(h + 1) * ATT_HEAD_DIM] = o[g * L:(g + 1) * L].astype(BF16)


def _attention(proj, sink):
    b, s, _ = proj.shape
    L = ATT_BLOCK
    nb = s // L
    sb = ATT_STEP_BLOCKS
    assert sb >= 2 and nb % sb == 0
    kblk = COL_AK // ATT_KV_WIDTH
    vblk = COL_AV // ATT_KV_WIDTH
    left = lambda col: pl.BlockSpec((1, L, ATT_KV_WIDTH), lambda bi, n: (bi, jnp.maximum(n * sb - 1, 0), col))
    right = lambda col: pl.BlockSpec((1, L, ATT_KV_WIDTH), lambda bi, n: (bi, jnp.minimum((n + 1) * sb, nb - 1), col))
    centre = lambda col: pl.BlockSpec((1, sb * L, ATT_KV_WIDTH), lambda bi, n: (bi, n, col))
    return pl.pallas_call(
        functools.partial(_attn_kernel, n_steps=nb // sb),
        grid=(b, nb // sb),
        in_specs=[
            pl.BlockSpec(memory_space=pltpu.SMEM),
            pl.BlockSpec((1, sb * L, ATT_WIDTH), lambda bi, n: (bi, n, COL_AQ // ATT_WIDTH)),
            left(kblk), centre(kblk), right(kblk),
            left(vblk), centre(vblk), right(vblk),
        ],
        out_specs=pl.BlockSpec((1, sb * L, ATT_WIDTH), lambda bi, n: (bi, n, 0)),
        out_shape=jax.ShapeDtypeStruct((b, s, ATT_WIDTH), BF16),
        compiler_params=_params("arbitrary", "arbitrary"),
        name="window_attn",
    )(sink, proj, proj, proj, proj, proj, proj, proj)


def _split3(x):
    hi = x.astype(BF16)
    rest = x - hi.astype(F32)
    mid = rest.astype(BF16)
    return hi, mid, (rest - mid.astype(F32)).astype(BF16)


def _mlstm_kernel(q_ref, k_ref, v_ref, mo_ref, gcol_ref, grow_ref, bcol_ref, brow_ref, gain_ref, o_ref,
                  bc_scr, kt_scr, ar_scr, wr_scr, bt_scr, c_scr, h_scr, *, seq):
    L = MLSTM_BLOCK
    nc = seq // L
    ns = 2 * M_HEADS
    scale = M_QK_DIM ** -0.5
    row = lax.broadcasted_iota(jnp.int32, (L, L), 0)
    col = lax.broadcasted_iota(jnp.int32, (L, L), 1)
    lower = col <= row
    upper = col >= row
    tril = jnp.where(lower, 1.0, 0.0).astype(BF16)
    triu = jnp.where(upper, 1.0, 0.0).astype(BF16)

    fwd_lane = lax.broadcasted_iota(jnp.int32, (1, N_GATES), 1) < ns + M_HEADS
    for c in range(nc):
        rows = slice(c * L, (c + 1) * L)
        lf = jax.nn.log_sigmoid(gcol_ref[0, rows, :] + bcol_ref[...])
        lf3 = _split3(lf)
        pre = sum(jnp.dot(tril, part, preferred_element_type=F32) for part in lf3)
        suf = sum(jnp.dot(triu, part, preferred_element_type=F32) for part in lf3)
        bcol = jnp.where(fwd_lane, pre, suf)
        for k in range(ns):
            bc_scr[k, rows, :] = jnp.broadcast_to(bcol[:, ns + k:ns + k + 1], (L, LANES))
        kt_scr[:, rows] = k_ref[0, rows, :].astype(F32).T.astype(BF16)

    gr = grow_ref[0] + brow_ref[...]
    ig_r = gr[:ns].reshape(ns * nc, L)
    lf_r = jax.nn.log_sigmoid(gr[ns:]).reshape(ns * nc, L)
    lf_r3 = _split3(lf_r)
    pre_r = sum(jnp.dot(part, triu, preferred_element_type=F32) for part in lf_r3)
    suf_r = sum(jnp.dot(part, tril, preferred_element_type=F32) for part in lf_r3)
    fwd_rows = lax.broadcasted_iota(jnp.int32, (ns * nc, 1), 0) < M_HEADS * nc
    a_r = ig_r - jnp.where(fwd_rows, pre_r, suf_r)
    btot = jnp.sum(lf_r, axis=-1, keepdims=True)
    ar_scr[...] = a_r.reshape(ns, nc, L)
    wr_scr[...] = (btot + a_r).reshape(ns, nc, L)
    bt_scr[...] = jnp.broadcast_to(btot, (ns * nc, LANES)).reshape(ns, nc, LANES)

    c_scr[...] = jnp.zeros_like(c_scr)
    h_scr[...] = jnp.zeros_like(h_scr)
    ones_cols = jnp.ones((L, LANES), BF16)

    twice = lambda t: jnp.concatenate([t, t], axis=1)

    def chunk_step(k, c, m_st):
        d, h = divmod(k, M_HEADS)
        r0 = pl.multiple_of(c * L, L)
        q = q_ref[0, pl.ds(r0, L), h * M_QK_DIM:(h + 1) * M_QK_DIM]
        kk = k_ref[0, pl.ds(r0, L), h * M_QK_DIM:(h + 1) * M_QK_DIM]
        v_ext = jnp.concatenate([v_ref[0, pl.ds(r0, L), h * M_V_DIM:(h + 1) * M_V_DIM], ones_cols], axis=1)
        bc = bc_scr[k, pl.ds(r0, L), :]
        ar = ar_scr[k, pl.ds(c, 1), :]
        wr = wr_scr[k, pl.ds(c, 1), :]
        bt = bt_scr[k, pl.ds(c, 1), :]
        dm = jnp.where(lower if d == 0 else upper, twice(bc) + ar, NEG)
        g_inter = bc + m_st
        m_t = jnp.maximum(jnp.broadcast_to(jnp.max(dm, axis=-1, keepdims=True), (L, LANES)), g_inter)
        e_inter = jnp.exp(g_inter - m_t) * scale
        s_qk = lax.dot_general(q, kk, (((1,), (1,)), ((), ())), preferred_element_type=F32)
        p = s_qk * scale * jnp.exp(dm - twice(m_t))
        qc = jnp.dot(q, c_scr[k].astype(BF16), preferred_element_type=F32)
        pv = jnp.dot(p.astype(BF16), v_ext, preferred_element_type=F32)
        num = pv[:, :M_V_DIM] + twice(e_inter) * qc[:, :M_V_DIM]
        den = pv[:, M_V_DIM:] + e_inter * qc[:, M_V_DIM:]
        inv = 1.0 / jnp.maximum(jnp.abs(den), jnp.exp(-m_t))
        hsl = (pl.ds(r0, L), slice(h * M_V_DIM, (h + 1) * M_V_DIM))
        h_scr[hsl] = h_scr[hsl] + num * twice(inv)
        m_new = jnp.maximum(bt + m_st, jnp.broadcast_to(jnp.max(wr, axis=-1, keepdims=True), (1, LANES)))
        a = jnp.exp(bt + m_st - m_new)
        kt = kt_scr[h * M_QK_DIM:(h + 1) * M_QK_DIM, pl.ds(r0, L)]
        ek_t = (kt.astype(F32) * jnp.exp(wr - twice(m_new))).astype(BF16)
        c_scr[k] = jnp.concatenate([a, a, a], axis=1) * c_scr[k] + jnp.dot(ek_t, v_ext, preferred_element_type=F32)
        return m_new

    def body(it, ms):
        out = []
        for k in range(ns):
            c = it if k < M_HEADS else nc - 1 - it
            out.append(chunk_step(k, c, ms[k]))
        return tuple(out)

    lax.fori_loop(0, nc, body, tuple(jnp.zeros((1, LANES), F32) for _ in range(ns)))

    def fin(i, carry):
        r0 = pl.multiple_of(i * L, L)
        for h in range(M_HEADS):
            sl = slice(h * M_V_DIM, (h + 1) * M_V_DIM)
            x = h_scr[pl.ds(r0, L), sl]
            y = x * lax.rsqrt(jnp.mean(x * x, axis=-1, keepdims=True) + EPS) * gain_ref[:, sl]
            o_ref[0, pl.ds(r0, L), sl] = (y * jax.nn.sigmoid(mo_ref[0, pl.ds(r0, L), sl].astype(F32))).astype(BF16)
        return carry

    lax.fori_loop(0, nc, fin, 0)


def _mlstm(proj, gates, b_gates, m_gain):
    b, s, _ = proj.shape
    L = MLSTM_BLOCK
    nc = s // L
    ns = 2 * M_HEADS
    grow = jnp.transpose(gates, (0, 2, 1)).reshape(b, N_GATES, nc, L)
    col = lambda width, off: pl.BlockSpec((1, s, width), lambda bi: (bi, 0, off // width))
    return pl.pallas_call(
        functools.partial(_mlstm_kernel, seq=s),
        grid=(b,),
        in_specs=[
            col(M_QK_WIDTH, COL_MQ), col(M_QK_WIDTH, COL_MK), col(M_WIDTH, COL_MV),
            pl.BlockSpec((1, s, M_WIDTH), lambda bi: (bi, 0, COL_MO // M_WIDTH), pipeline_mode=pl.Buffered(1)),
            pl.BlockSpec((1, s, N_GATES), lambda bi: (bi, 0, 0)),
            pl.BlockSpec((1, N_GATES, nc, L), lambda bi: (bi, 0, 0, 0)),
            pl.BlockSpec((1, N_GATES), lambda bi: (0, 0)),
            pl.BlockSpec((N_GATES, 1, 1), lambda bi: (0, 0, 0)),
            pl.BlockSpec((1, M_WIDTH), lambda bi: (0, 0)),
        ],
        out_specs=pl.BlockSpec((1, s, M_WIDTH), lambda bi: (bi, 0, 0)),
        out_shape=jax.ShapeDtypeStruct((b, s, M_WIDTH), BF16),
        scratch_shapes=[
            pltpu.VMEM((ns, s, LANES), F32),
            pltpu.VMEM((M_QK_WIDTH, s), BF16),
            pltpu.VMEM((ns, nc, L), F32), pltpu.VMEM((ns, nc, L), F32), pltpu.VMEM((ns, nc, LANES), F32),
            pltpu.VMEM((ns, M_QK_DIM, M_V_DIM + LANES), F32),
            pltpu.VMEM((s, M_WIDTH), F32),
        ],
        compiler_params=_params("arbitrary"),
        name="mlstm",
    )(proj, proj, proj, proj, gates, grow, b_gates.reshape(1, N_GATES), b_gates.reshape(N_GATES, 1, 1),
      m_gain.reshape(1, M_WIDTH))


def _outproj_kernel(att_ref, mo_ref, wa_ref, wm_ref, x_ref, g1_ref, sc_ref, sh_ref, wr_ref,
                    x1_ref, h2_ref, aff_ref):
    j = pl.program_id(2)
    nj = pl.num_programs(2)
    tn = OUTPROJ_TN
    mix = (jnp.dot(att_ref[0], wa_ref[...], preferred_element_type=F32)
           + jnp.dot(mo_ref[0], wm_ref[...], preferred_element_type=F32))
    x1t = x_ref[0] + g1_ref[0] * mix
    for jj in range(D_MODEL // tn):
        @pl.when(j == jj)
        def _(jj=jj):
            x1_ref[0, :, jj * tn:(jj + 1) * tn] = x1t

    @pl.when(j == nj - 1)
    def _():
        xf = x1_ref[0]
        r = lax.rsqrt(jnp.mean(xf * xf, axis=-1, keepdims=True) + EPS)
        h2 = (xf * r) * (1.0 + sc_ref[0]) + sh_ref[0]
        h2_hi = h2.astype(BF16)
        h2_ref[0] = h2
        terms = jnp.dot(h2_hi, wr_ref[...], preferred_element_type=F32)
        logits = terms[:, :N_EXPERTS] + terms[:, N_EXPERTS:]
        e = jnp.exp(logits - jnp.max(logits, axis=-1, keepdims=True))
        aff_ref[0] = e / jnp.sum(e, axis=-1, keepdims=True)


def _outproj(att, m_out, w_out, x, mod6, w_router):
    b, s, d = x.shape
    tm, tn = OUTPROJ_TM, OUTPROJ_TN
    half = ATT_WIDTH
    assert w_out.shape[0] == 2 * half and M_WIDTH == half
    wr_hi = w_router.astype(BF16)
    wr_lo = (w_router - wr_hi.astype(F32)).astype(BF16)
    return pl.pallas_call(
        _outproj_kernel,
        grid=(b, s // tm, d // tn),
        in_specs=[
            pl.BlockSpec((1, tm, ATT_WIDTH), lambda bi, i, j: (bi, i, 0)),
            pl.BlockSpec((1, tm, M_WIDTH), lambda bi, i, j: (bi, i, 0)),
            pl.BlockSpec((half, tn), lambda bi, i, j: (0, j)),
            pl.BlockSpec((half, tn), lambda bi, i, j: (1, j)),
            pl.BlockSpec((1, tm, tn), lambda bi, i, j: (bi, i, j)),
            pl.BlockSpec((1, 1, tn), lambda bi, i, j: (bi * 6 + 2, 0, j)),
            pl.BlockSpec((1, 1, d), lambda bi, i, j: (bi * 6 + 4, 0, 0)),
            pl.BlockSpec((1, 1, d), lambda bi, i, j: (bi * 6 + 3, 0, 0)),
            pl.BlockSpec((d, 2 * N_EXPERTS), lambda bi, i, j: (0, 0)),
        ],
        out_specs=[
            pl.BlockSpec((1, tm, d), lambda bi, i, j: (bi, i, 0)),
            pl.BlockSpec((1, tm, d), lambda bi, i, j: (bi, i, 0)),
            pl.BlockSpec((1, tm, N_EXPERTS), lambda bi, i, j: (bi, i, 0)),
        ],
        out_shape=[
            jax.ShapeDtypeStruct((b, s, d), F32),
            jax.ShapeDtypeStruct((b, s, d), F32),
            jax.ShapeDtypeStruct((b, s, N_EXPERTS), F32),
        ],
        compiler_params=_params("arbitrary", "arbitrary", "arbitrary"),
        name="out_proj",
    )(att, m_out, w_out, w_out, x, mod6, mod6, mod6, jnp.concatenate([wr_hi, wr_lo], axis=1))


def _prefix_sum_lanes(x, n):
    lane = lax.broadcasted_iota(jnp.int32, x.shape, 1)
    sh = 1
    while sh < n:
        x = x + jnp.where(lane >= sh, pltpu.roll(x, sh, 1), 0.0)
        sh *= 2
    return x


def _select_kernel(aff_ref, pos_ref, gate_ref, *, cap, seq):
    a = aff_ref[...]
    ne = a.shape[0]

    def body(_, lohi):
        lo, hi = lohi
        mid = jnp.where(lo > 0.0, 0.5 * (lo + hi), hi * (1.0 / 256.0))
        cnt = jnp.sum(jnp.where(a >= mid, 1.0, 0.0), axis=-1, keepdims=True)
        ok = cnt >= cap
        return jnp.where(ok, mid, lo), jnp.where(ok, hi, mid)

    lo0 = jnp.zeros((ne, 1), F32)
    hi0 = jnp.maximum(2.0 * jnp.max(a, axis=-1, keepdims=True), SELECT_MIN_UPPER)
    lo, hi = lax.fori_loop(0, SELECT_BISECTIONS, body, (lo0, hi0))
    above = a >= hi
    band = jnp.logical_and(a >= lo, jnp.logical_not(above))
    n_above = jnp.sum(jnp.where(above, 1.0, 0.0), axis=-1, keepdims=True)
    band_rank = _prefix_sum_lanes(jnp.where(band, 1.0, 0.0), seq)
    sel = jnp.where(above, 1.0, jnp.where(band, jnp.where(band_rank <= cap - n_above, 1.0, 0.0), 0.0))
    pos = _prefix_sum_lanes(sel, seq) - 1.0
    chosen = sel > 0.5
    pos_ref[...] = jnp.where(chosen, pos, -1.0)
    gate_ref[...] = jnp.where(chosen, a, 0.0)


def _select(aff_rows, cap):
    n, s = aff_rows.shape
    spec = pl.BlockSpec((n, s), lambda i: (0, 0))
    return pl.pallas_call(
        functools.partial(_select_kernel, cap=cap, seq=s),
        grid=(1,),
        in_specs=[spec],
        out_specs=[spec, spec],
        out_shape=[jax.ShapeDtypeStruct((n, s), F32), jax.ShapeDtypeStruct((n, s), F32)],
        compiler_params=_params("arbitrary"),
        name="expert_select",
    )(aff_rows)


def _lane_spread(ne):
    blk = lax.broadcasted_iota(jnp.int32, (ne, ne * LANES), 1) // LANES
    return jnp.where(blk == lax.broadcasted_iota(jnp.int32, (ne, ne * LANES), 0), 1.0, 0.0).astype(BF16)


def _compact_kernel(pos_ref, idx_ref, *, cap):
    seq, ne = pos_ref.shape[1], pos_ref.shape[2]
    rt = COMBINE_BUILD_ROWS
    halves = cap // LANES
    spread = _lane_spread(ne)
    lane = lax.broadcasted_iota(jnp.int32, (rt, LANES), 1).astype(F32)
    tok0 = lax.broadcasted_iota(jnp.int32, (rt, LANES), 0).astype(F32)

    def body(i, accs):
        r0 = pl.multiple_of(i * rt, rt)
        pos = jnp.dot(pos_ref[0, pl.ds(r0, rt), :].astype(BF16), spread, preferred_element_type=F32)
        tok = tok0 + r0.astype(F32)
        out = []
        for e in range(ne):
            p = pos[:, e * LANES:(e + 1) * LANES]
            for h in range(halves):
                hit = lane + float(h * LANES) == p
                out.append(accs[e * halves + h] + jnp.sum(jnp.where(hit, tok, 0.0), axis=0, keepdims=True))
        return tuple(out)

    accs = lax.fori_loop(0, seq // rt, body, tuple(jnp.zeros((1, LANES), F32) for _ in range(ne * halves)))
    for e in range(ne):
        for h in range(halves):
            idx_ref[0, e:e + 1, h * LANES:(h + 1) * LANES] = accs[e * halves + h].astype(jnp.int32)


def _compact(pos_cols, cap):
    b, s, ne = pos_cols.shape
    return pl.pallas_call(
        functools.partial(_compact_kernel, cap=cap),
        grid=(b,),
        in_specs=[pl.BlockSpec((1, s, ne), lambda bi: (bi, 0, 0))],
        out_specs=pl.BlockSpec((1, ne, cap), lambda bi: (bi, 0, 0)),
        out_shape=jax.ShapeDtypeStruct((b, ne, cap), jnp.int32),
        compiler_params=_params("arbitrary"),
        name="expert_compact",
    )(pos_cols)


def _ffn_kernel(idx_ref, h2_hbm, wg_ref, wu_ref, wd_ref, ye_ref, x_scr, hid_scr, sem, *, nb, cap):
    e = pl.program_id(0)
    f = pl.program_id(1)
    ne = pl.num_programs(0)
    nf = pl.num_programs(1)
    rows = nb * cap
    per_step = rows // FFN_UP_STEPS
    tf = wg_ref.shape[-1]
    slot = lax.rem(e, 2)
    nslot = 1 - slot
    nxt = jnp.minimum(e + 1, ne - 1)

    def row_copy(expert, r, b, sl):
        tok = idx_ref[expert, r]
        return pltpu.make_async_copy(h2_hbm.at[b, pl.ds(tok, 1), :], x_scr.at[sl, pl.ds(r, 1), :], sem.at[sl])

    def wait_rows(sl):
        pltpu.make_async_copy(x_scr.at[sl], x_scr.at[sl], sem.at[sl]).wait()

    @pl.when(jnp.logical_and(e == 0, f == 0))
    def _():
        def first(r, carry):
            row_copy(0, r, r // cap, 0).start()
            return carry
        lax.fori_loop(0, rows, first, 0)

    for ff in range(FFN_UP_STEPS):
        for cur in range(2):
            @pl.when(jnp.logical_and(f == ff, slot == cur))
            def _(ff=ff, cur=cur):
                if ff == 0:
                    wait_rows(cur)
                for r in range(ff * per_step, (ff + 1) * per_step):
                    row_copy(nxt, r, r // cap, 1 - cur).start(priority=r % 2)
                x = x_scr[cur].astype(BF16)
                g = jnp.dot(x, wg_ref[0, 0].astype(BF16), preferred_element_type=F32)
                u = jnp.dot(x, wu_ref[0, 0].astype(BF16), preferred_element_type=F32)
                hid_scr[:, ff * tf:(ff + 1) * tf] = ((g * jax.nn.sigmoid(g)) * u).astype(BF16)

    @pl.when(f >= FFN_UP_STEPS)
    def _():
        y = jnp.dot(hid_scr[...], wd_ref[0, 0].astype(BF16), preferred_element_type=F32)
        ye_ref[...] = y.reshape(nb, 1, cap, y.shape[-1]).astype(BF16)

    @pl.when(jnp.logical_and(e == ne - 1, f == nf - 1))
    def _():
        wait_rows(nslot)


def _ffn(idx, h2, w_gate, w_up, w_down, layer):
    b, s, d = h2.shape
    ne = idx.shape[0]
    cap = idx.shape[1] // b
    ff = w_gate.shape[-1]
    tf = ff // FFN_UP_STEPS
    dn = d // FFN_DOWN_STEPS
    up = lambda f: jnp.minimum(f, FFN_UP_STEPS - 1)
    down = lambda f: jnp.maximum(f - FFN_UP_STEPS, 0)
    grid_spec = pltpu.PrefetchScalarGridSpec(
        num_scalar_prefetch=1,
        grid=(ne, FFN_UP_STEPS + FFN_DOWN_STEPS),
        in_specs=[
            pl.BlockSpec(memory_space=pl.ANY),
            pl.BlockSpec((1, 1, d, tf), lambda e, f, idx_ref: (layer, e, 0, up(f))),
            pl.BlockSpec((1, 1, d, tf), lambda e, f, idx_ref: (layer, e, 0, up(f))),
            pl.BlockSpec((1, 1, ff, dn), lambda e, f, idx_ref: (layer, e, 0, down(f))),
        ],
        out_specs=pl.BlockSpec((b, 1, cap, dn), lambda e, f, idx_ref: (0, e, 0, down(f))),
        scratch_shapes=[
            pltpu.VMEM((2, b * cap, d), F32),
            pltpu.VMEM((b * cap, ff), BF16),
            pltpu.SemaphoreType.DMA((2,)),
        ],
    )
    return pl.pallas_call(
        functools.partial(_ffn_kernel, nb=b, cap=cap),
        grid_spec=grid_spec,
        out_shape=jax.ShapeDtypeStruct((b, ne, cap, d), BF16),
        compiler_params=_params("arbitrary", "arbitrary"),
        name="expert_ffn",
    )(idx, h2, w_gate, w_up, w_down)


def _combine_kernel(pos_ref, gate_ref, ye_ref, x1_ref, g2_ref, o_ref, scat_scr, *, cap):
    j = pl.program_id(1)
    seq, ne = pos_ref.shape[1], pos_ref.shape[2]
    rt = COMBINE_BUILD_ROWS

    @pl.when(j == 0)
    def _():
        spread = _lane_spread(ne)
        lane = lax.broadcasted_iota(jnp.int32, (rt, LANES), 1).astype(F32)

        def build(i, carry):
            r0 = pl.multiple_of(i * rt, rt)
            pos = jnp.dot(pos_ref[0, pl.ds(r0, rt), :].astype(BF16), spread, preferred_element_type=F32)
            gate = jnp.dot(gate_ref[0, pl.ds(r0, rt), :].astype(BF16), spread, preferred_element_type=F32)
            for e in range(ne):
                p = pos[:, e * LANES:(e + 1) * LANES]
                g = gate[:, e * LANES:(e + 1) * LANES]
                for h in range(cap // LANES):
                    cols = slice(e * cap + h * LANES, e * cap + (h + 1) * LANES)
                    scat_scr[pl.ds(r0, rt), cols] = jnp.where(lane + float(h * LANES) == p, g, 0.0).astype(BF16)
            return carry

        lax.fori_loop(0, seq // rt, build, 0)

    acc = jnp.dot(scat_scr[...], ye_ref[0], preferred_element_type=F32)
    o_ref[0] = x1_ref[0] + g2_ref[0] * acc


def _combine(pos_cols, gate_cols, ye, x1, mod6):
    b, s, d = x1.shape
    ne, cap = ye.shape[1], ye.shape[2]
    tn = COMBINE_TN
    return pl.pallas_call(
        functools.partial(_combine_kernel, cap=cap),
        grid=(b, d // tn),
        in_specs=[
            pl.BlockSpec((1, s, ne), lambda bi, j: (bi, 0, 0)),
            pl.BlockSpec((1, s, ne), lambda bi, j: (bi, 0, 0)),
            pl.BlockSpec((1, ne * cap, tn), lambda bi, j: (bi, 0, j)),
            pl.BlockSpec((1, s, tn), lambda bi, j: (bi, 0, j)),
            pl.BlockSpec((1, 1, tn), lambda bi, j: (bi * 6 + 5, 0, j)),
        ],
        out_specs=pl.BlockSpec((1, s, tn), lambda bi, j: (bi, 0, j)),
        out_shape=jax.ShapeDtypeStruct((b, s, d), F32),
        scratch_shapes=[pltpu.VMEM((s, ne * cap), BF16)],
        compiler_params=_params("arbitrary", "arbitrary"),
        name="expert_combine",
    )(pos_cols, gate_cols, ye.reshape(b, ne * cap, d), x1, mod6)


def _rope_tables(seq):
    inv = 1.0 / (ROPE_THETA ** (jnp.arange(0, ATT_HEAD_DIM, 2, dtype=F32) / ATT_HEAD_DIM))
    ang = jnp.arange(seq, dtype=F32)[:, None] * inv[None, :]
    ang = jnp.concatenate([ang, ang], axis=-1)
    sign = jnp.where(jnp.arange(ATT_HEAD_DIM) < ATT_HEAD_DIM // 2, -1.0, 1.0).astype(F32)
    return jnp.cos(ang), jnp.sin(ang) * sign


def kernel(x, c, w_ada, b_ada, w_in, b_gates, q_gain, k_gain, sink, m_gain, w_out,
           w_router, w_gate, w_up, w_down):
    b, s, d = x.shape
    depth = w_ada.shape[0]
    cap = CAPACITY_FACTOR * s // N_EXPERTS
    cos, sin_signed = _rope_tables(s)
    c_pad = jnp.pad(c, ((0, SUBLANES - b), (0, 0)))
    mod = _ada(c_pad, w_ada, b_ada)
    w_in_bf = w_in.astype(BF16)
    for l in range(depth):
        mod6 = mod[l, :b].reshape(b * 6, 1, d)
        proj, gates = _inproj(x, mod6, w_in_bf, w_in[l, :, PROJ_WIDTH:].astype(BF16), l, cos, sin_signed,
                              q_gain[l].reshape(1, -1), k_gain[l].reshape(1, -1))
        att = _attention(proj, sink[l])
        m_out = _mlstm(proj, gates, b_gates[l].reshape(-1), m_gain[l])
        x1, h2, aff = _outproj(att, m_out, w_out[l].astype(BF16), x, mod6, w_router[l])
        pos_r, gate_r = _select(jnp.transpose(aff, (0, 2, 1)).reshape(b * N_EXPERTS, s), cap)
        to_cols = lambda t: jnp.transpose(t.reshape(b, N_EXPERTS, s), (0, 2, 1))
        pos_c = to_cols(pos_r)
        idx = jnp.transpose(_compact(pos_c, cap), (1, 0, 2)).reshape(N_EXPERTS, b * cap)
        ye = _ffn(idx, h2, w_gate, w_up, w_down, l)
        x = _combine(pos_c, to_cols(gate_r), ye, x1, mod6)
    return x
```

```python
import functools

import jax
import jax.numpy as jnp
from jax import lax
from jax.experimental import pallas as pl
from jax.experimental.pallas import tpu as pltpu

F32 = jnp.float32
BF16 = jnp.bfloat16
SUBLANES = 8
LANES = 128

D_MODEL = 2048
ATT_HEAD_DIM = 128
ATT_HEADS = 8
ATT_KV_HEADS = 2
ATT_GROUP = ATT_HEADS // ATT_KV_HEADS
ATT_WIDTH = ATT_HEADS * ATT_HEAD_DIM
ATT_KV_WIDTH = ATT_KV_HEADS * ATT_HEAD_DIM
WINDOW = 128
ATT_BLOCK = 128
ATT_STEP_BLOCKS = 8
ROPE_THETA = 10000.0
M_HEADS = 4
M_V_DIM = 256
M_QK_DIM = 128
M_WIDTH = M_HEADS * M_V_DIM
M_QK_WIDTH = M_HEADS * M_QK_DIM
MLSTM_BLOCK = 256
N_GATES = 4 * M_HEADS
N_EXPERTS = 16
EXPERT_FF = D_MODEL // 2
CAPACITY_FACTOR = 2
EPS = 1e-6
NEG = -1e30

COL_AQ = 0
COL_MV = ATT_WIDTH
COL_MO = COL_MV + M_WIDTH
COL_AK = COL_MO + M_WIDTH
COL_AV = COL_AK + ATT_KV_WIDTH
COL_MQ = COL_AV + ATT_KV_WIDTH
COL_MK = COL_MQ + M_QK_WIDTH
PROJ_WIDTH = COL_MK + M_QK_WIDTH
IN_COL_BLOCKS = (0, 1, 5, 6, 7, 8, 2, 3, 4)

VMEM_LIMIT_BYTES = 56 * 1024 * 1024

ADA_TN = 1024
ADA_K_SPLITS = 4
INPROJ_TM = 1024
INPROJ_TN = 512
OUTPROJ_TM = 1024
OUTPROJ_TN = 512
FFN_UP_STEPS = 4
FFN_DOWN_STEPS = 2
COMBINE_TN = 512
COMBINE_BUILD_ROWS = 256
SELECT_BISECTIONS = 56
SELECT_MIN_UPPER = 1e-30


def _params(*sem):
    return pltpu.CompilerParams(dimension_semantics=sem, vmem_limit_bytes=VMEM_LIMIT_BYTES)


def _ada_kernel(c_ref, *refs):
    w_refs, b_ref, o_ref = refs[:ADA_K_SPLITS], refs[ADA_K_SPLITS], refs[ADA_K_SPLITS + 1]
    c = c_ref[...]
    c_act = c * jax.nn.sigmoid(c)
    c_hi = c_act.astype(BF16)
    c_lo = (c_act - c_hi.astype(F32)).astype(BF16)
    kb = c.shape[1] // ADA_K_SPLITS
    acc = b_ref[0]
    for s, w_ref in enumerate(w_refs):
        w = w_ref[0]
        w_hi = w.astype(BF16)
        w_lo = (w - w_hi.astype(F32)).astype(BF16)
        ks = slice(s * kb, (s + 1) * kb)
        acc = acc + jnp.dot(c_hi[:, ks], w_hi, preferred_element_type=F32)
        acc = acc + jnp.dot(c_lo[:, ks], w_hi, preferred_element_type=F32)
        acc = acc + jnp.dot(c_hi[:, ks], w_lo, preferred_element_type=F32)
    o_ref[0] = acc


def _ada(c_pad, w_ada, b_ada):
    depth, d, n = w_ada.shape
    rows = c_pad.shape[0]
    return pl.pallas_call(
        _ada_kernel,
        grid=(depth, n // ADA_TN),
        in_specs=(
            [pl.BlockSpec((rows, d), lambda l, j: (0, 0))]
            + [pl.BlockSpec((1, d // ADA_K_SPLITS, ADA_TN), functools.partial(lambda l, j, s: (l, s, j), s=s))
               for s in range(ADA_K_SPLITS)]
            + [pl.BlockSpec((1, 1, ADA_TN), lambda l, j: (l, 0, j))]),
        out_specs=pl.BlockSpec((1, rows, ADA_TN), lambda l, j: (l, 0, j)),
        out_shape=jax.ShapeDtypeStruct((depth, rows, n), F32),
        compiler_params=_params("arbitrary", "arbitrary"),
        name="ada_mod",
    )(c_pad, *([w_ada] * ADA_K_SPLITS), b_ada.reshape(depth, 1, n))


def _inproj_kernel(perm_ref, x_ref, sc_ref, sh_ref, w_ref, wg_ref, cos_ref, sin_ref, qg_ref, kg_ref,
                   o_ref, g_ref, h_scr):
    del perm_ref
    j = pl.program_id(2)

    @pl.when(j == 0)
    def _():
        x = x_ref[0]
        r = lax.rsqrt(jnp.mean(x * x, axis=-1, keepdims=True) + EPS)
        h = (x * r) * (1.0 + sc_ref[0]) + sh_ref[0]
        hb = h.astype(BF16)
        h_scr[...] = hb
        g_ref[0] = jnp.dot(hb, wg_ref[...], preferred_element_type=F32)

    y = jnp.dot(h_scr[...], w_ref[0], preferred_element_type=F32)

    head_ones = jnp.ones((ATT_HEAD_DIM, ATT_HEAD_DIM), BF16)

    def norm_rope(t, gain):
        ss = jnp.dot((t * t).astype(BF16), head_ones, preferred_element_type=F32)
        tn = t * lax.rsqrt(ss * (1.0 / ATT_HEAD_DIM) + EPS) * gain
        return tn * cos_ref[...] + pltpu.roll(tn, ATT_HEAD_DIM // 2, 1) * sin_ref[...]

    heads_per_tile = INPROJ_TN // ATT_HEAD_DIM
    q_tiles = ATT_WIDTH // INPROJ_TN
    kv_tile = COL_AK // INPROJ_TN
    scale = ATT_HEAD_DIM ** -0.5

    @pl.when(j < q_tiles)
    def _():
        for u in range(heads_per_tile):
            sl = slice(u * ATT_HEAD_DIM, (u + 1) * ATT_HEAD_DIM)
            o_ref[0, :, sl] = (norm_rope(y[:, sl], qg_ref[...]) * scale).astype(BF16)

    @pl.when(j == kv_tile)
    def _():
        for u in range(ATT_KV_HEADS):
            sl = slice(u * ATT_HEAD_DIM, (u + 1) * ATT_HEAD_DIM)
            o_ref[0, :, sl] = norm_rope(y[:, sl], kg_ref[...]).astype(BF16)
        o_ref[0, :, ATT_KV_WIDTH:] = y[:, ATT_KV_WIDTH:].astype(BF16)

    @pl.when(jnp.logical_and(j >= q_tiles, j != kv_tile))
    def _():
        o_ref[0] = y.astype(BF16)


def _inproj(x, mod6, w_bf, w_gate, layer, cos, sin_signed, q_gain, k_gain):
    b, s, d = x.shape
    tm, tn = INPROJ_TM, INPROJ_TN
    assert COL_AK % tn == 0 and ATT_WIDTH % tn == 0 and 2 * ATT_KV_WIDTH == tn
    grid_spec = pltpu.PrefetchScalarGridSpec(
        num_scalar_prefetch=1,
        grid=(b, s // tm, PROJ_WIDTH // tn),
        in_specs=[
            pl.BlockSpec((1, tm, d), lambda bi, i, j, perm: (bi, i, 0)),
            pl.BlockSpec((1, 1, d), lambda bi, i, j, perm: (bi * 6 + 1, 0, 0)),
            pl.BlockSpec((1, 1, d), lambda bi, i, j, perm: (bi * 6 + 0, 0, 0)),
            pl.BlockSpec((1, d, tn), lambda bi, i, j, perm: (layer, 0, perm[j])),
            pl.BlockSpec((d, N_GATES), lambda bi, i, j, perm: (0, 0)),
            pl.BlockSpec((tm, ATT_HEAD_DIM), lambda bi, i, j, perm: (i, 0)),
            pl.BlockSpec((tm, ATT_HEAD_DIM), lambda bi, i, j, perm: (i, 0)),
            pl.BlockSpec((1, ATT_HEAD_DIM), lambda bi, i, j, perm: (0, 0)),
            pl.BlockSpec((1, ATT_HEAD_DIM), lambda bi, i, j, perm: (0, 0)),
        ],
        out_specs=[
            pl.BlockSpec((1, tm, tn), lambda bi, i, j, perm: (bi, i, j)),
            pl.BlockSpec((1, tm, N_GATES), lambda bi, i, j, perm: (bi, i, 0)),
        ],
        scratch_shapes=[pltpu.VMEM((tm, d), BF16)],
    )
    return pl.pallas_call(
        _inproj_kernel,
        grid_spec=grid_spec,
        out_shape=[
            jax.ShapeDtypeStruct((b, s, PROJ_WIDTH), BF16),
            jax.ShapeDtypeStruct((b, s, N_GATES), F32),
        ],
        compiler_params=_params("arbitrary", "arbitrary", "arbitrary"),
        name="in_proj",
    )(jnp.asarray(IN_COL_BLOCKS, jnp.int32), x, mod6, mod6, w_bf, w_gate, cos, sin_signed, q_gain, k_gain)


def _attn_kernel(sink_ref, q_ref, kl_ref, kc_ref, kr_ref, vl_ref, vc_ref, vr_ref, o_ref, *, n_steps):
    n = pl.program_id(1)
    L = ATT_BLOCK
    rows = ATT_GROUP * L
    i = lax.broadcasted_iota(jnp.int32, (rows, 3 * L), 0) & (L - 1)
    jj = lax.broadcasted_iota(jnp.int32, (rows, 3 * L), 1)
    in_window = jnp.abs(jj - L - i) <= WINDOW
    first_mask = in_window & ((jj >= L) | (n > 0))
    last_mask = in_window & ((jj < 2 * L) | (n < n_steps - 1))
    rgrp = lax.broadcasted_iota(jnp.int32, (rows, 1), 0) // L
    for kv in range(ATT_KV_HEADS):
        hs = slice(kv * ATT_HEAD_DIM, (kv + 1) * ATT_HEAD_DIM)
        kband = jnp.concatenate([kl_ref[0, :, hs], kc_ref[0, :, hs], kr_ref[0, :, hs]], axis=0)
        vband = jnp.concatenate([vl_ref[0, :, hs], vc_ref[0, :, hs], vr_ref[0, :, hs]], axis=0)
        sink = jnp.zeros((rows, 1), F32)
        for g in range(ATT_GROUP):
            sink = jnp.where(rgrp == g, sink_ref[kv * ATT_GROUP + g], sink)
        for blk in range(ATT_STEP_BLOCKS):
            qrows = slice(blk * L, (blk + 1) * L)
            q = jnp.concatenate(
                [q_ref[0, qrows, (kv * ATT_GROUP + g) * ATT_HEAD_DIM:(kv * ATT_GROUP + g + 1) * ATT_HEAD_DIM]
                 for g in range(ATT_GROUP)], axis=0)
            kb = kband[blk * L:(blk + 3) * L]
            vb = vband[blk * L:(blk + 3) * L]
            valid = first_mask if blk == 0 else (last_mask if blk == ATT_STEP_BLOCKS - 1 else in_window)
            s = lax.dot_general(q, kb, (((1,), (1,)), ((), ())), preferred_element_type=F32)
            s = jnp.where(valid, s, NEG)
            m = jnp.maximum(jnp.max(s, axis=-1, keepdims=True), sink)
            p = jnp.exp(s - m)
            denom = jnp.sum(p, axis=-1, keepdims=True) + jnp.exp(sink - m)
            o = jnp.dot(p.astype(BF16), vb, preferred_element_type=F32) / denom
            for g in range(ATT_GROUP):
                h = kv * ATT_GROUP + g
                o_ref[0, qrows, h * ATT_HEAD_DIM:(h + 1) * ATT_HEAD_DIM] = o[g * L:(g + 1) * L].astype(BF16)


def _attention(proj, sink):
    b, s, _ = proj.shape
    L = ATT_BLOCK
    nb = s // L
    sb = ATT_STEP_BLOCKS
    assert sb >= 2 and nb % sb == 0
    kblk = COL_AK // ATT_KV_WIDTH
    vblk = COL_AV // ATT_KV_WIDTH
    left = lambda col: pl.BlockSpec((1, L, ATT_KV_WIDTH), lambda bi, n: (bi, jnp.maximum(n * sb - 1, 0), col))
    right = lambda col: pl.BlockSpec((1, L, ATT_KV_WIDTH), lambda bi, n: (bi, jnp.minimum((n + 1) * sb, nb - 1), col))
    centre = lambda col: pl.BlockSpec((1, sb * L, ATT_KV_WIDTH), lambda bi, n: (bi, n, col))
    return pl.pallas_call(
        functools.partial(_attn_kernel, n_steps=nb // sb),
        grid=(b, nb // sb),
        in_specs=[
            pl.BlockSpec(memory_space=pltpu.SMEM),
            pl.BlockSpec((1, sb * L, ATT_WIDTH), lambda bi, n: (bi, n, COL_AQ // ATT_WIDTH)),
            left(kblk), centre(kblk), right(kblk),
            left(vblk), centre(vblk), right(vblk),
        ],
        out_specs=pl.BlockSpec((1, sb * L, ATT_WIDTH), lambda bi, n: (bi, n, 0)),
        out_shape=jax.ShapeDtypeStruct((b, s, ATT_WIDTH), BF16),
        compiler_params=_params("arbitrary", "arbitrary"),
        name="window_attn",
    )(sink, proj, proj, proj, proj, proj, proj, proj)


def _split3(x):
    hi = x.astype(BF16)
    rest = x - hi.astype(F32)
    mid = rest.astype(BF16)
    return hi, mid, (rest - mid.astype(F32)).astype(BF16)


def _mlstm_kernel(q_ref, k_ref, v_ref, mo_ref, gcol_ref, grow_ref, bcol_ref, brow_ref, gain_ref, o_ref,
                  bc_scr, kt_scr, ar_scr, wr_scr, bt_scr, c_scr, h_scr, *, seq):
    L = MLSTM_BLOCK
    nc = seq // L
    ns = 2 * M_HEADS
    scale = M_QK_DIM ** -0.5
    row = lax.broadcasted_iota(jnp.int32, (L, L), 0)
    col = lax.broadcasted_iota(jnp.int32, (L, L), 1)
    lower = col <= row
    upper = col >= row
    tril = jnp.where(lower, 1.0, 0.0).astype(BF16)
    triu = jnp.where(upper, 1.0, 0.0).astype(BF16)

    fwd_lane = lax.broadcasted_iota(jnp.int32, (1, N_GATES), 1) < ns + M_HEADS
    for c in range(nc):
        rows = slice(c * L, (c + 1) * L)
        lf = jax.nn.log_sigmoid(gcol_ref[0, rows, :] + bcol_ref[...])
        lf3 = _split3(lf)
        pre = sum(jnp.dot(tril, part, preferred_element_type=F32) for part in lf3)
        suf = sum(jnp.dot(triu, part, preferred_element_type=F32) for part in lf3)
        bcol = jnp.where(fwd_lane, pre, suf)
        for k in range(ns):
            bc_scr[k, rows, :] = jnp.broadcast_to(bcol[:, ns + k:ns + k + 1], (L, LANES))
        kt_scr[:, rows] = k_ref[0, rows, :].astype(F32).T.astype(BF16)

    gr = grow_ref[0] + brow_ref[...]
    ig_r = gr[:ns].reshape(ns * nc, L)
    lf_r = jax.nn.log_sigmoid(gr[ns:]).reshape(ns * nc, L)
    lf_r3 = _split3(lf_r)
    pre_r = sum(jnp.dot(part, triu, preferred_element_type=F32) for part in lf_r3)
    suf_r = sum(jnp.dot(part, tril, preferred_element_type=F32) for part in lf_r3)
    fwd_rows = lax.broadcasted_iota(jnp.int32, (ns * nc, 1), 0) < M_HEADS * nc
    a_r = ig_r - jnp.where(fwd_rows, pre_r, suf_r)
    btot = jnp.sum(lf_r, axis=-1, keepdims=True)
    ar_scr[...] = a_r.reshape(ns, nc, L)
    wr_scr[...] = (btot + a_r).reshape(ns, nc, L)
    bt_scr[...] = jnp.broadcast_to(btot, (ns * nc, LANES)).reshape(ns, nc, LANES)

    c_scr[...] = jnp.zeros_like(c_scr)
    h_scr[...] = jnp.zeros_like(h_scr)
    ones_cols = jnp.ones((L, LANES), BF16)

    twice = lambda t: jnp.concatenate([t, t], axis=1)

    def chunk_step(k, c, m_st):
        d, h = divmod(k, M_HEADS)
        r0 = pl.multiple_of(c * L, L)
        q = q_ref[0, pl.ds(r0, L), h * M_QK_DIM:(h + 1) * M_QK_DIM]
        kk = k_ref[0, pl.ds(r0, L), h * M_QK_DIM:(h + 1) * M_QK_DIM]
        v_ext = jnp.concatenate([v_ref[0, pl.ds(r0, L), h * M_V_DIM:(h + 1) * M_V_DIM], ones_cols], axis=1)
        bc = bc_scr[k, pl.ds(r0, L), :]
        ar = ar_scr[k, pl.ds(c, 1), :]
        wr = wr_scr[k, pl.ds(c, 1), :]
        bt = bt_scr[k, pl.ds(c, 1), :]
        dm = jnp.where(lower if d == 0 else upper, twice(bc) + ar, NEG)
        g_inter = bc + m_st
        m_t = jnp.maximum(jnp.broadcast_to(jnp.max(dm, axis=-1, keepdims=True), (L, LANES)), g_inter)
        e_inter = jnp.exp(g_inter - m_t) * scale
        s_qk = lax.dot_general(q, kk, (((1,), (1,)), ((), ())), preferred_element_type=F32)
        p = s_qk * scale * jnp.exp(dm - twice(m_t))
        qc = jnp.dot(q, c_scr[k].astype(BF16), preferred_element_type=F32)
        pv = jnp.dot(p.astype(BF16), v_ext, preferred_element_type=F32)
        num = pv[:, :M_V_DIM] + twice(e_inter) * qc[:, :M_V_DIM]
        den = pv[:, M_V_DIM:] + e_inter * qc[:, M_V_DIM:]
        inv = 1.0 / jnp.maximum(jnp.abs(den), jnp.exp(-m_t))
        hsl = (pl.ds(r0, L), slice(h * M_V_DIM, (h + 1) * M_V_DIM))
        h_scr[hsl] = h_scr[hsl] + num * twice(inv)
        m_new = jnp.maximum(bt + m_st, jnp.broadcast_to(jnp.max(wr, axis=-1, keepdims=True), (1, LANES)))
        a = jnp.exp(bt + m_st - m_new)
        kt = kt_scr[h * M_QK_DIM:(h + 1) * M_QK_DIM, pl.ds(r0, L)]
        ek_t = (kt.astype(F32) * jnp.exp(wr - twice(m_new))).astype(BF16)
        c_scr[k] = jnp.concatenate([a, a, a], axis=1) * c_scr[k] + jnp.dot(ek_t, v_ext, preferred_element_type=F32)
        return m_new

    def body(it, ms):
        out = []
        for k in range(ns):
            c = it if k < M_HEADS else nc - 1 - it
            out.append(chunk_step(k, c, ms[k]))
        return tuple(out)

    lax.fori_loop(0, nc, body, tuple(jnp.zeros((1, LANES), F32) for _ in range(ns)))

    def fin(i, carry):
        r0 = pl.multiple_of(i * L, L)
        for h in range(M_HEADS):
            sl = slice(h * M_V_DIM, (h + 1) * M_V_DIM)
            x = h_scr[pl.ds(r0, L), sl]
            y = x * lax.rsqrt(jnp.mean(x * x, axis=-1, keepdims=True) + EPS) * gain_ref[:, sl]
            o_ref[0, pl.ds(r0, L), sl] = (y * jax.nn.sigmoid(mo_ref[0, pl.ds(r0, L), sl].astype(F32))).astype(BF16)
        return carry

    lax.fori_loop(0, nc, fin, 0)


def _mlstm(proj, gates, b_gates, m_gain):
    b, s, _ = proj.shape
    L = MLSTM_BLOCK
    nc = s // L
    ns = 2 * M_HEADS
    grow = jnp.transpose(gates, (0, 2, 1)).reshape(b, N_GATES, nc, L)
    col = lambda width, off: pl.BlockSpec((1, s, width), lambda bi: (bi, 0, off // width))
    return pl.pallas_call(
        functools.partial(_mlstm_kernel, seq=s),
        grid=(b,),
        in_specs=[
            col(M_QK_WIDTH, COL_MQ), col(M_QK_WIDTH, COL_MK), col(M_WIDTH, COL_MV),
            pl.BlockSpec((1, s, M_WIDTH), lambda bi: (bi, 0, COL_MO // M_WIDTH), pipeline_mode=pl.Buffered(1)),
            pl.BlockSpec((1, s, N_GATES), lambda bi: (bi, 0, 0)),
            pl.BlockSpec((1, N_GATES, nc, L), lambda bi: (bi, 0, 0, 0)),
            pl.BlockSpec((1, N_GATES), lambda bi: (0, 0)),
            pl.BlockSpec((N_GATES, 1, 1), lambda bi: (0, 0, 0)),
            pl.BlockSpec((1, M_WIDTH), lambda bi: (0, 0)),
        ],
        out_specs=pl.BlockSpec((1, s, M_WIDTH), lambda bi: (bi, 0, 0)),
        out_shape=jax.ShapeDtypeStruct((b, s, M_WIDTH), BF16),
        scratch_shapes=[
            pltpu.VMEM((ns, s, LANES), F32),
            pltpu.VMEM((M_QK_WIDTH, s), BF16),
            pltpu.VMEM((ns, nc, L), F32), pltpu.VMEM((ns, nc, L), F32), pltpu.VMEM((ns, nc, LANES), F32),
            pltpu.VMEM((ns, M_QK_DIM, M_V_DIM + LANES), F32),
            pltpu.VMEM((s, M_WIDTH), F32),
        ],
        compiler_params=_params("arbitrary"),
        name="mlstm",
    )(proj, proj, proj, proj, gates, grow, b_gates.reshape(1, N_GATES), b_gates.reshape(N_GATES, 1, 1),
      m_gain.reshape(1, M_WIDTH))


def _outproj_kernel(att_ref, mo_ref, wa_ref, wm_ref, x_ref, g1_ref, sc_ref, sh_ref, wr_ref,
                    x1_ref, h2_ref, aff_ref):
    j = pl.program_id(2)
    nj = pl.num_programs(2)
    tn = OUTPROJ_TN
    mix = (jnp.dot(att_ref[0], wa_ref[...], preferred_element_type=F32)
           + jnp.dot(mo_ref[0], wm_ref[...], preferred_element_type=F32))
    x1t = x_ref[0] + g1_ref[0] * mix
    for jj in range(D_MODEL // tn):
        @pl.when(j == jj)
        def _(jj=jj):
            x1_ref[0, :, jj * tn:(jj + 1) * tn] = x1t

    @pl.when(j == nj - 1)
    def _():
        xf = x1_ref[0]
        r = lax.rsqrt(jnp.mean(xf * xf, axis=-1, keepdims=True) + EPS)
        h2 = (xf * r) * (1.0 + sc_ref[0]) + sh_ref[0]
        h2_hi = h2.astype(BF16)
        h2_ref[0] = h2
        terms = jnp.dot(h2_hi, wr_ref[...], preferred_element_type=F32)
        logits = terms[:, :N_EXPERTS] + terms[:, N_EXPERTS:]
        e = jnp.exp(logits - jnp.max(logits, axis=-1, keepdims=True))
        aff_ref[0] = e / jnp.sum(e, axis=-1, keepdims=True)


def _outproj(att, m_out, w_out, x, mod6, w_router):
    b, s, d = x.shape
    tm, tn = OUTPROJ_TM, OUTPROJ_TN
    half = ATT_WIDTH
    assert w_out.shape[0] == 2 * half and M_WIDTH == half
    wr_hi = w_router.astype(BF16)
    wr_lo = (w_router - wr_hi.astype(F32)).astype(BF16)
    return pl.pallas_call(
        _outproj_kernel,
        grid=(b, s // tm, d // tn),
        in_specs=[
            pl.BlockSpec((1, tm, ATT_WIDTH), lambda bi, i, j: (bi, i, 0)),
            pl.BlockSpec((1, tm, M_WIDTH), lambda bi, i, j: (bi, i, 0)),
            pl.BlockSpec((half, tn), lambda bi, i, j: (0, j)),
            pl.BlockSpec((half, tn), lambda bi, i, j: (1, j)),
            pl.BlockSpec((1, tm, tn), lambda bi, i, j: (bi, i, j)),
            pl.BlockSpec((1, 1, tn), lambda bi, i, j: (bi * 6 + 2, 0, j)),
            pl.BlockSpec((1, 1, d), lambda bi, i, j: (bi * 6 + 4, 0, 0)),
            pl.BlockSpec((1, 1, d), lambda bi, i, j: (bi * 6 + 3, 0, 0)),
            pl.BlockSpec((d, 2 * N_EXPERTS), lambda bi, i, j: (0, 0)),
        ],
        out_specs=[
            pl.BlockSpec((1, tm, d), lambda bi, i, j: (bi, i, 0)),
            pl.BlockSpec((1, tm, d), lambda bi, i, j: (bi, i, 0)),
            pl.BlockSpec((1, tm, N_EXPERTS), lambda bi, i, j: (bi, i, 0)),
        ],
        out_shape=[
            jax.ShapeDtypeStruct((b, s, d), F32),
            jax.ShapeDtypeStruct((b, s, d), F32),
            jax.ShapeDtypeStruct((b, s, N_EXPERTS), F32),
        ],
        compiler_params=_params("arbitrary", "arbitrary", "arbitrary"),
        name="out_proj",
    )(att, m_out, w_out, w_out, x, mod6, mod6, mod6, jnp.concatenate([wr_hi, wr_lo], axis=1))


def _prefix_sum_lanes(x, n):
    lane = lax.broadcasted_iota(jnp.int32, x.shape, 1)
    sh = 1
    while sh < n:
        x = x + jnp.where(lane >= sh, pltpu.roll(x, sh, 1), 0.0)
        sh *= 2
    return x


def _select_kernel(aff_ref, pos_ref, gate_ref, *, cap, seq):
    a = aff_ref[...]
    ne = a.shape[0]

    def body(_, lohi):
        lo, hi = lohi
        mid = jnp.where(lo > 0.0, 0.5 * (lo + hi), hi * (1.0 / 256.0))
        cnt = jnp.sum(jnp.where(a >= mid, 1.0, 0.0), axis=-1, keepdims=True)
        ok = cnt >= cap
        return jnp.where(ok, mid, lo), jnp.where(ok, hi, mid)

    lo0 = jnp.zeros((ne, 1), F32)
    hi0 = jnp.maximum(2.0 * jnp.max(a, axis=-1, keepdims=True), SELECT_MIN_UPPER)
    lo, hi = lax.fori_loop(0, SELECT_BISECTIONS, body, (lo0, hi0))
    above = a >= hi
    band = jnp.logical_and(a >= lo, jnp.logical_not(above))
    n_above = jnp.sum(jnp.where(above, 1.0, 0.0), axis=-1, keepdims=True)
    band_rank = _prefix_sum_lanes(jnp.where(band, 1.0, 0.0), seq)
    sel = jnp.where(above, 1.0, jnp.where(band, jnp.where(band_rank <= cap - n_above, 1.0, 0.0), 0.0))
    pos = _prefix_sum_lanes(sel, seq) - 1.0
    chosen = sel > 0.5
    pos_ref[...] = jnp.where(chosen, pos, -1.0)
    gate_ref[...] = jnp.where(chosen, a, 0.0)


def _select(aff_rows, cap):
    n, s = aff_rows.shape
    spec = pl.BlockSpec((n, s), lambda i: (0, 0))
    return pl.pallas_call(
        functools.partial(_select_kernel, cap=cap, seq=s),
        grid=(1,),
        in_specs=[spec],
        out_specs=[spec, spec],
        out_shape=[jax.ShapeDtypeStruct((n, s), F32), jax.ShapeDtypeStruct((n, s), F32)],
        compiler_params=_params("arbitrary"),
        name="expert_select",
    )(aff_rows)


def _lane_spread(ne):
    blk = lax.broadcasted_iota(jnp.int32, (ne, ne * LANES), 1) // LANES
    return jnp.where(blk == lax.broadcasted_iota(jnp.int32, (ne, ne * LANES), 0), 1.0, 0.0).astype(BF16)


def _compact_kernel(pos_ref, idx_ref, *, cap):
    seq, ne = pos_ref.shape[1], pos_ref.shape[2]
    rt = COMBINE_BUILD_ROWS
    halves = cap // LANES
    spread = _lane_spread(ne)
    lane = lax.broadcasted_iota(jnp.int32, (rt, LANES), 1).astype(F32)
    tok0 = lax.broadcasted_iota(jnp.int32, (rt, LANES), 0).astype(F32)

    def body(i, accs):
        r0 = pl.multiple_of(i * rt, rt)
        pos = jnp.dot(pos_ref[0, pl.ds(r0, rt), :].astype(BF16), spread, preferred_element_type=F32)
        tok = tok0 + r0.astype(F32)
        out = []
        for e in range(ne):
            p = pos[:, e * LANES:(e + 1) * LANES]
            for h in range(halves):
                hit = lane + float(h * LANES) == p
                out.append(accs[e * halves + h] + jnp.sum(jnp.where(hit, tok, 0.0), axis=0, keepdims=True))
        return tuple(out)

    accs = lax.fori_loop(0, seq // rt, body, tuple(jnp.zeros((1, LANES), F32) for _ in range(ne * halves)))
    for e in range(ne):
        for h in range(halves):
            idx_ref[0, e:e + 1, h * LANES:(h + 1) * LANES] = accs[e * halves + h].astype(jnp.int32)


def _compact(pos_cols, cap):
    b, s, ne = pos_cols.shape
    return pl.pallas_call(
        functools.partial(_compact_kernel, cap=cap),
        grid=(b,),
        in_specs=[pl.BlockSpec((1, s, ne), lambda bi: (bi, 0, 0))],
        out_specs=pl.BlockSpec((1, ne, cap), lambda bi: (bi, 0, 0)),
        out_shape=jax.ShapeDtypeStruct((b, ne, cap), jnp.int32),
        compiler_params=_params("arbitrary"),
        name="expert_compact",
    )(pos_cols)


def _ffn_kernel(idx_ref, h2_hbm, wg_ref, wu_ref, wd_ref, ye_ref, x_scr, hid_scr, sem, *, nb, cap):
    e = pl.program_id(0)
    f = pl.program_id(1)
    ne = pl.num_programs(0)
    nf = pl.num_programs(1)
    rows = nb * cap
    per_step = rows // FFN_UP_STEPS
    tf = wg_ref.shape[-1]
    slot = lax.rem(e, 2)
    nslot = 1 - slot
    nxt = jnp.minimum(e + 1, ne - 1)

    def row_copy(expert, r, sl):
        row = idx_ref[expert, r]
        return pltpu.make_async_copy(h2_hbm.at[pl.ds(row, 1), :], x_scr.at[sl, pl.ds(r, 1), :], sem.at[sl])

    def wait_rows(sl):
        pltpu.make_async_copy(x_scr.at[sl], x_scr.at[sl], sem.at[sl]).wait()

    @pl.when(jnp.logical_and(e == 0, f == 0))
    def _():
        def first(r, carry):
            row_copy(0, r, 0).start()
            return carry
        lax.fori_loop(0, rows, first, 0)

    for ff in range(FFN_UP_STEPS):
        for cur in range(2):
            @pl.when(jnp.logical_and(f == ff, slot == cur))
            def _(ff=ff, cur=cur):
                if ff == 0:
                    wait_rows(cur)
                for r in range(ff * per_step, (ff + 1) * per_step):
                    row_copy(nxt, r, 1 - cur).start(priority=r % 2)
                x = x_scr[cur].astype(BF16)
                g = jnp.dot(x, wg_ref[0, 0].astype(BF16), preferred_element_type=F32)
                u = jnp.dot(x, wu_ref[0, 0].astype(BF16), preferred_element_type=F32)
                hid_scr[:, ff * tf:(ff + 1) * tf] = ((g * jax.nn.sigmoid(g)) * u).astype(BF16)

    @pl.when(f >= FFN_UP_STEPS)
    def _():
        y = jnp.dot(hid_scr[...], wd_ref[0, 0].astype(BF16), preferred_element_type=F32)
        ye_ref[...] = y.reshape(nb, 1, cap, y.shape[-1]).astype(BF16)

    @pl.when(jnp.logical_and(e == ne - 1, f == nf - 1))
    def _():
        wait_rows(nslot)


def _ffn(idx, h2, w_gate, w_up, w_down, layer):
    b, s, d = h2.shape
    idx = idx + (jnp.arange(idx.shape[1], dtype=jnp.int32) // (idx.shape[1] // b) * s)[None, :]
    ne = idx.shape[0]
    cap = idx.shape[1] // b
    ff = w_gate.shape[-1]
    tf = ff // FFN_UP_STEPS
    dn = d // FFN_DOWN_STEPS
    up = lambda f: jnp.minimum(f, FFN_UP_STEPS - 1)
    down = lambda f: jnp.maximum(f - FFN_UP_STEPS, 0)
    grid_spec = pltpu.PrefetchScalarGridSpec(
        num_scalar_prefetch=1,
        grid=(ne, FFN_UP_STEPS + FFN_DOWN_STEPS),
        in_specs=[
            pl.BlockSpec(memory_space=pl.ANY),
            pl.BlockSpec((1, 1, d, tf), lambda e, f, idx_ref: (layer, e, 0, up(f))),
            pl.BlockSpec((1, 1, d, tf), lambda e, f, idx_ref: (layer, e, 0, up(f))),
            pl.BlockSpec((1, 1, ff, dn), lambda e, f, idx_ref: (layer, e, 0, down(f))),
        ],
        out_specs=pl.BlockSpec((b, 1, cap, dn), lambda e, f, idx_ref: (0, e, 0, down(f))),
        scratch_shapes=[
            pltpu.VMEM((2, b * cap, d), F32),
            pltpu.VMEM((b * cap, ff), BF16),
            pltpu.SemaphoreType.DMA((2,)),
        ],
    )
    return pl.pallas_call(
        functools.partial(_ffn_kernel, nb=b, cap=cap),
        grid_spec=grid_spec,
        out_shape=jax.ShapeDtypeStruct((b, ne, cap, d), BF16),
        compiler_params=_params("arbitrary", "arbitrary"),
        name="expert_ffn",
    )(idx, h2.reshape(b * s, d), w_gate, w_up, w_down)


def _combine_kernel(pos_ref, gate_ref, ye_ref, x1_ref, g2_ref, o_ref, scat_scr, *, cap):
    j = pl.program_id(1)
    seq, ne = pos_ref.shape[1], pos_ref.shape[2]
    rt = COMBINE_BUILD_ROWS

    @pl.when(j == 0)
    def _():
        spread = _lane_spread(ne)
        lane = lax.broadcasted_iota(jnp.int32, (rt, LANES), 1).astype(F32)

        def build(i, carry):
            r0 = pl.multiple_of(i * rt, rt)
            pos = jnp.dot(pos_ref[0, pl.ds(r0, rt), :].astype(BF16), spread, preferred_element_type=F32)
            gate = jnp.dot(gate_ref[0, pl.ds(r0, rt), :].astype(BF16), spread, preferred_element_type=F32)
            for e in range(ne):
                p = pos[:, e * LANES:(e + 1) * LANES]
                g = gate[:, e * LANES:(e + 1) * LANES]
                for h in range(cap // LANES):
                    cols = slice(e * cap + h * LANES, e * cap + (h + 1) * LANES)
                    scat_scr[pl.ds(r0, rt), cols] = jnp.where(lane + float(h * LANES) == p, g, 0.0).astype(BF16)
            return carry

        lax.fori_loop(0, seq // rt, build, 0)

    acc = jnp.dot(scat_scr[...], ye_ref[0], preferred_element_type=F32)
    o_ref[0] = x1_ref[0] + g2_ref[0] * acc


def _combine(pos_cols, gate_cols, ye, x1, mod6):
    b, s, d = x1.shape
    ne, cap = ye.shape[1], ye.shape[2]
    tn = COMBINE_TN
    return pl.pallas_call(
        functools.partial(_combine_kernel, cap=cap),
        grid=(b, d // tn),
        in_specs=[
            pl.BlockSpec((1, s, ne), lambda bi, j: (bi, 0, 0)),
            pl.BlockSpec((1, s, ne), lambda bi, j: (bi, 0, 0)),
            pl.BlockSpec((1, ne * cap, tn), lambda bi, j: (bi, 0, j)),
            pl.BlockSpec((1, s, tn), lambda bi, j: (bi, 0, j)),
            pl.BlockSpec((1, 1, tn), lambda bi, j: (bi * 6 + 5, 0, j)),
        ],
        out_specs=pl.BlockSpec((1, s, tn), lambda bi, j: (bi, 0, j)),
        out_shape=jax.ShapeDtypeStruct((b, s, d), F32),
        scratch_shapes=[pltpu.VMEM((s, ne * cap), BF16)],
        compiler_params=_params("arbitrary", "arbitrary"),
        name="expert_combine",
    )(pos_cols, gate_cols, ye.reshape(b, ne * cap, d), x1, mod6)


def _rope_tables(seq):
    inv = 1.0 / (ROPE_THETA ** (jnp.arange(0, ATT_HEAD_DIM, 2, dtype=F32) / ATT_HEAD_DIM))
    ang = jnp.arange(seq, dtype=F32)[:, None] * inv[None, :]
    ang = jnp.concatenate([ang, ang], axis=-1)
    sign = jnp.where(jnp.arange(ATT_HEAD_DIM) < ATT_HEAD_DIM // 2, -1.0, 1.0).astype(F32)
    return jnp.cos(ang), jnp.sin(ang) * sign


def kernel(x, c, w_ada, b_ada, w_in, b_gates, q_gain, k_gain, sink, m_gain, w_out,
           w_router, w_gate, w_up, w_down):
    b, s, d = x.shape
    depth = w_ada.shape[0]
    cap = CAPACITY_FACTOR * s // N_EXPERTS
    cos, sin_signed = _rope_tables(s)
    c_pad = jnp.pad(c, ((0, SUBLANES - b), (0, 0)))
    mod = _ada(c_pad, w_ada, b_ada)
    w_in_bf = w_in.astype(BF16)
    for l in range(depth):
        mod6 = mod[l, :b].reshape(b * 6, 1, d)
        proj, gates = _inproj(x, mod6, w_in_bf, w_in[l, :, PROJ_WIDTH:].astype(BF16), l, cos, sin_signed,
                              q_gain[l].reshape(1, -1), k_gain[l].reshape(1, -1))
        att = _attention(proj, sink[l])
        m_out = _mlstm(proj, gates, b_gates[l].reshape(-1), m_gain[l])
        x1, h2, aff = _outproj(att, m_out, w_out[l].astype(BF16), x, mod6, w_router[l])
        pos_r, gate_r = _select(jnp.transpose(aff, (0, 2, 1)).reshape(b * N_EXPERTS, s), cap)
        to_cols = lambda t: jnp.transpose(t.reshape(b, N_EXPERTS, s), (0, 2, 1))
        pos_c = to_cols(pos_r)
        idx = jnp.transpose(_compact(pos_c, cap), (1, 0, 2)).reshape(N_EXPERTS, b * cap)
        ye = _ffn(idx, h2, w_gate, w_up, w_down, l)
        x = _combine(pos_c, to_cols(gate_r), ye, x1, mod6)
    return x
```

```python
import functools

import jax
import jax.numpy as jnp
from jax import lax
from jax.experimental import pallas as pl
from jax.experimental.pallas import tpu as pltpu

F32 = jnp.float32
BF16 = jnp.bfloat16
SUBLANES = 8
LANES = 128

D_MODEL = 2048
ATT_HEAD_DIM = 128
ATT_HEADS = 8
ATT_KV_HEADS = 2
ATT_GROUP = ATT_HEADS // ATT_KV_HEADS
ATT_WIDTH = ATT_HEADS * ATT_HEAD_DIM
ATT_KV_WIDTH = ATT_KV_HEADS * ATT_HEAD_DIM
WINDOW = 128
ATT_BLOCK = 128
ATT_STEP_BLOCKS = 8
ROPE_THETA = 10000.0
M_HEADS = 4
M_V_DIM = 256
M_QK_DIM = 128
M_WIDTH = M_HEADS * M_V_DIM
M_QK_WIDTH = M_HEADS * M_QK_DIM
MLSTM_BLOCK = 256
N_GATES = 4 * M_HEADS
N_EXPERTS = 16
EXPERT_FF = D_MODEL // 2
CAPACITY_FACTOR = 2
EPS = 1e-6
NEG = -1e30

COL_AQ = 0
COL_MV = ATT_WIDTH
COL_MO = COL_MV + M_WIDTH
COL_AK = COL_MO + M_WIDTH
COL_AV = COL_AK + ATT_KV_WIDTH
COL_MQ = COL_AV + ATT_KV_WIDTH
COL_MK = COL_MQ + M_QK_WIDTH
PROJ_WIDTH = COL_MK + M_QK_WIDTH
IN_COL_BLOCKS = (0, 1, 5, 6, 7, 8, 2, 3, 4)

VMEM_LIMIT_BYTES = 56 * 1024 * 1024

ADA_TN = 1024
ADA_K_SPLITS = 4
INPROJ_TM = 1024
INPROJ_TN = 512
OUTPROJ_TM = 1024
OUTPROJ_TN = 512
FFN_UP_STEPS = 4
FFN_DOWN_STEPS = 2
COMBINE_TN = 512
COMBINE_BUILD_ROWS = 256
SELECT_BISECTIONS = 56
SELECT_MIN_UPPER = 1e-30


def _params(*sem):
    return pltpu.CompilerParams(dimension_semantics=sem, vmem_limit_bytes=VMEM_LIMIT_BYTES)


def _ada_kernel(c_ref, *refs):
    w_refs, b_ref, o_ref = refs[:ADA_K_SPLITS], refs[ADA_K_SPLITS], refs[ADA_K_SPLITS + 1]
    c = c_ref[...]
    c_act = c * jax.nn.sigmoid(c)
    c_hi = c_act.astype(BF16)
    c_lo = (c_act - c_hi.astype(F32)).astype(BF16)
    kb = c.shape[1] // ADA_K_SPLITS
    acc = b_ref[0]
    for s, w_ref in enumerate(w_refs):
        w = w_ref[0]
        w_hi = w.astype(BF16)
        w_lo = (w - w_hi.astype(F32)).astype(BF16)
        ks = slice(s * kb, (s + 1) * kb)
        acc = acc + jnp.dot(c_hi[:, ks], w_hi, preferred_element_type=F32)
        acc = acc + jnp.dot(c_lo[:, ks], w_hi, preferred_element_type=F32)
        acc = acc + jnp.dot(c_hi[:, ks], w_lo, preferred_element_type=F32)
    o_ref[0] = acc


def _ada(c_pad, w_ada, b_ada):
    depth, d, n = w_ada.shape
    rows = c_pad.shape[0]
    return pl.pallas_call(
        _ada_kernel,
        grid=(depth, n // ADA_TN),
        in_specs=(
            [pl.BlockSpec((rows, d), lambda l, j: (0, 0))]
            + [pl.BlockSpec((1, d // ADA_K_SPLITS, ADA_TN), functools.partial(lambda l, j, s: (l, s, j), s=s))
               for s in range(ADA_K_SPLITS)]
            + [pl.BlockSpec((1, 1, ADA_TN), lambda l, j: (l, 0, j))]),
        out_specs=pl.BlockSpec((1, rows, ADA_TN), lambda l, j: (l, 0, j)),
        out_shape=jax.ShapeDtypeStruct((depth, rows, n), F32),
        compiler_params=_params("arbitrary", "arbitrary"),
        name="ada_mod",
    )(c_pad, *([w_ada] * ADA_K_SPLITS), b_ada.reshape(depth, 1, n))


def _inproj_kernel(perm_ref, x_ref, sc_ref, sh_ref, w_ref, wg_ref, cos_ref, sin_ref, qg_ref, kg_ref,
                   o_ref, g_ref, h_scr):
    del perm_ref
    j = pl.program_id(2)

    @pl.when(j == 0)
    def _():
        x = x_ref[0]
        r = lax.rsqrt(jnp.mean(x * x, axis=-1, keepdims=True) + EPS)
        h = (x * r) * (1.0 + sc_ref[0]) + sh_ref[0]
        hb = h.astype(BF16)
        h_scr[...] = hb
        g_ref[0] = jnp.dot(hb, wg_ref[...], preferred_element_type=F32)

    y = jnp.dot(h_scr[...], w_ref[0], preferred_element_type=F32)

    head_ones = jnp.ones((ATT_HEAD_DIM, ATT_HEAD_DIM), BF16)

    def norm_rope(t, gain):
        ss = jnp.dot((t * t).astype(BF16), head_ones, preferred_element_type=F32)
        tn = t * lax.rsqrt(ss * (1.0 / ATT_HEAD_DIM) + EPS) * gain
        return tn * cos_ref[...] + pltpu.roll(tn, ATT_HEAD_DIM // 2, 1) * sin_ref[...]

    heads_per_tile = INPROJ_TN // ATT_HEAD_DIM
    q_tiles = ATT_WIDTH // INPROJ_TN
    kv_tile = COL_AK // INPROJ_TN
    scale = ATT_HEAD_DIM ** -0.5

    @pl.when(j < q_tiles)
    def _():
        for u in range(heads_per_tile):
            sl = slice(u * ATT_HEAD_DIM, (u + 1) * ATT_HEAD_DIM)
            o_ref[0, :, sl] = (norm_rope(y[:, sl], qg_ref[...]) * scale).astype(BF16)

    @pl.when(j == kv_tile)
    def _():
        for u in range(ATT_KV_HEADS):
            sl = slice(u * ATT_HEAD_DIM, (u + 1) * ATT_HEAD_DIM)
            o_ref[0, :, sl] = norm_rope(y[:, sl], kg_ref[...]).astype(BF16)
        o_ref[0, :, ATT_KV_WIDTH:] = y[:, ATT_KV_WIDTH:].astype(BF16)

    @pl.when(jnp.logical_and(j >= q_tiles, j != kv_tile))
    def _():
        o_ref[0] = y.astype(BF16)


def _inproj(x, mod6, w_bf, w_gate, layer, cos, sin_signed, q_gain, k_gain):
    b, s, d = x.shape
    tm, tn = INPROJ_TM, INPROJ_TN
    assert COL_AK % tn == 0 and ATT_WIDTH % tn == 0 and 2 * ATT_KV_WIDTH == tn
    grid_spec = pltpu.PrefetchScalarGridSpec(
        num_scalar_prefetch=1,
        grid=(b, s // tm, PROJ_WIDTH // tn),
        in_specs=[
            pl.BlockSpec((1, tm, d), lambda bi, i, j, perm: (bi, i, 0)),
            pl.BlockSpec((1, 1, d), lambda bi, i, j, perm: (bi * 6 + 1, 0, 0)),
            pl.BlockSpec((1, 1, d), lambda bi, i, j, perm: (bi * 6 + 0, 0, 0)),
            pl.BlockSpec((1, d, tn), lambda bi, i, j, perm: (layer, 0, perm[j])),
            pl.BlockSpec((d, N_GATES), lambda bi, i, j, perm: (0, 0)),
            pl.BlockSpec((tm, ATT_HEAD_DIM), lambda bi, i, j, perm: (i, 0)),
            pl.BlockSpec((tm, ATT_HEAD_DIM), lambda bi, i, j, perm: (i, 0)),
            pl.BlockSpec((1, ATT_HEAD_DIM), lambda bi, i, j, perm: (0, 0)),
            pl.BlockSpec((1, ATT_HEAD_DIM), lambda bi, i, j, perm: (0, 0)),
        ],
        out_specs=[
            pl.BlockSpec((1, tm, tn), lambda bi, i, j, perm: (bi, i, j)),
            pl.BlockSpec((1, tm, N_GATES), lambda bi, i, j, perm: (bi, i, 0)),
        ],
        scratch_shapes=[pltpu.VMEM((tm, d), BF16)],
    )
    return pl.pallas_call(
        _inproj_kernel,
        grid_spec=grid_spec,
        out_shape=[
            jax.ShapeDtypeStruct((b, s, PROJ_WIDTH), BF16),
            jax.ShapeDtypeStruct((b, s, N_GATES), F32),
        ],
        compiler_params=_params("arbitrary", "arbitrary", "arbitrary"),
        name="in_proj",
    )(jnp.asarray(IN_COL_BLOCKS, jnp.int32), x, mod6, mod6, w_bf, w_gate, cos, sin_signed, q_gain, k_gain)


def _attn_kernel(sink_ref, q_ref, kl_ref, kc_ref, kr_ref, vl_ref, vc_ref, vr_ref, o_ref, *, n_steps):
    n = pl.program_id(1)
    L = ATT_BLOCK
    rows = ATT_GROUP * L
    i = lax.broadcasted_iota(jnp.int32, (rows, 3 * L), 0) & (L - 1)
    jj = lax.broadcasted_iota(jnp.int32, (rows, 3 * L), 1)
    in_window = jnp.abs(jj - L - i) <= WINDOW
    first_mask = in_window & ((jj >= L) | (n > 0))
    last_mask = in_window & ((jj < 2 * L) | (n < n_steps - 1))
    rgrp = lax.broadcasted_iota(jnp.int32, (rows, 1), 0) // L
    for kv in range(ATT_KV_HEADS):
        hs = slice(kv * ATT_HEAD_DIM, (kv + 1) * ATT_HEAD_DIM)
        kband = jnp.concatenate([kl_ref[0, :, hs], kc_ref[0, :, hs], kr_ref[0, :, hs]], axis=0)
        vband = jnp.concatenate([vl_ref[0, :, hs], vc_ref[0, :, hs], vr_ref[0, :, hs]], axis=0)
        sink = jnp.zeros((rows, 1), F32)
        for g in range(ATT_GROUP):
            sink = jnp.where(rgrp == g, sink_ref[kv * ATT_GROUP + g], sink)
        for blk in range(ATT_STEP_BLOCKS):
            qrows = slice(blk * L, (blk + 1) * L)
            q = jnp.concatenate(
                [q_ref[0, qrows, (kv * ATT_GROUP + g) * ATT_HEAD_DIM:(kv * ATT_GROUP + g + 1) * ATT_HEAD_DIM]
                 for g in range(ATT_GROUP)], axis=0)
            kb = kband[blk * L:(blk + 3) * L]
            vb = vband[blk * L:(blk + 3) * L]
            valid = first_mask if blk == 0 else (last_mask if blk == ATT_STEP_BLOCKS - 1 else in_window)
            s = lax.dot_general(q, kb, (((1,), (1,)), ((), ())), preferred_element_type=F32)
            s = jnp.where(valid, s, NEG)
            m = jnp.maximum(jnp.max(s, axis=-1, keepdims=True), sink)
            p = jnp.exp(s - m)
            denom = jnp.sum(p, axis=-1, keepdims=True) + jnp.exp(sink - m)
            o = jnp.dot(p.astype(BF16), vb, preferred_element_type=F32) / denom
            for g in range(ATT_GROUP):
                h = kv * ATT_GROUP + g
                o_ref[0, qrows, h * ATT_HEAD_DIM:(h + 1) * ATT_HEAD_DIM] = o[g * L:(g + 1) * L].astype(BF16)


def _attention(proj, sink):
    b, s, _ = proj.shape
    L = ATT_BLOCK
    nb = s // L
    sb = ATT_STEP_BLOCKS
    assert sb >= 2 and nb % sb == 0
    kblk = COL_AK // ATT_KV_WIDTH
    vblk = COL_AV // ATT_KV_WIDTH
    left = lambda col: pl.BlockSpec((1, L, ATT_KV_WIDTH), lambda bi, n: (bi, jnp.maximum(n * sb - 1, 0), col))
    right = lambda col: pl.BlockSpec((1, L, ATT_KV_WIDTH), lambda bi, n: (bi, jnp.minimum((n + 1) * sb, nb - 1), col))
    centre = lambda col: pl.BlockSpec((1, sb * L, ATT_KV_WIDTH), lambda bi, n: (bi, n, col))
    return pl.pallas_call(
        functools.partial(_attn_kernel, n_steps=nb // sb),
        grid=(b, nb // sb),
        in_specs=[
            pl.BlockSpec(memory_space=pltpu.SMEM),
            pl.BlockSpec((1, sb * L, ATT_WIDTH), lambda bi, n: (bi, n, COL_AQ // ATT_WIDTH)),
            left(kblk), centre(kblk), right(kblk),
            left(vblk), centre(vblk), right(vblk),
        ],
        out_specs=pl.BlockSpec((1, sb * L, ATT_WIDTH), lambda bi, n: (bi, n, 0)),
        out_shape=jax.ShapeDtypeStruct((b, s, ATT_WIDTH), BF16),
        compiler_params=_params("arbitrary", "arbitrary"),
        name="window_attn",
    )(sink, proj, proj, proj, proj, proj, proj, proj)


def _split3(x):
    hi = x.astype(BF16)
    rest = x - hi.astype(F32)
    mid = rest.astype(BF16)
    return hi, mid, (rest - mid.astype(F32)).astype(BF16)


def _mlstm_kernel(q_ref, k_ref, v_ref, mo_ref, gcol_ref, grow_ref, bcol_ref, brow_ref, gain_ref, o_ref,
                  bc_scr, kt_scr, ar_scr, wr_scr, bt_scr, c_scr, h_scr, *, seq):
    L = MLSTM_BLOCK
    nc = seq // L
    ns = 2 * M_HEADS
    scale = M_QK_DIM ** -0.5
    row = lax.broadcasted_iota(jnp.int32, (L, L), 0)
    col = lax.broadcasted_iota(jnp.int32, (L, L), 1)
    lower = col <= row
    upper = col >= row
    tril = jnp.where(lower, 1.0, 0.0).astype(BF16)
    triu = jnp.where(upper, 1.0, 0.0).astype(BF16)

    fwd_lane = lax.broadcasted_iota(jnp.int32, (1, N_GATES), 1) < ns + M_HEADS
    for c in range(nc):
        rows = slice(c * L, (c + 1) * L)
        lf = jax.nn.log_sigmoid(gcol_ref[0, rows, :] + bcol_ref[...])
        lf3 = _split3(lf)
        pre = sum(jnp.dot(tril, part, preferred_element_type=F32) for part in lf3)
        suf = sum(jnp.dot(triu, part, preferred_element_type=F32) for part in lf3)
        bcol = jnp.where(fwd_lane, pre, suf)
        for k in range(ns):
            bc_scr[k, rows, :] = jnp.broadcast_to(bcol[:, ns + k:ns + k + 1], (L, LANES))
        kt_scr[:, rows] = k_ref[0, rows, :].astype(F32).T.astype(BF16)

    gr = grow_ref[0] + brow_ref[...]
    ig_r = gr[:ns].reshape(ns * nc, L)
    lf_r = jax.nn.log_sigmoid(gr[ns:]).reshape(ns * nc, L)
    lf_r3 = _split3(lf_r)
    pre_r = sum(jnp.dot(part, triu, preferred_element_type=F32) for part in lf_r3)
    suf_r = sum(jnp.dot(part, tril, preferred_element_type=F32) for part in lf_r3)
    fwd_rows = lax.broadcasted_iota(jnp.int32, (ns * nc, 1), 0) < M_HEADS * nc
    a_r = ig_r - jnp.where(fwd_rows, pre_r, suf_r)
    btot = jnp.sum(lf_r, axis=-1, keepdims=True)
    ar_scr[...] = a_r.reshape(ns, nc, L)
    wr_scr[...] = (btot + a_r).reshape(ns, nc, L)
    bt_scr[...] = jnp.broadcast_to(btot, (ns * nc, LANES)).reshape(ns, nc, LANES)

    c_scr[...] = jnp.zeros_like(c_scr)
    h_scr[...] = jnp.zeros_like(h_scr)
    ones_cols = jnp.ones((L, LANES), BF16)

    twice = lambda t: jnp.concatenate([t, t], axis=1)

    def chunk_step(k, c, m_st):
        d, h = divmod(k, M_HEADS)
        r0 = pl.multiple_of(c * L, L)
        q = q_ref[0, pl.ds(r0, L), h * M_QK_DIM:(h + 1) * M_QK_DIM]
        kk = k_ref[0, pl.ds(r0, L), h * M_QK_DIM:(h + 1) * M_QK_DIM]
        v_ext = jnp.concatenate([v_ref[0, pl.ds(r0, L), h * M_V_DIM:(h + 1) * M_V_DIM], ones_cols], axis=1)
        bc = bc_scr[k, pl.ds(r0, L), :]
        ar = ar_scr[k, pl.ds(c, 1), :]
        wr = wr_scr[k, pl.ds(c, 1), :]
        bt = bt_scr[k, pl.ds(c, 1), :]
        dm = jnp.where(lower if d == 0 else upper, twice(bc) + ar, NEG)
        g_inter = bc + m_st
        m_t = jnp.maximum(jnp.broadcast_to(jnp.max(dm, axis=-1, keepdims=True), (L, LANES)), g_inter)
        e_inter = jnp.exp(g_inter - m_t) * scale
        s_qk = lax.dot_general(q, kk, (((1,), (1,)), ((), ())), preferred_element_type=F32)
        p = s_qk * scale * jnp.exp(dm - twice(m_t))
        qc = jnp.dot(q, c_scr[k].astype(BF16), preferred_element_type=F32)
        pv = jnp.dot(p.astype(BF16), v_ext, preferred_element_type=F32)
        num = pv[:, :M_V_DIM] + twice(e_inter) * qc[:, :M_V_DIM]
        den = pv[:, M_V_DIM:] + e_inter * qc[:, M_V_DIM:]
        inv = 1.0 / jnp.maximum(jnp.abs(den), jnp.exp(-m_t))
        hsl = (pl.ds(r0, L), slice(h * M_V_DIM, (h + 1) * M_V_DIM))
        h_scr[hsl] = h_scr[hsl] + num * twice(inv)
        m_new = jnp.maximum(bt + m_st, jnp.broadcast_to(jnp.max(wr, axis=-1, keepdims=True), (1, LANES)))
        a = jnp.exp(bt + m_st - m_new)
        kt = kt_scr[h * M_QK_DIM:(h + 1) * M_QK_DIM, pl.ds(r0, L)]
        ek_t = (kt.astype(F32) * jnp.exp(wr - twice(m_new))).astype(BF16)
        c_scr[k] = jnp.concatenate([a, a, a], axis=1) * c_scr[k] + jnp.dot(ek_t, v_ext, preferred_element_type=F32)
        return m_new

    def body(it, ms):
        out = []
        for k in range(ns):
            c = it if k < M_HEADS else nc - 1 - it
            out.append(chunk_step(k, c, ms[k]))
        return tuple(out)

    lax.fori_loop(0, nc, body, tuple(jnp.zeros((1, LANES), F32) for _ in range(ns)))

    def fin(i, carry):
        r0 = pl.multiple_of(i * L, L)
        for h in range(M_HEADS):
            sl = slice(h * M_V_DIM, (h + 1) * M_V_DIM)
            x = h_scr[pl.ds(r0, L), sl]
            y = x * lax.rsqrt(jnp.mean(x * x, axis=-1, keepdims=True) + EPS) * gain_ref[:, sl]
            o_ref[0, pl.ds(r0, L), sl] = (y * jax.nn.sigmoid(mo_ref[0, pl.ds(r0, L), sl].astype(F32))).astype(BF16)
        return carry

    lax.fori_loop(0, nc, fin, 0)


def _mlstm(proj, gates, b_gates, m_gain):
    b, s, _ = proj.shape
    L = MLSTM_BLOCK
    nc = s // L
    ns = 2 * M_HEADS
    grow = jnp.transpose(gates, (0, 2, 1)).reshape(b, N_GATES, nc, L)
    col = lambda width, off: pl.BlockSpec((1, s, width), lambda bi: (bi, 0, off // width))
    return pl.pallas_call(
        functools.partial(_mlstm_kernel, seq=s),
        grid=(b,),
        in_specs=[
            col(M_QK_WIDTH, COL_MQ), col(M_QK_WIDTH, COL_MK), col(M_WIDTH, COL_MV),
            pl.BlockSpec((1, s, M_WIDTH), lambda bi: (bi, 0, COL_MO // M_WIDTH), pipeline_mode=pl.Buffered(1)),
            pl.BlockSpec((1, s, N_GATES), lambda bi: (bi, 0, 0)),
            pl.BlockSpec((1, N_GATES, nc, L), lambda bi: (bi, 0, 0, 0)),
            pl.BlockSpec((1, N_GATES), lambda bi: (0, 0)),
            pl.BlockSpec((N_GATES, 1, 1), lambda bi: (0, 0, 0)),
            pl.BlockSpec((1, M_WIDTH), lambda bi: (0, 0)),
        ],
        out_specs=pl.BlockSpec((1, s, M_WIDTH), lambda bi: (bi, 0, 0)),
        out_shape=jax.ShapeDtypeStruct((b, s, M_WIDTH), BF16),
        scratch_shapes=[
            pltpu.VMEM((ns, s, LANES), F32),
            pltpu.VMEM((M_QK_WIDTH, s), BF16),
            pltpu.VMEM((ns, nc, L), F32), pltpu.VMEM((ns, nc, L), F32), pltpu.VMEM((ns, nc, LANES), F32),
            pltpu.VMEM((ns, M_QK_DIM, M_V_DIM + LANES), F32),
            pltpu.VMEM((s, M_WIDTH), F32),
        ],
        compiler_params=_params("arbitrary"),
        name="mlstm",
    )(proj, proj, proj, proj, gates, grow, b_gates.reshape(1, N_GATES), b_gates.reshape(N_GATES, 1, 1),
      m_gain.reshape(1, M_WIDTH))


def _outproj_kernel(att_ref, mo_ref, wa_ref, wm_ref, x_ref, g1_ref, sc_ref, sh_ref, wr_ref,
                    x1_ref, h2_ref, aff_ref):
    j = pl.program_id(2)
    nj = pl.num_programs(2)
    tn = OUTPROJ_TN
    mix = (jnp.dot(att_ref[0], wa_ref[...], preferred_element_type=F32)
           + jnp.dot(mo_ref[0], wm_ref[...], preferred_element_type=F32))
    x1t = x_ref[0] + g1_ref[0] * mix
    for jj in range(D_MODEL // tn):
        @pl.when(j == jj)
        def _(jj=jj):
            x1_ref[0, :, jj * tn:(jj + 1) * tn] = x1t

    @pl.when(j == nj - 1)
    def _():
        xf = x1_ref[0]
        r = lax.rsqrt(jnp.mean(xf * xf, axis=-1, keepdims=True) + EPS)
        h2 = (xf * r) * (1.0 + sc_ref[0]) + sh_ref[0]
        h2_hi = h2.astype(BF16)
        h2_ref[0] = h2
        terms = jnp.dot(h2_hi, wr_ref[...], preferred_element_type=F32)
        logits = terms[:, :N_EXPERTS] + terms[:, N_EXPERTS:]
        e = jnp.exp(logits - jnp.max(logits, axis=-1, keepdims=True))
        aff_ref[0] = e / jnp.sum(e, axis=-1, keepdims=True)


def _outproj(att, m_out, w_out, x, mod6, w_router):
    b, s, d = x.shape
    tm, tn = OUTPROJ_TM, OUTPROJ_TN
    half = ATT_WIDTH
    assert w_out.shape[0] == 2 * half and M_WIDTH == half
    wr_hi = w_router.astype(BF16)
    wr_lo = (w_router - wr_hi.astype(F32)).astype(BF16)
    return pl.pallas_call(
        _outproj_kernel,
        grid=(b, s // tm, d // tn),
        in_specs=[
            pl.BlockSpec((1, tm, ATT_WIDTH), lambda bi, i, j: (bi, i, 0)),
            pl.BlockSpec((1, tm, M_WIDTH), lambda bi, i, j: (bi, i, 0)),
            pl.BlockSpec((half, tn), lambda bi, i, j: (0, j)),
            pl.BlockSpec((half, tn), lambda bi, i, j: (1, j)),
            pl.BlockSpec((1, tm, tn), lambda bi, i, j: (bi, i, j)),
            pl.BlockSpec((1, 1, tn), lambda bi, i, j: (bi * 6 + 2, 0, j)),
            pl.BlockSpec((1, 1, d), lambda bi, i, j: (bi * 6 + 4, 0, 0)),
            pl.BlockSpec((1, 1, d), lambda bi, i, j: (bi * 6 + 3, 0, 0)),
            pl.BlockSpec((d, 2 * N_EXPERTS), lambda bi, i, j: (0, 0)),
        ],
        out_specs=[
            pl.BlockSpec((1, tm, d), lambda bi, i, j: (bi, i, 0)),
            pl.BlockSpec((1, tm, d), lambda bi, i, j: (bi, i, 0)),
            pl.BlockSpec((1, tm, N_EXPERTS), lambda bi, i, j: (bi, i, 0)),
        ],
        out_shape=[
            jax.ShapeDtypeStruct((b, s, d), F32),
            jax.ShapeDtypeStruct((b, s, d), F32),
            jax.ShapeDtypeStruct((b, s, N_EXPERTS), F32),
        ],
        compiler_params=_params("arbitrary", "arbitrary", "arbitrary"),
        name="out_proj",
    )(att, m_out, w_out, w_out, x, mod6, mod6, mod6, jnp.concatenate([wr_hi, wr_lo], axis=1))


def _prefix_sum_lanes(x, n):
    lane = lax.broadcasted_iota(jnp.int32, x.shape, 1)
    sh = 1
    while sh < n:
        x = x + jnp.where(lane >= sh, pltpu.roll(x, sh, 1), 0.0)
        sh *= 2
    return x


def _select_kernel(aff_ref, pos_ref, gate_ref, *, cap, seq):
    a = aff_ref[...]
    ne = a.shape[0]

    def body(_, lohi):
        lo, hi = lohi
        mid = jnp.where(lo > 0.0, 0.5 * (lo + hi), hi * (1.0 / 256.0))
        cnt = jnp.sum(jnp.where(a >= mid, 1.0, 0.0), axis=-1, keepdims=True)
        ok = cnt >= cap
        return jnp.where(ok, mid, lo), jnp.where(ok, hi, mid)

    lo0 = jnp.zeros((ne, 1), F32)
    hi0 = jnp.maximum(2.0 * jnp.max(a, axis=-1, keepdims=True), SELECT_MIN_UPPER)
    lo, hi = lax.fori_loop(0, SELECT_BISECTIONS, body, (lo0, hi0))
    above = a >= hi
    band = jnp.logical_and(a >= lo, jnp.logical_not(above))
    n_above = jnp.sum(jnp.where(above, 1.0, 0.0), axis=-1, keepdims=True)
    band_rank = _prefix_sum_lanes(jnp.where(band, 1.0, 0.0), seq)
    sel = jnp.where(above, 1.0, jnp.where(band, jnp.where(band_rank <= cap - n_above, 1.0, 0.0), 0.0))
    pos = _prefix_sum_lanes(sel, seq) - 1.0
    chosen = sel > 0.5
    pos_ref[...] = jnp.where(chosen, pos, -1.0)
    gate_ref[...] = jnp.where(chosen, a, 0.0)


def _select(aff_rows, cap):
    n, s = aff_rows.shape
    spec = pl.BlockSpec((n, s), lambda i: (0, 0))
    return pl.pallas_call(
        functools.partial(_select_kernel, cap=cap, seq=s),
        grid=(1,),
        in_specs=[spec],
        out_specs=[spec, spec],
        out_shape=[jax.ShapeDtypeStruct((n, s), F32), jax.ShapeDtypeStruct((n, s), F32)],
        compiler_params=_params("arbitrary"),
        name="expert_select",
    )(aff_rows)


def _lane_spread(ne):
    blk = lax.broadcasted_iota(jnp.int32, (ne, ne * LANES), 1) // LANES
    return jnp.where(blk == lax.broadcasted_iota(jnp.int32, (ne, ne * LANES), 0), 1.0, 0.0).astype(BF16)


def _compact_kernel(pos_ref, idx_ref, *, cap):
    seq, ne = pos_ref.shape[1], pos_ref.shape[2]
    rt = COMBINE_BUILD_ROWS
    halves = cap // LANES
    spread = _lane_spread(ne)
    lane = lax.broadcasted_iota(jnp.int32, (rt, LANES), 1).astype(F32)
    tok0 = lax.broadcasted_iota(jnp.int32, (rt, LANES), 0).astype(F32)

    def body(i, accs):
        r0 = pl.multiple_of(i * rt, rt)
        pos = jnp.dot(pos_ref[0, pl.ds(r0, rt), :].astype(BF16), spread, preferred_element_type=F32)
        tok = tok0 + r0.astype(F32)
        out = []
        for e in range(ne):
            p = pos[:, e * LANES:(e + 1) * LANES]
            for h in range(halves):
                hit = lane + float(h * LANES) == p
                out.append(accs[e * halves + h] + jnp.sum(jnp.where(hit, tok, 0.0), axis=0, keepdims=True))
        return tuple(out)

    accs = lax.fori_loop(0, seq // rt, body, tuple(jnp.zeros((1, LANES), F32) for _ in range(ne * halves)))
    for e in range(ne):
        for h in range(halves):
            idx_ref[0, e:e + 1, h * LANES:(h + 1) * LANES] = accs[e * halves + h].astype(jnp.int32)


def _compact(pos_cols, cap):
    b, s, ne = pos_cols.shape
    return pl.pallas_call(
        functools.partial(_compact_kernel, cap=cap),
        grid=(b,),
        in_specs=[pl.BlockSpec((1, s, ne), lambda bi: (bi, 0, 0))],
        out_specs=pl.BlockSpec((1, ne, cap), lambda bi: (bi, 0, 0)),
        out_shape=jax.ShapeDtypeStruct((b, ne, cap), jnp.int32),
        compiler_params=_params("arbitrary"),
        name="expert_compact",
    )(pos_cols)


def _ffn_kernel(idx_ref, h2_hbm, wg_ref, wu_ref, wd_ref, ye_ref, x_scr, hid_scr, sem, *, nb, cap):
    e = pl.program_id(0)
    f = pl.program_id(1)
    ne = pl.num_programs(0)
    nf = pl.num_programs(1)
    rows = nb * cap
    per_step = rows // FFN_UP_STEPS
    tf = wg_ref.shape[-1]
    slot = lax.rem(e, 2)
    nslot = 1 - slot
    nxt = jnp.minimum(e + 1, ne - 1)

    def row_copy(expert, r, sl):
        row = idx_ref[expert * rows + r]
        return pltpu.make_async_copy(h2_hbm.at[pl.ds(row, 1), :], x_scr.at[sl, pl.ds(r, 1), :], sem.at[sl])

    def wait_rows(sl):
        pltpu.make_async_copy(x_scr.at[sl], x_scr.at[sl], sem.at[sl]).wait()

    @pl.when(jnp.logical_and(e == 0, f == 0))
    def _():
        def first(r, carry):
            row_copy(0, r, 0).start()
            return carry
        lax.fori_loop(0, rows, first, 0)

    for ff in range(FFN_UP_STEPS):
        for cur in range(2):
            @pl.when(jnp.logical_and(f == ff, slot == cur))
            def _(ff=ff, cur=cur):
                if ff == 0:
                    wait_rows(cur)
                for r in range(ff * per_step, (ff + 1) * per_step):
                    row_copy(nxt, r, 1 - cur).start(priority=r % 2)
                x = x_scr[cur].astype(BF16)
                g = jnp.dot(x, wg_ref[0, 0].astype(BF16), preferred_element_type=F32)
                u = jnp.dot(x, wu_ref[0, 0].astype(BF16), preferred_element_type=F32)
                hid_scr[:, ff * tf:(ff + 1) * tf] = ((g * jax.nn.sigmoid(g)) * u).astype(BF16)

    @pl.when(f >= FFN_UP_STEPS)
    def _():
        y = jnp.dot(hid_scr[...], wd_ref[0, 0].astype(BF16), preferred_element_type=F32)
        ye_ref[...] = y.reshape(nb, 1, cap, y.shape[-1]).astype(BF16)

    @pl.when(jnp.logical_and(e == ne - 1, f == nf - 1))
    def _():
        wait_rows(nslot)


def _ffn(idx, h2, w_gate, w_up, w_down, layer):
    b, s, d = h2.shape
    idx = idx + (jnp.arange(idx.shape[1], dtype=jnp.int32) // (idx.shape[1] // b) * s)[None, :]
    ne = idx.shape[0]
    cap = idx.shape[1] // b
    ff = w_gate.shape[-1]
    tf = ff // FFN_UP_STEPS
    dn = d // FFN_DOWN_STEPS
    up = lambda f: jnp.minimum(f, FFN_UP_STEPS - 1)
    down = lambda f: jnp.maximum(f - FFN_UP_STEPS, 0)
    grid_spec = pltpu.PrefetchScalarGridSpec(
        num_scalar_prefetch=1,
        grid=(ne, FFN_UP_STEPS + FFN_DOWN_STEPS),
        in_specs=[
            pl.BlockSpec(memory_space=pl.ANY),
            pl.BlockSpec((1, 1, d, tf), lambda e, f, idx_ref: (layer, e, 0, up(f))),
            pl.BlockSpec((1, 1, d, tf), lambda e, f, idx_ref: (layer, e, 0, up(f))),
            pl.BlockSpec((1, 1, ff, dn), lambda e, f, idx_ref: (layer, e, 0, down(f))),
        ],
        out_specs=pl.BlockSpec((b, 1, cap, dn), lambda e, f, idx_ref: (0, e, 0, down(f))),
        scratch_shapes=[
            pltpu.VMEM((2, b * cap, d), F32),
            pltpu.VMEM((b * cap, ff), BF16),
            pltpu.SemaphoreType.DMA((2,)),
        ],
    )
    return pl.pallas_call(
        functools.partial(_ffn_kernel, nb=b, cap=cap),
        grid_spec=grid_spec,
        out_shape=jax.ShapeDtypeStruct((b, ne, cap, d), BF16),
        compiler_params=_params("arbitrary", "arbitrary"),
        name="expert_ffn",
    )(idx.reshape(-1), h2.reshape(b * s, d), w_gate, w_up, w_down)


def _combine_kernel(pos_ref, gate_ref, ye_ref, x1_ref, g2_ref, o_ref, scat_scr, *, cap):
    j = pl.program_id(1)
    seq, ne = pos_ref.shape[1], pos_ref.shape[2]
    rt = COMBINE_BUILD_ROWS

    @pl.when(j == 0)
    def _():
        spread = _lane_spread(ne)
        lane = lax.broadcasted_iota(jnp.int32, (rt, LANES), 1).astype(F32)

        def build(i, carry):
            r0 = pl.multiple_of(i * rt, rt)
            pos = jnp.dot(pos_ref[0, pl.ds(r0, rt), :].astype(BF16), spread, preferred_element_type=F32)
            gate = jnp.dot(gate_ref[0, pl.ds(r0, rt), :].astype(BF16), spread, preferred_element_type=F32)
            for e in range(ne):
                p = pos[:, e * LANES:(e + 1) * LANES]
                g = gate[:, e * LANES:(e + 1) * LANES]
                for h in range(cap // LANES):
                    cols = slice(e * cap + h * LANES, e * cap + (h + 1) * LANES)
                    scat_scr[pl.ds(r0, rt), cols] = jnp.where(lane + float(h * LANES) == p, g, 0.0).astype(BF16)
            return carry

        lax.fori_loop(0, seq // rt, build, 0)

    acc = jnp.dot(scat_scr[...], ye_ref[0], preferred_element_type=F32)
    o_ref[0] = x1_ref[0] + g2_ref[0] * acc


def _combine(pos_cols, gate_cols, ye, x1, mod6):
    b, s, d = x1.shape
    ne, cap = ye.shape[1], ye.shape[2]
    tn = COMBINE_TN
    return pl.pallas_call(
        functools.partial(_combine_kernel, cap=cap),
        grid=(b, d // tn),
        in_specs=[
            pl.BlockSpec((1, s, ne), lambda bi, j: (bi, 0, 0)),
            pl.BlockSpec((1, s, ne), lambda bi, j: (bi, 0, 0)),
            pl.BlockSpec((1, ne * cap, tn), lambda bi, j: (bi, 0, j)),
            pl.BlockSpec((1, s, tn), lambda bi, j: (bi, 0, j)),
            pl.BlockSpec((1, 1, tn), lambda bi, j: (bi * 6 + 5, 0, j)),
        ],
        out_specs=pl.BlockSpec((1, s, tn), lambda bi, j: (bi, 0, j)),
        out_shape=jax.ShapeDtypeStruct((b, s, d), F32),
        scratch_shapes=[pltpu.VMEM((s, ne * cap), BF16)],
        compiler_params=_params("arbitrary", "arbitrary"),
        name="expert_combine",
    )(pos_cols, gate_cols, ye.reshape(b, ne * cap, d), x1, mod6)


def _rope_tables(seq):
    inv = 1.0 / (ROPE_THETA ** (jnp.arange(0, ATT_HEAD_DIM, 2, dtype=F32) / ATT_HEAD_DIM))
    ang = jnp.arange(seq, dtype=F32)[:, None] * inv[None, :]
    ang = jnp.concatenate([ang, ang], axis=-1)
    sign = jnp.where(jnp.arange(ATT_HEAD_DIM) < ATT_HEAD_DIM // 2, -1.0, 1.0).astype(F32)
    return jnp.cos(ang), jnp.sin(ang) * sign


def kernel(x, c, w_ada, b_ada, w_in, b_gates, q_gain, k_gain, sink, m_gain, w_out,
           w_router, w_gate, w_up, w_down):
    b, s, d = x.shape
    depth = w_ada.shape[0]
    cap = CAPACITY_FACTOR * s // N_EXPERTS
    cos, sin_signed = _rope_tables(s)
    c_pad = jnp.pad(c, ((0, SUBLANES - b), (0, 0)))
    mod = _ada(c_pad, w_ada, b_ada)
    w_in_bf = w_in.astype(BF16)
    for l in range(depth):
        mod6 = mod[l, :b].reshape(b * 6, 1, d)
        proj, gates = _inproj(x, mod6, w_in_bf, w_in[l, :, PROJ_WIDTH:].astype(BF16), l, cos, sin_signed,
                              q_gain[l].reshape(1, -1), k_gain[l].reshape(1, -1))
        att = _attention(proj, sink[l])
        m_out = _mlstm(proj, gates, b_gates[l].reshape(-1), m_gain[l])
        x1, h2, aff = _outproj(att, m_out, w_out[l].astype(BF16), x, mod6, w_router[l])
        pos_r, gate_r = _select(jnp.transpose(aff, (0, 2, 1)).reshape(b * N_EXPERTS, s), cap)
        to_cols = lambda t: jnp.transpose(t.reshape(b, N_EXPERTS, s), (0, 2, 1))
        pos_c = to_cols(pos_r)
        idx = jnp.transpose(_compact(pos_c, cap), (1, 0, 2)).reshape(N_EXPERTS, b * cap)
        ye = _ffn(idx, h2, w_gate, w_up, w_down, l)
        x = _combine(pos_c, to_cols(gate_r), ye, x1, mod6)
    return x
```
